```python
import math
import jax, jax.numpy as jnp
from jax import lax
import numpy as np

D_MODEL = 1024
BATCH = 8
SEQ = 2048
DEPTH = 4

N_HEADS_A = 8
HEAD_DIM_A = 64
WIDTH_A = N_HEADS_A * HEAD_DIM_A
DILATED_PATTERNS = ((128, 1), (512, 4), (2048, 16))
N_POOL_GROUPS = 4
POOL_WINDOWS = (2, 4, 8, 16)
WIDTH_B = D_MODEL // 2
POOL_GROUP_DIM = WIDTH_B // N_POOL_GROUPS
MIX_WIDTH_EVEN = WIDTH_A + WIDTH_B
IN_WIDTH_EVEN = 3 * WIDTH_A + WIDTH_B
REL_BUCKETS = 32
REL_MAX_DISTANCE = 2048
N_HEADS_C = 4
INNER_C = 2 * D_MODEL
HEAD_DIM_C = INNER_C // N_HEADS_C
CONV_WIDTH = 4
QKV_BLOCK = 4
MLSTM_CHUNK = 64
D_FF = 256 * ((8 * D_MODEL // 3 + 255) // 256)
FFN_RESIDUAL = 0.5
RMS_EPS = 1e-6
N_EVEN = (DEPTH + 1) // 2
N_ODD = DEPTH // 2

kernel_name = 'hybrid_dilated_pool_mlstm_macaron'


def rms_norm(x, gain):
    xf = x.astype(jnp.float32)
    y = xf * lax.rsqrt(jnp.mean(xf * xf, axis=-1, keepdims=True) + RMS_EPS)
    return (y * gain.astype(jnp.float32)).astype(x.dtype)


def swiglu(x, w_gate, w_up, w_down):
    return (jax.nn.silu(x @ w_gate) * (x @ w_up)) @ w_down


def t5_bucket(distance):
    max_exact = REL_BUCKETS // 2
    d = jnp.maximum(distance.astype(jnp.float32), 1.0)
    large = max_exact + (jnp.log(d / max_exact) / math.log(REL_MAX_DISTANCE / max_exact)
                         * (REL_BUCKETS - max_exact)).astype(jnp.int32)
    large = jnp.minimum(large, REL_BUCKETS - 1)
    return jnp.where(distance < max_exact, distance, large)


def dilated_band_attention(q, k, v, rel_bias, window, dilation):
    b, s, h, e = q.shape
    steps = window // dilation
    blk = steps
    span = dilation * blk
    s_pad = -(-s // span) * span
    n_sub = s_pad // dilation
    nb = n_sub // blk

    def to_blocks(t):
        t = jnp.pad(t, ((0, 0), (0, s_pad - s), (0, 0), (0, 0)))
        t = t.reshape(b, n_sub, dilation, h, e).transpose(0, 2, 3, 1, 4)
        return t.reshape(b, dilation, h, nb, blk, e)

    def with_prev(t):
        prev = jnp.pad(t, ((0, 0), (0, 0), (0, 0), (1, 0), (0, 0), (0, 0)))[:, :, :, :nb]
        return jnp.concatenate([prev, t], axis=4)

    def from_blocks(t):
        rest = t.shape[5:]
        t = jnp.moveaxis(t.reshape(b, dilation, h, n_sub, *rest), 3, 1)
        return t.reshape(b, s_pad, h, *rest)[:, :s]

    qb, kb, vb = to_blocks(q), to_blocks(k), to_blocks(v)
    kw, vw = with_prev(kb), with_prev(vb)
    kj = jnp.arange(2 * blk)[None, :]
    delta = (jnp.arange(blk)[:, None] + blk) - kj
    band = (delta >= 0) & (delta <= steps)
    valid = band[None] & ((jnp.arange(nb)[:, None, None] > 0) | (kj >= blk)[None])
    bias = rel_bias[t5_bucket(jnp.maximum(delta, 0) * dilation)].astype(jnp.float32)
    bias = jnp.transpose(bias, (2, 0, 1))[None, None, :, None]
    logits = jnp.einsum('bdhnqe,bdhnke->bdhnqk', qb, kw) + bias
    logits = jnp.where(valid[None, None, None], logits, -jnp.inf)
    m = jnp.max(logits, axis=-1, keepdims=True)
    p = jnp.exp(logits - m)
    denom = jnp.sum(p, axis=-1, keepdims=True)
    o = jnp.einsum('bdhnqk,bdhnke->bdhnqe', p, vw) / denom
    lse = (m + jnp.log(denom))[..., 0]
    return from_blocks(o), from_blocks(lse)


def multi_scale_pool(u, pool_w, pool_scale):
    b, s, _ = u.shape
    uf = u.astype(jnp.float32).reshape(b, s, N_POOL_GROUPS, POOL_GROUP_DIM)
    cs = jnp.cumsum(uf, axis=1)
    pos = jnp.arange(1, s + 1, dtype=jnp.float32)
    diffs = []
    for g, w in enumerate(POOL_WINDOWS):
        c = cs[:, :, g]
        lagged = jnp.pad(c, ((0, 0), (w, 0), (0, 0)))[:, :s]
        mean = (c - lagged) / jnp.minimum(pos, float(w))[:, None]
        diffs.append(mean - uf[:, :, g])
    y = jnp.einsum('bsgc,gcd->bsgd', jnp.stack(diffs, axis=2), pool_w.astype(jnp.float32))
    y = y.reshape(b, s, WIDTH_B) * pool_scale.astype(jnp.float32)
    return y.astype(u.dtype)


def dilated_pool_mixer(xn, rel_bias, w_in, q_gain, k_gain, pool_w, pool_scale, w_out):
    b, s, _ = xn.shape
    proj = xn @ w_in
    q, k, v, u = jnp.split(proj, [WIDTH_A, 2 * WIDTH_A, 3 * WIDTH_A], axis=-1)
    heads = lambda t: t.reshape(b, s, N_HEADS_A, HEAD_DIM_A).astype(jnp.float32)
    q = rms_norm(heads(q), q_gain) * HEAD_DIM_A ** -0.5
    k = rms_norm(heads(k), k_gain)
    v = heads(v)
    outs, lses = [], []
    for window, dilation in DILATED_PATTERNS:
        o, lse = dilated_band_attention(q, k, v, rel_bias, window, dilation)
        outs.append(o)
        lses.append(lse)
    wts = jax.nn.softmax(jnp.stack(lses), axis=0)
    attn = jnp.sum(wts[..., None] * jnp.stack(outs), axis=0).reshape(b, s, WIDTH_A)
    pooled = multi_scale_pool(u, pool_w, pool_scale)
    mixed = jnp.concatenate([attn.astype(xn.dtype), pooled], axis=-1)
    return mixed @ w_out


def mlstm_chunkwise(q, k, v, ig, fg):
    b, h, s, e = q.shape
    L = MLSTM_CHUNK
    nc = -(-s // L)
    pad = nc * L - s
    pad4 = lambda t: jnp.pad(t, ((0, 0), (0, 0), (0, pad), (0, 0)))
    pad3 = lambda t: jnp.pad(t, ((0, 0), (0, 0), (0, pad)))
    chunks = lambda t: jnp.moveaxis(t.reshape(b, h, nc, L, *t.shape[3:]), 2, 0)
    qc, kc, vc = chunks(pad4(q)), chunks(pad4(k)), chunks(pad4(v))
    ic = chunks(pad3(ig))
    bcum = jnp.cumsum(chunks(pad3(jax.nn.log_sigmoid(fg))), axis=-1)
    causal = jnp.tril(jnp.ones((L, L), dtype=bool))
    d_intra = jnp.where(causal, bcum[..., :, None] - bcum[..., None, :] + ic[..., None, :], -jnp.inf)
    b_last = bcum[..., -1]
    a_state = b_last[..., None] - bcum + ic

    def step(carry, xs):
        C, n, m = carry
        q_, k_, v_, dm, bc, a, bl = xs
        inter = bc + m[..., None]
        m_t = jnp.maximum(inter, jnp.max(dm, axis=-1))
        w_inter = jnp.exp(inter - m_t)
        s_qk = jnp.einsum('bhte,bhse->bhts', q_, k_) * jnp.exp(dm - m_t[..., None])
        num = (w_inter[..., None] * jnp.einsum('bhte,bhef->bhtf', q_, C)
               + jnp.einsum('bhts,bhsf->bhtf', s_qk, v_))
        den = w_inter * jnp.einsum('bhte,bhe->bht', q_, n) + jnp.sum(s_qk, axis=-1)
        h_out = num / jnp.maximum(jnp.abs(den), jnp.exp(-m_t))[..., None]
        m_new = jnp.maximum(bl + m, jnp.max(a, axis=-1))
        decay = jnp.exp(bl + m - m_new)
        w_s = jnp.exp(a - m_new[..., None])
        C = decay[..., None, None] * C + jnp.einsum('bhse,bhsf->bhef', k_ * w_s[..., None], v_)
        n = decay[..., None] * n + jnp.einsum('bhs,bhse->bhe', w_s, k_)
        return (C, n, m_new), h_out

    init = (jnp.zeros((b, h, e, e), jnp.float32), jnp.zeros((b, h, e), jnp.float32),
            jnp.zeros((b, h), jnp.float32))
    _, hs = lax.scan(step, init, (qc, kc, vc, d_intra, bcum, a_state, b_last))
    return jnp.moveaxis(hs, 0, 2).reshape(b, h, nc * L, e)[:, :, :s]


def mlstm_mixer(xn, w_in, conv_w, conv_b, wq, wk, wv, w_gates, b_gates, skip, out_gain, w_out):
    b, s, _ = xn.shape
    x_m, z = jnp.split(xn @ w_in, 2, axis=-1)
    x_c = lax.conv_general_dilated(x_m, conv_w[:, None, :], window_strides=(1,),
                                   padding=[(CONV_WIDTH - 1, 0)],
                                   dimension_numbers=('NWC', 'WIO', 'NWC'),
                                   feature_group_count=INNER_C)
    x_c = jax.nn.silu(x_c + conv_b)
    blockdiag = lambda t, w: jnp.einsum('bsgi,gio->bsgo', t.reshape(b, s, -1, QKV_BLOCK), w).reshape(b, s, INNER_C)
    q, k, v = blockdiag(x_c, wq), blockdiag(x_c, wk), blockdiag(x_m, wv)
    gates = (jnp.concatenate([q, k, v], axis=-1) @ w_gates + b_gates).astype(jnp.float32)
    ig = gates[..., :N_HEADS_C].transpose(0, 2, 1)
    fg = gates[..., N_HEADS_C:].transpose(0, 2, 1)
    heads = lambda t: t.reshape(b, s, N_HEADS_C, HEAD_DIM_C).transpose(0, 2, 1, 3).astype(jnp.float32)
    hh = mlstm_chunkwise(heads(q), heads(k) * HEAD_DIM_C ** -0.5, heads(v), ig, fg)
    hh = rms_norm(hh.transpose(0, 2, 1, 3), out_gain.reshape(N_HEADS_C, HEAD_DIM_C))
    hh = hh.reshape(b, s, INNER_C).astype(xn.dtype)
    y = (hh + skip * x_c) * jax.nn.silu(z)
    return y @ w_out


def setup_inputs(seed: int = 0) -> dict:
    key = jax.random.key(seed)
    ks = iter(jax.random.split(key, 40))
    nrm = lambda shape, fan_in: jax.random.normal(next(ks), shape, jnp.float32) * fan_in ** -0.5
    gain = lambda shape: 1.0 + 0.02 * jax.random.normal(next(ks), shape, jnp.float32)
    x = jax.random.normal(next(ks), (BATCH, SEQ, D_MODEL), jnp.float32)
    rel_bias = 0.5 * jax.random.normal(next(ks), (REL_BUCKETS, N_HEADS_A), jnp.float32)
    norm_gains = gain((DEPTH, 3, D_MODEL))
    ffn_w_gate = nrm((DEPTH, 2, D_MODEL, D_FF), D_MODEL)
    ffn_w_up = nrm((DEPTH, 2, D_MODEL, D_FF), D_MODEL)
    ffn_w_down = nrm((DEPTH, 2, D_FF, D_MODEL), D_FF)
    ev_w_in = nrm((N_EVEN, D_MODEL, IN_WIDTH_EVEN), D_MODEL)
    ev_q_gain = gain((N_EVEN, HEAD_DIM_A))
    ev_k_gain = gain((N_EVEN, HEAD_DIM_A))
    ev_pool_w = nrm((N_EVEN, N_POOL_GROUPS, POOL_GROUP_DIM, POOL_GROUP_DIM), POOL_GROUP_DIM)
    ev_pool_scale = gain((N_EVEN, WIDTH_B))
    ev_w_out = nrm((N_EVEN, MIX_WIDTH_EVEN, D_MODEL), MIX_WIDTH_EVEN)
    od_w_in = nrm((N_ODD, D_MODEL, 2 * INNER_C), D_MODEL)
    od_conv_w = nrm((N_ODD, CONV_WIDTH, INNER_C), CONV_WIDTH)
    od_conv_b = 0.02 * jax.random.normal(next(ks), (N_ODD, INNER_C), jnp.float32)
    od_wq = nrm((N_ODD, INNER_C // QKV_BLOCK, QKV_BLOCK, QKV_BLOCK), QKV_BLOCK)
    od_wk = nrm((N_ODD, INNER_C // QKV_BLOCK, QKV_BLOCK, QKV_BLOCK), QKV_BLOCK)
    od_wv = nrm((N_ODD, INNER_C // QKV_BLOCK, QKV_BLOCK, QKV_BLOCK), QKV_BLOCK)
    od_w_gates = nrm((N_ODD, 3 * INNER_C, 2 * N_HEADS_C), 3 * INNER_C)
    od_b_gates = jnp.concatenate([
        0.1 * jax.random.normal(next(ks), (N_ODD, N_HEADS_C), jnp.float32),
        jax.random.uniform(next(ks), (N_ODD, N_HEADS_C), jnp.float32, 3.0, 6.0)],
        axis=-1)
    od_skip = gain((N_ODD, INNER_C))
    od_out_gain = gain((N_ODD, INNER_C))
    od_w_out = nrm((N_ODD, INNER_C, D_MODEL), INNER_C)
    return {'x': x, 'rel_bias': rel_bias, 'norm_gains': norm_gains,
            'ffn_w_gate': ffn_w_gate, 'ffn_w_up': ffn_w_up, 'ffn_w_down': ffn_w_down,
            'ev_w_in': ev_w_in, 'ev_q_gain': ev_q_gain, 'ev_k_gain': ev_k_gain,
            'ev_pool_w': ev_pool_w, 'ev_pool_scale': ev_pool_scale, 'ev_w_out': ev_w_out,
            'od_w_in': od_w_in, 'od_conv_w': od_conv_w, 'od_conv_b': od_conv_b,
            'od_wq': od_wq, 'od_wk': od_wk, 'od_wv': od_wv,
            'od_w_gates': od_w_gates, 'od_b_gates': od_b_gates, 'od_skip': od_skip,
            'od_out_gain': od_out_gain, 'od_w_out': od_w_out}


def reference(x, rel_bias, norm_gains, ffn_w_gate, ffn_w_up, ffn_w_down,
              ev_w_in, ev_q_gain, ev_k_gain, ev_pool_w, ev_pool_scale, ev_w_out,
              od_w_in, od_conv_w, od_conv_b, od_wq, od_wk, od_wv,
              od_w_gates, od_b_gates, od_skip, od_out_gain, od_w_out):
    for layer in range(DEPTH):
        g = norm_gains[layer]
        x = x + FFN_RESIDUAL * swiglu(rms_norm(x, g[0]), ffn_w_gate[layer, 0],
                                      ffn_w_up[layer, 0], ffn_w_down[layer, 0])
        xn = rms_norm(x, g[1])
        if layer % 2 == 0:
            i = layer // 2
            x = x + dilated_pool_mixer(xn, rel_bias, ev_w_in[i], ev_q_gain[i], ev_k_gain[i],
                                       ev_pool_w[i], ev_pool_scale[i], ev_w_out[i])
        else:
            i = layer // 2
            x = x + mlstm_mixer(xn, od_w_in[i], od_conv_w[i], od_conv_b[i], od_wq[i], od_wk[i],
                                od_wv[i], od_w_gates[i], od_b_gates[i], od_skip[i],
                                od_out_gain[i], od_w_out[i])
        x = x + FFN_RESIDUAL * swiglu(rms_norm(x, g[2]), ffn_w_gate[layer, 1],
                                      ffn_w_up[layer, 1], ffn_w_down[layer, 1])
    return x
```

```python
import functools
import math

import numpy as np
import jax
import jax.numpy as jnp
from jax import lax
from jax.experimental import pallas as pl
from jax.experimental.pallas import tpu as pltpu

F32 = jnp.float32
BF16 = jnp.bfloat16

D_MODEL = 1024
DEPTH = 4
N_HEADS_A = 8
HEAD_DIM_A = 64
WIDTH_A = N_HEADS_A * HEAD_DIM_A
DILATED_PATTERNS = ((128, 1), (512, 4), (2048, 16))
N_POOL_GROUPS = 4
POOL_WINDOWS = (2, 4, 8, 16)
WIDTH_B = D_MODEL // 2
POOL_GROUP_DIM = WIDTH_B // N_POOL_GROUPS
IN_WIDTH_EVEN = 3 * WIDTH_A + WIDTH_B
REL_BUCKETS = 32
REL_MAX_DISTANCE = 2048
N_HEADS_C = 4
INNER_C = 2 * D_MODEL
HEAD_DIM_C = INNER_C // N_HEADS_C
CONV_WIDTH = 4
QKV_BLOCK = 4
D_FF = 256 * ((8 * D_MODEL // 3 + 255) // 256)
FFN_RESIDUAL = 0.5
RMS_EPS = 1e-6

LANES = 128
VMEM_LIMIT = 56 * 1024 * 1024
MASKED = -1e30

TOKEN_TILE = 512
ATTN_TILE = 256
MLSTM_CHUNK = 256
CONV_CHANNEL_TILE = 256


def _params(*sem):
    return pltpu.CompilerParams(dimension_semantics=sem, vmem_limit_bytes=VMEM_LIMIT)


def _resident(shape):
    nd = len(shape)
    return pl.BlockSpec(shape, lambda *_: (0,) * nd, pipeline_mode=pl.Buffered(1))


def _rms(x, gain):
    return x * lax.rsqrt(jnp.mean(x * x, axis=-1, keepdims=True) + RMS_EPS) * gain


def _silu(x):
    return x / (1.0 + jnp.exp(-x))


def _dot(a, b):
    return jnp.dot(a, b, preferred_element_type=F32)


def _ffn_kernel(x_ref, g_ref, wg_ref, wu_ref, wd_ref, o_ref, *, ff_tile):
    x = x_ref[...]
    xn = _rms(x, g_ref[...]).astype(BF16)
    acc = None
    for c in range(D_FF // ff_tile):
        sl = slice(c * ff_tile, (c + 1) * ff_tile)
        gate = _dot(xn, wg_ref[:, sl])
        up = _dot(xn, wu_ref[:, sl])
        h = (_silu(gate) * up).astype(BF16)
        y = _dot(h, wd_ref[sl, :])
        acc = y if acc is None else acc + y
    o_ref[...] = x + FFN_RESIDUAL * acc


def _ffn(x, gain, w_gate, w_up, w_down, *, ff_tile=D_FF):
    t = x.shape[0]
    tm = TOKEN_TILE
    row = pl.BlockSpec((tm, D_MODEL), lambda i: (i, 0))
    return pl.pallas_call(
        functools.partial(_ffn_kernel, ff_tile=ff_tile),
        grid=(t // tm,),
        in_specs=[row, _resident((1, D_MODEL)), _resident((D_MODEL, D_FF)),
                  _resident((D_MODEL, D_FF)), _resident((D_FF, D_MODEL))],
        out_specs=row,
        out_shape=jax.ShapeDtypeStruct((t, D_MODEL), F32),
        compiler_params=_params("parallel"),
        name="ffn",
    )(x, gain.reshape(1, D_MODEL), w_gate, w_up, w_down)


def _head_sum_matrix():
    r = lax.broadcasted_iota(jnp.int32, (LANES, LANES), 0) // HEAD_DIM_A
    c = lax.broadcasted_iota(jnp.int32, (LANES, LANES), 1) // HEAD_DIM_A
    return jnp.where(r == c, 1.0, 0.0).astype(BF16)


def _head_norm(t, same_head, gain, scale):
    sq = t * t
    hi = sq.astype(BF16)
    lo = (sq - hi.astype(F32)).astype(BF16)
    ss = _dot(hi, same_head) + _dot(lo, same_head)
    return t * lax.rsqrt(ss * (1.0 / HEAD_DIM_A) + RMS_EPS) * (gain * scale)


def _proj_even_kernel(x_ref, g_ref, w_ref, qg_ref, kg_ref, q_ref, k_ref, v_ref, u_ref):
    xn = _rms(x_ref[...], g_ref[...]).astype(BF16)
    proj = _dot(xn, w_ref[...])
    same_head = _head_sum_matrix()
    for j in range(WIDTH_A // LANES):
        sl = slice(j * LANES, (j + 1) * LANES)
        q_ref[:, sl] = _head_norm(proj[:, sl], same_head, qg_ref[...],
                                  HEAD_DIM_A ** -0.5).astype(BF16)
        ks = slice(WIDTH_A + j * LANES, WIDTH_A + (j + 1) * LANES)
        k_ref[:, sl] = _head_norm(proj[:, ks], same_head, kg_ref[...], 1.0).astype(BF16)
    v_ref[...] = proj[:, 2 * WIDTH_A:3 * WIDTH_A].astype(BF16)
    u_ref[...] = proj[:, 3 * WIDTH_A:]


def _proj_even(x, gain, w_in, q_gain, k_gain):
    t = x.shape[0]
    tm = TOKEN_TILE
    row = lambda n: pl.BlockSpec((tm, n), lambda i: (i, 0))
    pair = lambda g: jnp.tile(g, LANES // HEAD_DIM_A).reshape(1, LANES)
    return pl.pallas_call(
        _proj_even_kernel,
        grid=(t // tm,),
        in_specs=[row(D_MODEL), _resident((1, D_MODEL)), _resident((D_MODEL, IN_WIDTH_EVEN)),
                  _resident((1, LANES)), _resident((1, LANES))],
        out_specs=[row(WIDTH_A), row(WIDTH_A), row(WIDTH_A), row(WIDTH_B)],
        out_shape=[jax.ShapeDtypeStruct((t, WIDTH_A), BF16)] * 3
        + [jax.ShapeDtypeStruct((t, WIDTH_B), F32)],
        compiler_params=_params("parallel"),
        name="proj_even",
    )(x, gain.reshape(1, D_MODEL), w_in, pair(q_gain), pair(k_gain))


def _t5_bucket(distance):
    max_exact = REL_BUCKETS // 2
    d = jnp.maximum(distance.astype(F32), 1.0)
    large = max_exact + (jnp.log(d / max_exact) / math.log(REL_MAX_DISTANCE / max_exact)
                         * (REL_BUCKETS - max_exact)).astype(jnp.int32)
    large = jnp.minimum(large, REL_BUCKETS - 1)
    return jnp.where(distance < max_exact, distance, large)


def _attn_bias_tiles(rel_bias, seq, tile):
    dist = np.arange(seq)
    mult = np.zeros(seq, np.int64)
    for window, dilation in DILATED_PATTERNS:
        mult += (dist % dilation == 0) & (dist <= window)
    log_mult = np.where(mult > 0, np.log(np.maximum(mult, 1)), MASKED).astype(np.float32)
    per_dist = rel_bias[_t5_bucket(jnp.asarray(dist, jnp.int32))].astype(F32)
    per_dist = jnp.where(jnp.asarray(mult > 0)[:, None], per_dist + log_mult[:, None], MASKED)
    nbd = seq // tile
    idx = (np.arange(nbd)[:, None, None] * tile + np.arange(tile)[None, :, None]
           - np.arange(tile)[None, None, :])
    tiles = jnp.where(jnp.asarray(idx >= 0)[..., None],
                      per_dist[np.clip(idx, 0, seq - 1)], MASKED)
    tiles = jnp.transpose(tiles, (3, 0, 1, 2))
    return tiles.reshape(N_HEADS_A // 2, 2, nbd, tile, tile)


def _attn_kernel(q_ref, k_ref, v_ref, tab_ref, o_ref, *, tile):
    qi = pl.program_id(2)
    q = q_ref[0]
    first = lax.broadcasted_iota(jnp.int32, q.shape, 1) < HEAD_DIM_A
    zero = jnp.zeros_like(q)
    q_heads = (jnp.where(first, q, zero), jnp.where(first, zero, q))

    def body(kj, carry):
        start = pl.multiple_of(kj * tile, tile)
        k = k_ref[0, pl.ds(start, tile), :]
        v = v_ref[0, pl.ds(start, tile), :]
        out = []
        for hh in range(2):
            m, l, acc = carry[hh]
            s = lax.dot_general(q_heads[hh], k, (((1,), (1,)), ((), ())),
                                preferred_element_type=F32)
            s = s + tab_ref[0, hh, qi - kj]
            m_new = jnp.maximum(m, jnp.max(s, axis=-1, keepdims=True))
            alpha = jnp.exp(m - m_new)
            p = jnp.exp(s - m_new)
            l = alpha * l + jnp.sum(p, axis=-1, keepdims=True)
            acc = alpha * acc + _dot(p.astype(BF16), v)
            out.append((m_new, l, acc))
        return tuple(out)

    init = tuple((jnp.full((tile, 1), MASKED, F32), jnp.zeros((tile, 1), F32),
                  jnp.zeros((tile, LANES), F32)) for _ in range(2))
    (_, l0, a0), (_, l1, a1) = lax.fori_loop(0, qi + 1, body, init)
    o_ref[0] = jnp.where(first, a0 / l0, a1 / l1).astype(BF16)


def _attention(q, k, v, bias_tiles):
    b, s, _ = q.shape
    tile = ATTN_TILE
    hp = WIDTH_A // LANES
    nbd = s // tile
    seq_spec = pl.BlockSpec((1, s, LANES), lambda bi, h, i: (bi, 0, h))
    blk_spec = pl.BlockSpec((1, tile, LANES), lambda bi, h, i: (bi, i, h))
    return pl.pallas_call(
        functools.partial(_attn_kernel, tile=tile),
        grid=(b, hp, nbd),
        in_specs=[blk_spec, seq_spec, seq_spec,
                  pl.BlockSpec((1, 2, nbd, tile, tile), lambda bi, h, i: (h, 0, 0, 0, 0))],
        out_specs=blk_spec,
        out_shape=jax.ShapeDtypeStruct((b, s, WIDTH_A), BF16),
        compiler_params=_params("parallel", "parallel", "arbitrary"),
        name="dilated_attention",
    )(q, k, v, bias_tiles)


def _shift_rows(a, row, shift):
    return jnp.where(row >= shift, pltpu.roll(a, shift, 0), 0.0)


def _pool_kernel(u_ref, w_ref, sc_ref, o_ref):
    g = pl.program_id(1)
    u = u_ref[0]
    row = lax.broadcasted_iota(jnp.int32, u.shape, 0)
    sums = [u]
    for half in (1, 2, 4, 8):
        sums.append(sums[-1] + _shift_rows(sums[-1], row, half))
    total = sums[4]
    window = jnp.int32(POOL_WINDOWS[3])
    for gi in (2, 1, 0):
        total = jnp.where(g == gi, sums[gi + 1], total)
        window = jnp.where(g == gi, POOL_WINDOWS[gi], window)
    count = jnp.minimum(row + 1, window).astype(F32)
    diff = total / count - u
    o_ref[0] = (_dot(diff.astype(BF16), w_ref[0]) * sc_ref[...]).astype(BF16)


def _pool(u, pool_w, pool_scale):
    b, s, _ = u.shape
    assert POOL_WINDOWS == (2, 4, 8, 16)
    spec = pl.BlockSpec((1, s, POOL_GROUP_DIM), lambda bi, g: (bi, 0, g))
    return pl.pallas_call(
        _pool_kernel,
        grid=(b, N_POOL_GROUPS),
        in_specs=[spec,
                  pl.BlockSpec((1, POOL_GROUP_DIM, POOL_GROUP_DIM), lambda bi, g: (g, 0, 0)),
                  pl.BlockSpec((1, POOL_GROUP_DIM), lambda bi, g: (0, g))],
        out_specs=spec,
        out_shape=jax.ShapeDtypeStruct((b, s, WIDTH_B), BF16),
        compiler_params=_params("parallel", "parallel"),
        name="multi_scale_pool",
    )(u, pool_w, pool_scale.reshape(1, WIDTH_B))


def _out_even_kernel(x_ref, a_ref, p_ref, wa_ref, wp_ref, o_ref):
    o_ref[...] = x_ref[...] + _dot(a_ref[...], wa_ref[...]) + _dot(p_ref[...], wp_ref[...])


def _out_even(x, attn, pooled, w_attn, w_pool):
    t = x.shape[0]
    tm = TOKEN_TILE
    row = lambda n: pl.BlockSpec((tm, n), lambda i: (i, 0))
    return pl.pallas_call(
        _out_even_kernel,
        grid=(t // tm,),
        in_specs=[row(D_MODEL), row(WIDTH_A), row(WIDTH_B),
                  _resident((WIDTH_A, D_MODEL)), _resident((WIDTH_B, D_MODEL))],
        out_specs=row(D_MODEL),
        out_shape=jax.ShapeDtypeStruct((t, D_MODEL), F32),
        compiler_params=_params("parallel"),
        name="out_even",
    )(x, attn, pooled, w_attn, w_pool)


def _proj_odd_kernel(x_ref, g_ref, w_ref, xm_ref, z_ref):
    xn = _rms(x_ref[...], g_ref[...]).astype(BF16)
    xm_ref[...] = _dot(xn, w_ref[:, :INNER_C])
    z_ref[...] = _dot(xn, w_ref[:, INNER_C:]).astype(BF16)


def _proj_odd(x, gain, w_in):
    t = x.shape[0]
    tm = TOKEN_TILE
    row = lambda n: pl.BlockSpec((tm, n), lambda i: (i, 0))
    return pl.pallas_call(
        _proj_odd_kernel,
        grid=(t // tm,),
        in_specs=[row(D_MODEL), _resident((1, D_MODEL)), _resident((D_MODEL, 2 * INNER_C))],
        out_specs=[row(INNER_C), row(INNER_C)],
        out_shape=[jax.ShapeDtypeStruct((t, INNER_C), F32),
                   jax.ShapeDtypeStruct((t, INNER_C), BF16)],
        compiler_params=_params("parallel"),
        name="proj_odd",
    )(x, gain.reshape(1, D_MODEL), w_in)


def _conv_qkv_kernel(xm_ref, cw_ref, cb_ref, wq_ref, wk_ref, wv_ref, wg_ref, bg_ref,
                     xc_ref, q_ref, k_ref, v_ref, gates_ref):
    j = pl.program_id(1)
    xm = xm_ref[0]
    row = lax.broadcasted_iota(jnp.int32, xm.shape, 0)
    conv = xm * cw_ref[CONV_WIDTH - 1:CONV_WIDTH, :]
    for back in range(1, CONV_WIDTH):
        tap = CONV_WIDTH - 1 - back
        conv = conv + _shift_rows(xm, row, back) * cw_ref[tap:tap + 1, :]
    xc = _silu(conv + cb_ref[...])
    xc_ref[0] = xc.astype(BF16)
    xc16 = xc.astype(BF16)
    xm16 = xm.astype(BF16)

    @pl.when(j == 0)
    def _():
        gates_ref[0] = jnp.broadcast_to(bg_ref[...], gates_ref.shape[1:])

    gates = gates_ref[0]
    for gi in range(xm.shape[1] // LANES):
        sl = slice(gi * LANES, (gi + 1) * LANES)
        q = _dot(xc16[:, sl], wq_ref[gi])
        k = _dot(xc16[:, sl], wk_ref[gi])
        v = _dot(xm16[:, sl], wv_ref[gi])
        q_ref[0, :, sl] = q.astype(BF16)
        k_ref[0, :, sl] = (k * HEAD_DIM_C ** -0.5).astype(BF16)
        v_ref[0, :, sl] = v.astype(BF16)
        gates = (gates + _dot(q.astype(BF16), wg_ref[0, sl, :])
                 + _dot(k.astype(BF16), wg_ref[1, sl, :])
                 + _dot(v.astype(BF16), wg_ref[2, sl, :]))
    gates_ref[0] = gates


def _block_diag_tiles(w):
    per_tile = LANES // QKV_BLOCK
    w = w.reshape(-1, per_tile, QKV_BLOCK, QKV_BLOCK)
    eye = jnp.eye(per_tile, dtype=w.dtype)
    dense = jnp.einsum("tgio,gh->tgiho", w, eye)
    return dense.reshape(-1, LANES, LANES)


def _conv_qkv(x_m, conv_w, conv_b, wq, wk, wv, w_gates, b_gates):
    b, s, c = x_m.shape
    tc = CONV_CHANNEL_TILE
    nt = tc // LANES
    chan = pl.BlockSpec((1, s, tc), lambda bi, j: (bi, 0, j))
    bd = pl.BlockSpec((nt, LANES, LANES), lambda bi, j: (j, 0, 0))
    n_gates = 2 * N_HEADS_C
    wg = jnp.pad(w_gates.reshape(3, c, n_gates), ((0, 0), (0, 0), (0, LANES - n_gates)))
    bg = jnp.pad(b_gates.reshape(1, n_gates), ((0, 0), (0, LANES - n_gates)))
    act = jax.ShapeDtypeStruct((b, s, c), BF16)
    return pl.pallas_call(
        _conv_qkv_kernel,
        grid=(b, c // tc),
        in_specs=[chan,
                  pl.BlockSpec((CONV_WIDTH, tc), lambda bi, j: (0, j)),
                  pl.BlockSpec((1, tc), lambda bi, j: (0, j)),
                  bd, bd, bd,
                  pl.BlockSpec((3, tc, LANES), lambda bi, j: (0, j, 0)),
                  pl.BlockSpec((1, LANES), lambda bi, j: (0, 0))],
        out_specs=[chan, chan, chan, chan,
                   pl.BlockSpec((1, s, LANES), lambda bi, j: (bi, 0, 0))],
        out_shape=[act, act, act, act, jax.ShapeDtypeStruct((b, s, LANES), F32)],
        compiler_params=_params("parallel", "arbitrary"),
        name="conv_qkv_gates",
    )(x_m, conv_w, conv_b.reshape(1, c),
      _block_diag_tiles(wq).astype(BF16), _block_diag_tiles(wk).astype(BF16),
      _block_diag_tiles(wv).astype(BF16), wg.astype(BF16), bg)


def _log_sigmoid(x):
    return jnp.minimum(x, 0.0) - jnp.log(1.0 + jnp.exp(-jnp.abs(x)))


def _mlstm_kernel(q_ref, k_ref, v_ref, gc_ref, gr_ref, og_ref, o_ref, c_ref, n_ref, m_ref):
    @pl.when(pl.program_id(2) == 0)
    def _():
        c_ref[...] = jnp.zeros_like(c_ref)
        n_ref[...] = jnp.zeros_like(n_ref)
        m_ref[...] = jnp.zeros_like(m_ref)

    q, k, v = q_ref[0], k_ref[0], v_ref[0]
    chunk = q.shape[0]
    gcol, grow = gc_ref[0, 0], gr_ref[0, 0]
    ig_c, lf_c = gcol[:, 0:1], _log_sigmoid(gcol[:, 1:2])
    ig_r, lf_r = grow[0:1, :], _log_sigmoid(grow[1:2, :])
    t_idx = lax.broadcasted_iota(jnp.int32, (chunk, chunk), 0)
    s_idx = lax.broadcasted_iota(jnp.int32, (chunk, chunk), 1)
    causal = s_idx <= t_idx
    bcum_c = jnp.sum(jnp.where(causal, lf_r, 0.0), axis=1, keepdims=True)
    bcum_r = jnp.sum(jnp.where(t_idx <= s_idx, lf_c, 0.0), axis=0, keepdims=True)
    b_last = jnp.sum(lf_r, axis=1, keepdims=True)
    d = jnp.where(causal, bcum_c - bcum_r + ig_r, MASKED)
    m_prev = m_ref[...]
    inter = bcum_c + m_prev
    m_t = jnp.maximum(inter, jnp.max(d, axis=1, keepdims=True))
    w_inter = jnp.exp(inter - m_t)
    s_qk = lax.dot_general(q, k, (((1,), (1,)), ((), ())), preferred_element_type=F32)
    s_qk = s_qk * jnp.exp(d - m_t)
    c_state = c_ref[...]
    n_state = n_ref[...]
    kf = k.astype(F32)
    num = w_inter * _dot(q, c_state.astype(BF16)) + _dot(s_qk.astype(BF16), v)
    den = (w_inter * jnp.sum(q.astype(F32) * n_state, axis=1, keepdims=True)
           + jnp.sum(s_qk, axis=1, keepdims=True))
    h = num / jnp.maximum(jnp.abs(den), jnp.exp(-m_t))

    a_r = b_last - bcum_r + ig_r
    a_c = b_last - bcum_c + ig_c
    m_new = jnp.maximum(b_last + m_prev, jnp.max(a_r, axis=1, keepdims=True))
    decay = jnp.exp(b_last + m_prev - m_new)
    kw = kf * jnp.exp(a_c - m_new)
    c_ref[...] = decay * c_state + lax.dot_general(
        kw.astype(BF16), v, (((0,), (0,)), ((), ())), preferred_element_type=F32)
    n_ref[...] = decay * n_state + jnp.sum(kw, axis=0, keepdims=True)
    m_ref[...] = m_new

    o_ref[0] = _rms(h, og_ref[0]).astype(BF16)


def _mlstm(q, k, v, gates, out_gain):
    b, s, _ = q.shape
    chunk = MLSTM_CHUNK
    e = HEAD_DIM_C
    gh = gates[..., :2 * N_HEADS_C].reshape(b, s, 2, N_HEADS_C)
    g_col = jnp.transpose(gh, (0, 3, 1, 2))
    g_row = jnp.transpose(gh, (0, 3, 2, 1))
    head = pl.BlockSpec((1, chunk, e), lambda bi, h, c: (bi, c, h))
    return pl.pallas_call(
        _mlstm_kernel,
        grid=(b, N_HEADS_C, s // chunk),
        in_specs=[head, head, head,
                  pl.BlockSpec((1, 1, chunk, 2), lambda bi, h, c: (bi, h, c, 0)),
                  pl.BlockSpec((1, 1, 2, chunk), lambda bi, h, c: (bi, h, 0, c)),
                  pl.BlockSpec((1, 1, e), lambda bi, h, c: (h, 0, 0))],
        out_specs=head,
        out_shape=jax.ShapeDtypeStruct((b, s, INNER_C), BF16),
        scratch_shapes=[pltpu.VMEM((e, e), F32), pltpu.VMEM((1, e), F32),
                        pltpu.VMEM((1, 1), F32)],
        compiler_params=_params("parallel", "parallel", "arbitrary"),
        name="mlstm_chunkwise",
    )(q, k, v, g_col, g_row, out_gain.reshape(N_HEADS_C, 1, e))


def _out_odd_kernel(x_ref, h_ref, xc_ref, z_ref, skip_ref, w_ref, o_ref):
    y = (h_ref[...].astype(F32) + skip_ref[...] * xc_ref[...].astype(F32))
    y = y * _silu(z_ref[...].astype(F32))
    o_ref[...] = x_ref[...] + _dot(y.astype(BF16), w_ref[...])


def _out_odd(x, hh, xc, z, skip, w_out):
    t = x.shape[0]
    tm = TOKEN_TILE
    row = lambda n: pl.BlockSpec((tm, n), lambda i: (i, 0))
    return pl.pallas_call(
        _out_odd_kernel,
        grid=(t // tm,),
        in_specs=[row(D_MODEL), row(INNER_C), row(INNER_C), row(INNER_C),
                  _resident((1, INNER_C)), _resident((INNER_C, D_MODEL))],
        out_specs=row(D_MODEL),
        out_shape=jax.ShapeDtypeStruct((t, D_MODEL), F32),
        compiler_params=_params("parallel"),
        name="out_odd",
    )(x, hh, xc, z, skip.reshape(1, INNER_C), w_out)


def kernel(x, rel_bias, norm_gains, ffn_w_gate, ffn_w_up, ffn_w_down, ev_w_in, ev_q_gain, ev_k_gain, ev_pool_w, ev_pool_scale, ev_w_out, od_w_in, od_conv_w, od_conv_b, od_wq, od_wk, od_wv, od_w_gates, od_b_gates, od_skip, od_out_gain, od_w_out):
    b, s, d = x.shape
    t = b * s
    x = x.reshape(t, d)
    bias_tiles = _attn_bias_tiles(rel_bias, s, ATTN_TILE)
    for layer in range(DEPTH):
        g = norm_gains[layer]
        i = layer // 2
        x = _ffn(x, g[0], ffn_w_gate[layer, 0].astype(BF16), ffn_w_up[layer, 0].astype(BF16),
                 ffn_w_down[layer, 0].astype(BF16))
        if layer % 2 == 0:
            q, k, v, u = _proj_even(x, g[1], ev_w_in[i].astype(BF16), ev_q_gain[i], ev_k_gain[i])
            seq = lambda a: a.reshape(b, s, a.shape[-1])
            attn = _attention(seq(q), seq(k), seq(v), bias_tiles)
            pooled = _pool(seq(u), ev_pool_w[i].astype(BF16), ev_pool_scale[i])
            w_out = ev_w_out[i].astype(BF16)
            x = _out_even(x, attn.reshape(t, WIDTH_A), pooled.reshape(t, WIDTH_B),
                          w_out[:WIDTH_A], w_out[WIDTH_A:])
        else:
            x_m, z = _proj_odd(x, g[1], od_w_in[i].astype(BF16))
            xc, q, k, v, gates = _conv_qkv(x_m.reshape(b, s, INNER_C), od_conv_w[i], od_conv_b[i],
                                           od_wq[i], od_wk[i], od_wv[i], od_w_gates[i],
                                           od_b_gates[i])
            hh = _mlstm(q, k, v, gates, od_out_gain[i])
            x = _out_odd(x, hh.reshape(t, INNER_C), xc.reshape(t, INNER_C), z, od_skip[i],
                         od_w_out[i].astype(BF16))
        x = _ffn(x, g[2], ffn_w_gate[layer, 1].astype(BF16), ffn_w_up[layer, 1].astype(BF16),
                 ffn_w_down[layer, 1].astype(BF16))
    return x.reshape(b, s, d)
```

```python
import functools
import math

import numpy as np
import jax
import jax.numpy as jnp
from jax import lax
from jax.experimental import pallas as pl
from jax.experimental.pallas import tpu as pltpu

F32 = jnp.float32
BF16 = jnp.bfloat16

D_MODEL = 1024
DEPTH = 4
N_HEADS_A = 8
HEAD_DIM_A = 64
WIDTH_A = N_HEADS_A * HEAD_DIM_A
DILATED_PATTERNS = ((128, 1), (512, 4), (2048, 16))
N_POOL_GROUPS = 4
POOL_WINDOWS = (2, 4, 8, 16)
WIDTH_B = D_MODEL // 2
POOL_GROUP_DIM = WIDTH_B // N_POOL_GROUPS
IN_WIDTH_EVEN = 3 * WIDTH_A + WIDTH_B
REL_BUCKETS = 32
REL_MAX_DISTANCE = 2048
N_HEADS_C = 4
INNER_C = 2 * D_MODEL
HEAD_DIM_C = INNER_C // N_HEADS_C
CONV_WIDTH = 4
QKV_BLOCK = 4
D_FF = 256 * ((8 * D_MODEL // 3 + 255) // 256)
FFN_RESIDUAL = 0.5
RMS_EPS = 1e-6

LANES = 128
VMEM_LIMIT = 56 * 1024 * 1024
MASKED = -1e30

TOKEN_TILE = 512
ATTN_TILE = 256
MLSTM_CHUNK = 256
CONV_CHANNEL_TILE = 256


def _params(*sem):
    return pltpu.CompilerParams(dimension_semantics=sem, vmem_limit_bytes=VMEM_LIMIT)


def _resident(shape):
    nd = len(shape)
    return pl.BlockSpec(shape, lambda *_: (0,) * nd, pipeline_mode=pl.Buffered(1))


def _rms(x, gain):
    return x * lax.rsqrt(jnp.mean(x * x, axis=-1, keepdims=True) + RMS_EPS) * gain


def _silu(x):
    return x / (1.0 + jnp.exp(-x))


def _dot(a, b):
    return jnp.dot(a, b, preferred_element_type=F32)


def _ffn_kernel(x_ref, g_ref, wg_ref, wu_ref, wd_ref, o_ref, *, ff_tile):
    x = x_ref[...]
    xn = _rms(x, g_ref[...]).astype(BF16)
    acc = None
    for c in range(D_FF // ff_tile):
        sl = slice(c * ff_tile, (c + 1) * ff_tile)
        gate = _dot(xn, wg_ref[:, sl])
        up = _dot(xn, wu_ref[:, sl])
        h = (_silu(gate) * up).astype(BF16)
        y = _dot(h, wd_ref[sl, :])
        acc = y if acc is None else acc + y
    o_ref[...] = x + FFN_RESIDUAL * acc


def _ffn(x, gain, w_gate, w_up, w_down, *, ff_tile=D_FF):
    t = x.shape[0]
    tm = TOKEN_TILE
    row = pl.BlockSpec((tm, D_MODEL), lambda i: (i, 0))
    return pl.pallas_call(
        functools.partial(_ffn_kernel, ff_tile=ff_tile),
        grid=(t // tm,),
        in_specs=[row, _resident((1, D_MODEL)), _resident((D_MODEL, D_FF)),
                  _resident((D_MODEL, D_FF)), _resident((D_FF, D_MODEL))],
        out_specs=row,
        out_shape=jax.ShapeDtypeStruct((t, D_MODEL), F32),
        compiler_params=_params("parallel"),
        name="ffn",
    )(x, gain.reshape(1, D_MODEL), w_gate, w_up, w_down)


def _head_sum_matrix():
    r = lax.broadcasted_iota(jnp.int32, (LANES, LANES), 0) // HEAD_DIM_A
    c = lax.broadcasted_iota(jnp.int32, (LANES, LANES), 1) // HEAD_DIM_A
    return jnp.where(r == c, 1.0, 0.0).astype(BF16)


def _head_norm(t, same_head, gain, scale):
    sq = t * t
    hi = sq.astype(BF16)
    lo = (sq - hi.astype(F32)).astype(BF16)
    ss = _dot(hi, same_head) + _dot(lo, same_head)
    return t * lax.rsqrt(ss * (1.0 / HEAD_DIM_A) + RMS_EPS) * (gain * scale)


def _proj_even_kernel(x_ref, g_ref, w_ref, qg_ref, kg_ref, q_ref, k_ref, v_ref, u_ref):
    xn = _rms(x_ref[...], g_ref[...]).astype(BF16)
    proj = _dot(xn, w_ref[...])
    same_head = _head_sum_matrix()
    for j in range(WIDTH_A // LANES):
        sl = slice(j * LANES, (j + 1) * LANES)
        q_ref[:, sl] = _head_norm(proj[:, sl], same_head, qg_ref[...],
                                  HEAD_DIM_A ** -0.5).astype(BF16)
        ks = slice(WIDTH_A + j * LANES, WIDTH_A + (j + 1) * LANES)
        k_ref[:, sl] = _head_norm(proj[:, ks], same_head, kg_ref[...], 1.0).astype(BF16)
    v_ref[...] = proj[:, 2 * WIDTH_A:3 * WIDTH_A].astype(BF16)
    u_ref[...] = proj[:, 3 * WIDTH_A:]


def _proj_even(x, gain, w_in, q_gain, k_gain):
    t = x.shape[0]
    tm = TOKEN_TILE
    row = lambda n: pl.BlockSpec((tm, n), lambda i: (i, 0))
    pair = lambda g: jnp.tile(g, LANES // HEAD_DIM_A).reshape(1, LANES)
    return pl.pallas_call(
        _proj_even_kernel,
        grid=(t // tm,),
        in_specs=[row(D_MODEL), _resident((1, D_MODEL)), _resident((D_MODEL, IN_WIDTH_EVEN)),
                  _resident((1, LANES)), _resident((1, LANES))],
        out_specs=[row(WIDTH_A), row(WIDTH_A), row(WIDTH_A), row(WIDTH_B)],
        out_shape=[jax.ShapeDtypeStruct((t, WIDTH_A), BF16)] * 3
        + [jax.ShapeDtypeStruct((t, WIDTH_B), F32)],
        compiler_params=_params("parallel"),
        name="proj_even",
    )(x, gain.reshape(1, D_MODEL), w_in, pair(q_gain), pair(k_gain))


def _t5_bucket(distance):
    max_exact = REL_BUCKETS // 2
    d = jnp.maximum(distance.astype(F32), 1.0)
    large = max_exact + (jnp.log(d / max_exact) / math.log(REL_MAX_DISTANCE / max_exact)
                         * (REL_BUCKETS - max_exact)).astype(jnp.int32)
    large = jnp.minimum(large, REL_BUCKETS - 1)
    return jnp.where(distance < max_exact, distance, large)


def _attn_bias_table(rel_bias, seq, tile):
    dist = np.arange(seq)
    mult = np.zeros(seq, np.int64)
    for window, dilation in DILATED_PATTERNS:
        mult += (dist % dilation == 0) & (dist <= window)
    log_mult = np.where(mult > 0, np.log(np.maximum(mult, 1)), MASKED).astype(np.float32)
    per_dist = rel_bias[_t5_bucket(jnp.asarray(dist, jnp.int32))].astype(F32)
    per_dist = jnp.where(jnp.asarray(mult > 0)[:, None], per_dist + log_mult[:, None], MASKED)
    per_dist = per_dist.T
    period = seq + tile
    base = seq - tile
    heads = per_dist.shape[0]
    masked = jnp.full((heads, tile), MASKED, F32)
    v = jnp.concatenate([per_dist[:, base:], masked, per_dist[:, :base]], axis=1)
    w = jnp.roll(v[:, ::-1], 1, axis=1)
    flat = jnp.tile(w, (1, tile))[:, :tile * (period - 1)]
    table = flat.reshape(heads, tile, period - 1)[:, :, :seq]
    return table.reshape(heads // 2, 2, tile, seq)


def _attn_kernel(q_ref, k_ref, v_ref, tab_ref, o_ref, *, tile):
    seq = q_ref.shape[1]
    nq = seq // tile
    first = lax.broadcasted_iota(jnp.int32, (tile, LANES), 1) < HEAD_DIM_A
    for qi in range(nq):
        n = (qi + 1) * tile
        off = (nq - 1 - qi) * tile
        q = q_ref[0, qi * tile:(qi + 1) * tile, :]
        zero = jnp.zeros_like(q)
        q2 = jnp.concatenate([jnp.where(first, q, zero), jnp.where(first, zero, q)], axis=0)
        s = lax.dot_general(q2, k_ref[0, :n, :], (((1,), (1,)), ((), ())),
                            preferred_element_type=F32)
        s = s + jnp.concatenate([tab_ref[0, 0, :, off:off + n], tab_ref[0, 1, :, off:off + n]],
                                axis=0)
        p = jnp.exp(s - jnp.max(s, axis=-1, keepdims=True))
        l = jnp.sum(p, axis=-1, keepdims=True)
        o = _dot(p.astype(BF16), v_ref[0, :n, :]) / l
        o_ref[0, qi * tile:(qi + 1) * tile, :] = jnp.where(first, o[:tile], o[tile:]).astype(BF16)


def _attention(q, k, v, bias_table):
    b, s, _ = q.shape
    tile = ATTN_TILE
    seq_spec = pl.BlockSpec((1, s, LANES), lambda bi, h: (bi, 0, h))
    return pl.pallas_call(
        functools.partial(_attn_kernel, tile=tile),
        grid=(b, WIDTH_A // LANES),
        in_specs=[seq_spec, seq_spec, seq_spec,
                  pl.BlockSpec((1, 2, tile, s), lambda bi, h: (h, 0, 0, 0))],
        out_specs=seq_spec,
        out_shape=jax.ShapeDtypeStruct((b, s, WIDTH_A), BF16),
        compiler_params=_params("parallel", "arbitrary"),
        name="dilated_attention",
    )(q, k, v, bias_table)


def _shift_rows(a, row, shift):
    return jnp.where(row >= shift, pltpu.roll(a, shift, 0), 0.0)


def _pool_kernel(u_ref, w_ref, sc_ref, o_ref):
    g = pl.program_id(1)
    u = u_ref[0]
    row = lax.broadcasted_iota(jnp.int32, u.shape, 0)
    sums = [u]
    for half in (1, 2, 4, 8):
        sums.append(sums[-1] + _shift_rows(sums[-1], row, half))
    total = sums[4]
    window = jnp.int32(POOL_WINDOWS[3])
    for gi in (2, 1, 0):
        total = jnp.where(g == gi, sums[gi + 1], total)
        window = jnp.where(g == gi, POOL_WINDOWS[gi], window)
    count = jnp.minimum(row + 1, window).astype(F32)
    diff = total / count - u
    o_ref[0] = (_dot(diff.astype(BF16), w_ref[0]) * sc_ref[...]).astype(BF16)


def _pool(u, pool_w, pool_scale):
    b, s, _ = u.shape
    assert POOL_WINDOWS == (2, 4, 8, 16)
    spec = pl.BlockSpec((1, s, POOL_GROUP_DIM), lambda bi, g: (bi, 0, g))
    return pl.pallas_call(
        _pool_kernel,
        grid=(b, N_POOL_GROUPS),
        in_specs=[spec,
                  pl.BlockSpec((1, POOL_GROUP_DIM, POOL_GROUP_DIM), lambda bi, g: (g, 0, 0)),
                  pl.BlockSpec((1, POOL_GROUP_DIM), lambda bi, g: (0, g))],
        out_specs=spec,
        out_shape=jax.ShapeDtypeStruct((b, s, WIDTH_B), BF16),
        compiler_params=_params("parallel", "parallel"),
        name="multi_scale_pool",
    )(u, pool_w, pool_scale.reshape(1, WIDTH_B))


def _out_even_kernel(x_ref, a_ref, p_ref, wa_ref, wp_ref, o_ref):
    o_ref[...] = x_ref[...] + _dot(a_ref[...], wa_ref[...]) + _dot(p_ref[...], wp_ref[...])


def _out_even(x, attn, pooled, w_attn, w_pool):
    t = x.shape[0]
    tm = TOKEN_TILE
    row = lambda n: pl.BlockSpec((tm, n), lambda i: (i, 0))
    return pl.pallas_call(
        _out_even_kernel,
        grid=(t // tm,),
        in_specs=[row(D_MODEL), row(WIDTH_A), row(WIDTH_B),
                  _resident((WIDTH_A, D_MODEL)), _resident((WIDTH_B, D_MODEL))],
        out_specs=row(D_MODEL),
        out_shape=jax.ShapeDtypeStruct((t, D_MODEL), F32),
        compiler_params=_params("parallel"),
        name="out_even",
    )(x, attn, pooled, w_attn, w_pool)


def _proj_odd_kernel(x_ref, g_ref, w_ref, xm_ref, z_ref):
    xn = _rms(x_ref[...], g_ref[...]).astype(BF16)
    xm_ref[...] = _dot(xn, w_ref[:, :INNER_C])
    z_ref[...] = _dot(xn, w_ref[:, INNER_C:]).astype(BF16)


def _proj_odd(x, gain, w_in):
    t = x.shape[0]
    tm = TOKEN_TILE
    row = lambda n: pl.BlockSpec((tm, n), lambda i: (i, 0))
    return pl.pallas_call(
        _proj_odd_kernel,
        grid=(t // tm,),
        in_specs=[row(D_MODEL), _resident((1, D_MODEL)), _resident((D_MODEL, 2 * INNER_C))],
        out_specs=[row(INNER_C), row(INNER_C)],
        out_shape=[jax.ShapeDtypeStruct((t, INNER_C), F32),
                   jax.ShapeDtypeStruct((t, INNER_C), BF16)],
        compiler_params=_params("parallel"),
        name="proj_odd",
    )(x, gain.reshape(1, D_MODEL), w_in)


def _conv_qkv_kernel(xm_ref, cw_ref, cb_ref, wq_ref, wk_ref, wv_ref, wg_ref, bg_ref,
                     xc_ref, q_ref, k_ref, v_ref, gates_ref):
    j = pl.program_id(1)
    xm = xm_ref[0]
    row = lax.broadcasted_iota(jnp.int32, xm.shape, 0)
    conv = xm * cw_ref[CONV_WIDTH - 1:CONV_WIDTH, :]
    for back in range(1, CONV_WIDTH):
        tap = CONV_WIDTH - 1 - back
        conv = conv + _shift_rows(xm, row, back) * cw_ref[tap:tap + 1, :]
    xc = _silu(conv + cb_ref[...])
    xc_ref[0] = xc.astype(BF16)
    xc16 = xc.astype(BF16)
    xm16 = xm.astype(BF16)

    @pl.when(j == 0)
    def _():
        gates_ref[0] = jnp.broadcast_to(bg_ref[...], gates_ref.shape[1:])

    gates = gates_ref[0]
    for gi in range(xm.shape[1] // LANES):
        sl = slice(gi * LANES, (gi + 1) * LANES)
        q = _dot(xc16[:, sl], wq_ref[gi])
        k = _dot(xc16[:, sl], wk_ref[gi])
        v = _dot(xm16[:, sl], wv_ref[gi])
        q_ref[0, :, sl] = q.astype(BF16)
        k_ref[0, :, sl] = (k * HEAD_DIM_C ** -0.5).astype(BF16)
        v_ref[0, :, sl] = v.astype(BF16)
        gates = (gates + _dot(q.astype(BF16), wg_ref[0, sl, :])
                 + _dot(k.astype(BF16), wg_ref[1, sl, :])
                 + _dot(v.astype(BF16), wg_ref[2, sl, :]))
    gates_ref[0] = gates


def _block_diag_tiles(w):
    per_tile = LANES // QKV_BLOCK
    w = w.reshape(-1, per_tile, QKV_BLOCK, QKV_BLOCK)
    eye = jnp.eye(per_tile, dtype=w.dtype)
    dense = jnp.einsum("tgio,gh->tgiho", w, eye)
    return dense.reshape(-1, LANES, LANES)


def _conv_qkv(x_m, conv_w, conv_b, wq, wk, wv, w_gates, b_gates):
    b, s, c = x_m.shape
    tc = CONV_CHANNEL_TILE
    nt = tc // LANES
    chan = pl.BlockSpec((1, s, tc), lambda bi, j: (bi, 0, j))
    bd = pl.BlockSpec((nt, LANES, LANES), lambda bi, j: (j, 0, 0))
    n_gates = 2 * N_HEADS_C
    wg = jnp.pad(w_gates.reshape(3, c, n_gates), ((0, 0), (0, 0), (0, LANES - n_gates)))
    bg = jnp.pad(b_gates.reshape(1, n_gates), ((0, 0), (0, LANES - n_gates)))
    act = jax.ShapeDtypeStruct((b, s, c), BF16)
    return pl.pallas_call(
        _conv_qkv_kernel,
        grid=(b, c // tc),
        in_specs=[chan,
                  pl.BlockSpec((CONV_WIDTH, tc), lambda bi, j: (0, j)),
                  pl.BlockSpec((1, tc), lambda bi, j: (0, j)),
                  bd, bd, bd,
                  pl.BlockSpec((3, tc, LANES), lambda bi, j: (0, j, 0)),
                  pl.BlockSpec((1, LANES), lambda bi, j: (0, 0))],
        out_specs=[chan, chan, chan, chan,
                   pl.BlockSpec((1, s, LANES), lambda bi, j: (bi, 0, 0))],
        out_shape=[act, act, act, act, jax.ShapeDtypeStruct((b, s, LANES), F32)],
        compiler_params=_params("parallel", "arbitrary"),
        name="conv_qkv_gates",
    )(x_m, conv_w, conv_b.reshape(1, c),
      _block_diag_tiles(wq).astype(BF16), _block_diag_tiles(wk).astype(BF16),
      _block_diag_tiles(wv).astype(BF16), wg.astype(BF16), bg)


def _log_sigmoid(x):
    return jnp.minimum(x, 0.0) - jnp.log(1.0 + jnp.exp(-jnp.abs(x)))


def _mlstm_kernel(q_ref, k_ref, v_ref, gc_ref, gr_ref, og_ref, o_ref, c_ref, n_ref, m_ref):
    @pl.when(pl.program_id(2) == 0)
    def _():
        c_ref[...] = jnp.zeros_like(c_ref)
        n_ref[...] = jnp.zeros_like(n_ref)
        m_ref[...] = jnp.zeros_like(m_ref)

    q, k, v = q_ref[0], k_ref[0], v_ref[0]
    chunk = q.shape[0]
    gcol, grow = gc_ref[0, 0], gr_ref[0, 0]
    ig_c, lf_c = gcol[:, 0:1], _log_sigmoid(gcol[:, 1:2])
    ig_r, lf_r = grow[0:1, :], _log_sigmoid(grow[1:2, :])
    t_idx = lax.broadcasted_iota(jnp.int32, (chunk, chunk), 0)
    s_idx = lax.broadcasted_iota(jnp.int32, (chunk, chunk), 1)
    causal = s_idx <= t_idx
    bcum_c = jnp.sum(jnp.where(causal, lf_r, 0.0), axis=1, keepdims=True)
    bcum_r = jnp.sum(jnp.where(t_idx <= s_idx, lf_c, 0.0), axis=0, keepdims=True)
    b_last = jnp.sum(lf_r, axis=1, keepdims=True)
    d = jnp.where(causal, bcum_c - bcum_r + ig_r, MASKED)
    m_prev = m_ref[...]
    inter = bcum_c + m_prev
    m_t = jnp.maximum(inter, jnp.max(d, axis=1, keepdims=True))
    w_inter = jnp.exp(inter - m_t)
    s_qk = lax.dot_general(q, k, (((1,), (1,)), ((), ())), preferred_element_type=F32)
    s_qk = s_qk * jnp.exp(d - m_t)
    c_state = c_ref[...]
    n_state = n_ref[...]
    kf = k.astype(F32)
    num = w_inter * _dot(q, c_state.astype(BF16)) + _dot(s_qk.astype(BF16), v)
    den = (w_inter * jnp.sum(q.astype(F32) * n_state, axis=1, keepdims=True)
           + jnp.sum(s_qk, axis=1, keepdims=True))
    h = num / jnp.maximum(jnp.abs(den), jnp.exp(-m_t))

    a_r = b_last - bcum_r + ig_r
    a_c = b_last - bcum_c + ig_c
    m_new = jnp.maximum(b_last + m_prev, jnp.max(a_r, axis=1, keepdims=True))
    decay = jnp.exp(b_last + m_prev - m_new)
    kw = kf * jnp.exp(a_c - m_new)
    c_ref[...] = decay * c_state + lax.dot_general(
        kw.astype(BF16), v, (((0,), (0,)), ((), ())), preferred_element_type=F32)
    n_ref[...] = decay * n_state + jnp.sum(kw, axis=0, keepdims=True)
    m_ref[...] = m_new

    o_ref[0] = _rms(h, og_ref[0]).astype(BF16)


def _mlstm(q, k, v, gates, out_gain):
    b, s, _ = q.shape
    chunk = MLSTM_CHUNK
    e = HEAD_DIM_C
    gh = gates[..., :2 * N_HEADS_C].reshape(b, s, 2, N_HEADS_C)
    g_col = jnp.transpose(gh, (0, 3, 1, 2))
    g_row = jnp.transpose(gh, (0, 3, 2, 1))
    head = pl.BlockSpec((1, chunk, e), lambda bi, h, c: (bi, c, h))
    return pl.pallas_call(
        _mlstm_kernel,
        grid=(b, N_HEADS_C, s // chunk),
        in_specs=[head, head, head,
                  pl.BlockSpec((1, 1, chunk, 2), lambda bi, h, c: (bi, h, c, 0)),
                  pl.BlockSpec((1, 1, 2, chunk), lambda bi, h, c: (bi, h, 0, c)),
                  pl.BlockSpec((1, 1, e), lambda bi, h, c: (h, 0, 0))],
        out_specs=head,
        out_shape=jax.ShapeDtypeStruct((b, s, INNER_C), BF16),
        scratch_shapes=[pltpu.VMEM((e, e), F32), pltpu.VMEM((1, e), F32),
                        pltpu.VMEM((1, 1), F32)],
        compiler_params=_params("parallel", "parallel", "arbitrary"),
        name="mlstm_chunkwise",
    )(q, k, v, g_col, g_row, out_gain.reshape(N_HEADS_C, 1, e))


def _out_odd_kernel(x_ref, h_ref, xc_ref, z_ref, skip_ref, w_ref, o_ref):
    y = (h_ref[...].astype(F32) + skip_ref[...] * xc_ref[...].astype(F32))
    y = y * _silu(z_ref[...].astype(F32))
    o_ref[...] = x_ref[...] + _dot(y.astype(BF16), w_ref[...])


def _out_odd(x, hh, xc, z, skip, w_out):
    t = x.shape[0]
    tm = TOKEN_TILE
    row = lambda n: pl.BlockSpec((tm, n), lambda i: (i, 0))
    return pl.pallas_call(
        _out_odd_kernel,
        grid=(t // tm,),
        in_specs=[row(D_MODEL), row(INNER_C), row(INNER_C), row(INNER_C),
                  _resident((1, INNER_C)), _resident((INNER_C, D_MODEL))],
        out_specs=row(D_MODEL),
        out_shape=jax.ShapeDtypeStruct((t, D_MODEL), F32),
        compiler_params=_params("parallel"),
        name="out_odd",
    )(x, hh, xc, z, skip.reshape(1, INNER_C), w_out)


def kernel(x, rel_bias, norm_gains, ffn_w_gate, ffn_w_up, ffn_w_down, ev_w_in, ev_q_gain, ev_k_gain, ev_pool_w, ev_pool_scale, ev_w_out, od_w_in, od_conv_w, od_conv_b, od_wq, od_wk, od_wv, od_w_gates, od_b_gates, od_skip, od_out_gain, od_w_out):
    b, s, d = x.shape
    t = b * s
    x = x.reshape(t, d)
    bias_table = _attn_bias_table(rel_bias, s, ATTN_TILE)
    for layer in range(DEPTH):
        g = norm_gains[layer]
        i = layer // 2
        x = _ffn(x, g[0], ffn_w_gate[layer, 0].astype(BF16), ffn_w_up[layer, 0].astype(BF16),
                 ffn_w_down[layer, 0].astype(BF16))
        if layer % 2 == 0:
            q, k, v, u = _proj_even(x, g[1], ev_w_in[i].astype(BF16), ev_q_gain[i], ev_k_gain[i])
            seq = lambda a: a.reshape(b, s, a.shape[-1])
            attn = _attention(seq(q), seq(k), seq(v), bias_table)
            pooled = _pool(seq(u), ev_pool_w[i].astype(BF16), ev_pool_scale[i])
            w_out = ev_w_out[i].astype(BF16)
            x = _out_even(x, attn.reshape(t, WIDTH_A), pooled.reshape(t, WIDTH_B),
                          w_out[:WIDTH_A], w_out[WIDTH_A:])
        else:
            x_m, z = _proj_odd(x, g[1], od_w_in[i].astype(BF16))
            xc, q, k, v, gates = _conv_qkv(x_m.reshape(b, s, INNER_C), od_conv_w[i], od_conv_b[i],
                                           od_wq[i], od_wk[i], od_wv[i], od_w_gates[i],
                                           od_b_gates[i])
            hh = _mlstm(q, k, v, gates, od_out_gain[i])
            x = _out_odd(x, hh.reshape(t, INNER_C), xc.reshape(t, INNER_C), z, od_skip[i],
                         od_w_out[i].astype(BF16))
        x = _ffn(x, g[2], ffn_w_gate[layer, 1].astype(BF16), ffn_w_up[layer, 1].astype(BF16),
                 ffn_w_down[layer, 1].astype(BF16))
    return x.reshape(b, s, d)
```

```python
import functools
import math

import numpy as np
import jax
import jax.numpy as jnp
from jax import lax
from jax.experimental import pallas as pl
from jax.experimental.pallas import tpu as pltpu

F32 = jnp.float32
BF16 = jnp.bfloat16

D_MODEL = 1024
DEPTH = 4
N_HEADS_A = 8
HEAD_DIM_A = 64
WIDTH_A = N_HEADS_A * HEAD_DIM_A
DILATED_PATTERNS = ((128, 1), (512, 4), (2048, 16))
N_POOL_GROUPS = 4
POOL_WINDOWS = (2, 4, 8, 16)
WIDTH_B = D_MODEL // 2
POOL_GROUP_DIM = WIDTH_B // N_POOL_GROUPS
IN_WIDTH_EVEN = 3 * WIDTH_A + WIDTH_B
REL_BUCKETS = 32
REL_MAX_DISTANCE = 2048
N_HEADS_C = 4
INNER_C = 2 * D_MODEL
HEAD_DIM_C = INNER_C // N_HEADS_C
CONV_WIDTH = 4
QKV_BLOCK = 4
D_FF = 256 * ((8 * D_MODEL // 3 + 255) // 256)
FFN_RESIDUAL = 0.5
RMS_EPS = 1e-6

LANES = 128
VMEM_LIMIT = 56 * 1024 * 1024
MASKED = -1e30

TOKEN_TILE = 512
ATTN_TILE = 256
MLSTM_CHUNK = 256
CONV_CHANNEL_TILE = 256


def _params(*sem):
    return pltpu.CompilerParams(dimension_semantics=sem, vmem_limit_bytes=VMEM_LIMIT)


def _resident(shape):
    nd = len(shape)
    return pl.BlockSpec(shape, lambda *_: (0,) * nd, pipeline_mode=pl.Buffered(1))


def _rms(x, gain):
    return x * lax.rsqrt(jnp.mean(x * x, axis=-1, keepdims=True) + RMS_EPS) * gain


def _silu(x):
    return x / (1.0 + jnp.exp(-x))


def _dot(a, b):
    return jnp.dot(a, b, preferred_element_type=F32)


def _ffn_kernel(x_ref, g_ref, wg_ref, wu_ref, wd_ref, o_ref, *, ff_tile):
    x = x_ref[...]
    xn = _rms(x, g_ref[...]).astype(BF16)
    acc = None
    for c in range(D_FF // ff_tile):
        sl = slice(c * ff_tile, (c + 1) * ff_tile)
        gate = _dot(xn, wg_ref[:, sl])
        up = _dot(xn, wu_ref[:, sl])
        h = (_silu(gate) * up).astype(BF16)
        y = _dot(h, wd_ref[sl, :])
        acc = y if acc is None else acc + y
    o_ref[...] = x + FFN_RESIDUAL * acc


def _ffn(x, gain, w_gate, w_up, w_down, *, ff_tile=D_FF):
    t = x.shape[0]
    tm = TOKEN_TILE
    row = pl.BlockSpec((tm, D_MODEL), lambda i: (i, 0))
    return pl.pallas_call(
        functools.partial(_ffn_kernel, ff_tile=ff_tile),
        grid=(t // tm,),
        in_specs=[row, _resident((1, D_MODEL)), _resident((D_MODEL, D_FF)),
                  _resident((D_MODEL, D_FF)), _resident((D_FF, D_MODEL))],
        out_specs=row,
        out_shape=jax.ShapeDtypeStruct((t, D_MODEL), F32),
        compiler_params=_params("parallel"),
        name="ffn",
    )(x, gain.reshape(1, D_MODEL), w_gate, w_up, w_down)


def _head_sum_matrix():
    r = lax.broadcasted_iota(jnp.int32, (LANES, LANES), 0) // HEAD_DIM_A
    c = lax.broadcasted_iota(jnp.int32, (LANES, LANES), 1) // HEAD_DIM_A
    return jnp.where(r == c, 1.0, 0.0).astype(BF16)


def _head_norm(t, same_head, gain, scale):
    sq = t * t
    hi = sq.astype(BF16)
    lo = (sq - hi.astype(F32)).astype(BF16)
    ss = _dot(hi, same_head) + _dot(lo, same_head)
    return t * lax.rsqrt(ss * (1.0 / HEAD_DIM_A) + RMS_EPS) * (gain * scale)


def _proj_even_kernel(x_ref, g_ref, w_ref, qg_ref, kg_ref, q_ref, k_ref, v_ref, u_ref):
    xn = _rms(x_ref[...], g_ref[...]).astype(BF16)
    proj = _dot(xn, w_ref[...])
    same_head = _head_sum_matrix()
    for j in range(WIDTH_A // LANES):
        sl = slice(j * LANES, (j + 1) * LANES)
        q_ref[:, sl] = _head_norm(proj[:, sl], same_head, qg_ref[...],
                                  HEAD_DIM_A ** -0.5).astype(BF16)
        ks = slice(WIDTH_A + j * LANES, WIDTH_A + (j + 1) * LANES)
        k_ref[:, sl] = _head_norm(proj[:, ks], same_head, kg_ref[...], 1.0).astype(BF16)
    v_ref[...] = proj[:, 2 * WIDTH_A:3 * WIDTH_A].astype(BF16)
    u_ref[...] = proj[:, 3 * WIDTH_A:]


def _proj_even(x, gain, w_in, q_gain, k_gain):
    t = x.shape[0]
    tm = TOKEN_TILE
    row = lambda n: pl.BlockSpec((tm, n), lambda i: (i, 0))
    pair = lambda g: jnp.tile(g, LANES // HEAD_DIM_A).reshape(1, LANES)
    return pl.pallas_call(
        _proj_even_kernel,
        grid=(t // tm,),
        in_specs=[row(D_MODEL), _resident((1, D_MODEL)), _resident((D_MODEL, IN_WIDTH_EVEN)),
                  _resident((1, LANES)), _resident((1, LANES))],
        out_specs=[row(WIDTH_A), row(WIDTH_A), row(WIDTH_A), row(WIDTH_B)],
        out_shape=[jax.ShapeDtypeStruct((t, WIDTH_A), BF16)] * 3
        + [jax.ShapeDtypeStruct((t, WIDTH_B), F32)],
        compiler_params=_params("parallel"),
        name="proj_even",
    )(x, gain.reshape(1, D_MODEL), w_in, pair(q_gain), pair(k_gain))


def _t5_bucket(distance):
    max_exact = REL_BUCKETS // 2
    d = jnp.maximum(distance.astype(F32), 1.0)
    large = max_exact + (jnp.log(d / max_exact) / math.log(REL_MAX_DISTANCE / max_exact)
                         * (REL_BUCKETS - max_exact)).astype(jnp.int32)
    large = jnp.minimum(large, REL_BUCKETS - 1)
    return jnp.where(distance < max_exact, distance, large)


def _attn_bias_rows(rel_bias, seq, tile):
    dist = np.arange(seq)
    mult = np.zeros(seq, np.int64)
    for window, dilation in DILATED_PATTERNS:
        mult += (dist % dilation == 0) & (dist <= window)
    log_mult = np.where(mult > 0, np.log(np.maximum(mult, 1)), MASKED).astype(np.float32)
    per_dist = rel_bias[_t5_bucket(jnp.asarray(dist, jnp.int32))].astype(F32)
    per_dist = jnp.where(jnp.asarray(mult > 0)[:, None], per_dist + log_mult[:, None], MASKED)
    per_dist = per_dist.T
    base = seq - tile
    heads = per_dist.shape[0]
    masked = jnp.full((heads, tile), MASKED, F32)
    v = jnp.concatenate([per_dist[:, base:], masked, per_dist[:, :base]], axis=1)
    w = jnp.roll(v[:, ::-1], 1, axis=1)
    return w.reshape(heads // 2, 2, 1, seq + tile)


def _attn_kernel(q_ref, k_ref, v_ref, w_ref, o_ref, tab_ref, *, tile):
    seq = q_ref.shape[1]
    nq = seq // tile

    @pl.when(pl.program_id(1) == 0)
    def _():
        for hh in range(2):
            rows = jnp.broadcast_to(w_ref[0, hh], (tile, seq + tile))
            tab_ref[hh] = pltpu.roll(rows, 0, 1, stride=1, stride_axis=0)[:, :seq]

    first = lax.broadcasted_iota(jnp.int32, (tile, LANES), 1) < HEAD_DIM_A
    for qi in range(nq):
        n = (qi + 1) * tile
        off = (nq - 1 - qi) * tile
        q = q_ref[0, qi * tile:(qi + 1) * tile, :]
        zero = jnp.zeros_like(q)
        q2 = jnp.concatenate([jnp.where(first, q, zero), jnp.where(first, zero, q)], axis=0)
        s = lax.dot_general(q2, k_ref[0, :n, :], (((1,), (1,)), ((), ())),
                            preferred_element_type=F32)
        s = s + jnp.concatenate([tab_ref[0, :, off:off + n], tab_ref[1, :, off:off + n]], axis=0)
        p = jnp.exp(s - jnp.max(s, axis=-1, keepdims=True))
        l = jnp.sum(p, axis=-1, keepdims=True)
        o = _dot(p.astype(BF16), v_ref[0, :n, :]) / l
        o_ref[0, qi * tile:(qi + 1) * tile, :] = jnp.where(first, o[:tile], o[tile:]).astype(BF16)


def _attention(q, k, v, bias_rows):
    b, s, _ = q.shape
    tile = ATTN_TILE
    seq_spec = pl.BlockSpec((1, s, LANES), lambda h, bi: (bi, 0, h))
    return pl.pallas_call(
        functools.partial(_attn_kernel, tile=tile),
        grid=(WIDTH_A // LANES, b),
        in_specs=[seq_spec, seq_spec, seq_spec,
                  pl.BlockSpec((1, 2, 1, s + tile), lambda h, bi: (h, 0, 0, 0))],
        out_specs=seq_spec,
        out_shape=jax.ShapeDtypeStruct((b, s, WIDTH_A), BF16),
        scratch_shapes=[pltpu.VMEM((2, tile, s), F32)],
        compiler_params=_params("arbitrary", "arbitrary"),
        name="dilated_attention",
    )(q, k, v, bias_rows)


def _shift_rows(a, row, shift):
    return jnp.where(row >= shift, pltpu.roll(a, shift, 0), 0.0)


def _pool_kernel(u_ref, w_ref, sc_ref, o_ref):
    g = pl.program_id(1)
    u = u_ref[0]
    row = lax.broadcasted_iota(jnp.int32, u.shape, 0)
    sums = [u]
    for half in (1, 2, 4, 8):
        sums.append(sums[-1] + _shift_rows(sums[-1], row, half))
    total = sums[4]
    window = jnp.int32(POOL_WINDOWS[3])
    for gi in (2, 1, 0):
        total = jnp.where(g == gi, sums[gi + 1], total)
        window = jnp.where(g == gi, POOL_WINDOWS[gi], window)
    count = jnp.minimum(row + 1, window).astype(F32)
    diff = total / count - u
    o_ref[0] = (_dot(diff.astype(BF16), w_ref[0]) * sc_ref[...]).astype(BF16)


def _pool(u, pool_w, pool_scale):
    b, s, _ = u.shape
    assert POOL_WINDOWS == (2, 4, 8, 16)
    spec = pl.BlockSpec((1, s, POOL_GROUP_DIM), lambda bi, g: (bi, 0, g))
    return pl.pallas_call(
        _pool_kernel,
        grid=(b, N_POOL_GROUPS),
        in_specs=[spec,
                  pl.BlockSpec((1, POOL_GROUP_DIM, POOL_GROUP_DIM), lambda bi, g: (g, 0, 0)),
                  pl.BlockSpec((1, POOL_GROUP_DIM), lambda bi, g: (0, g))],
        out_specs=spec,
        out_shape=jax.ShapeDtypeStruct((b, s, WIDTH_B), BF16),
        compiler_params=_params("parallel", "parallel"),
        name="multi_scale_pool",
    )(u, pool_w, pool_scale.reshape(1, WIDTH_B))


def _out_even_kernel(x_ref, a_ref, p_ref, w_ref, o_ref):
    o_ref[...] = (x_ref[...] + _dot(a_ref[...], w_ref[:WIDTH_A, :])
                  + _dot(p_ref[...], w_ref[WIDTH_A:, :]))


def _out_even(x, attn, pooled, w_out):
    t = x.shape[0]
    tm = TOKEN_TILE
    row = lambda n: pl.BlockSpec((tm, n), lambda i: (i, 0))
    return pl.pallas_call(
        _out_even_kernel,
        grid=(t // tm,),
        in_specs=[row(D_MODEL), row(WIDTH_A), row(WIDTH_B),
                  _resident((WIDTH_A + WIDTH_B, D_MODEL))],
        out_specs=row(D_MODEL),
        out_shape=jax.ShapeDtypeStruct((t, D_MODEL), F32),
        compiler_params=_params("parallel"),
        name="out_even",
    )(x, attn, pooled, w_out)


def _proj_odd_kernel(x_ref, g_ref, w_ref, xm_ref, z_ref):
    xn = _rms(x_ref[...], g_ref[...]).astype(BF16)
    xm_ref[...] = _dot(xn, w_ref[:, :INNER_C])
    z_ref[...] = _dot(xn, w_ref[:, INNER_C:]).astype(BF16)


def _proj_odd(x, gain, w_in):
    t = x.shape[0]
    tm = TOKEN_TILE
    row = lambda n: pl.BlockSpec((tm, n), lambda i: (i, 0))
    return pl.pallas_call(
        _proj_odd_kernel,
        grid=(t // tm,),
        in_specs=[row(D_MODEL), _resident((1, D_MODEL)), _resident((D_MODEL, 2 * INNER_C))],
        out_specs=[row(INNER_C), row(INNER_C)],
        out_shape=[jax.ShapeDtypeStruct((t, INNER_C), F32),
                   jax.ShapeDtypeStruct((t, INNER_C), BF16)],
        compiler_params=_params("parallel"),
        name="proj_odd",
    )(x, gain.reshape(1, D_MODEL), w_in)


def _conv_qkv_kernel(xm_ref, cw_ref, cb_ref, wq_ref, wk_ref, wv_ref, wg_ref, bg_ref,
                     xc_ref, q_ref, k_ref, v_ref, gates_ref, gates_t_ref):
    j = pl.program_id(1)
    xm = xm_ref[0]
    row = lax.broadcasted_iota(jnp.int32, xm.shape, 0)
    conv = xm * cw_ref[CONV_WIDTH - 1:CONV_WIDTH, :]
    for back in range(1, CONV_WIDTH):
        tap = CONV_WIDTH - 1 - back
        conv = conv + _shift_rows(xm, row, back) * cw_ref[tap:tap + 1, :]
    xc = _silu(conv + cb_ref[...])
    xc_ref[0] = xc.astype(BF16)
    xc16 = xc.astype(BF16)
    xm16 = xm.astype(BF16)

    @pl.when(j == 0)
    def _():
        gates_ref[0] = jnp.broadcast_to(bg_ref[...], gates_ref.shape[1:])

    gates = gates_ref[0]
    for gi in range(xm.shape[1] // LANES):
        sl = slice(gi * LANES, (gi + 1) * LANES)
        q = _dot(xc16[:, sl], wq_ref[gi])
        k = _dot(xc16[:, sl], wk_ref[gi])
        v = _dot(xm16[:, sl], wv_ref[gi])
        q_ref[0, :, sl] = q.astype(BF16)
        k_ref[0, :, sl] = (k * HEAD_DIM_C ** -0.5).astype(BF16)
        v_ref[0, :, sl] = v.astype(BF16)
        gates = (gates + _dot(q.astype(BF16), wg_ref[0, sl, :])
                 + _dot(k.astype(BF16), wg_ref[1, sl, :])
                 + _dot(v.astype(BF16), wg_ref[2, sl, :]))
    gates_ref[0] = gates

    @pl.when(j == pl.num_programs(1) - 1)
    def _():
        gates_t_ref[0] = gates.T


def _block_diag_tiles(w):
    per_tile = LANES // QKV_BLOCK
    w = w.reshape(-1, per_tile, QKV_BLOCK, QKV_BLOCK)
    eye = jnp.eye(per_tile, dtype=w.dtype)
    dense = jnp.einsum("tgio,gh->tgiho", w, eye)
    return dense.reshape(-1, LANES, LANES)


def _conv_qkv(x_m, conv_w, conv_b, wq, wk, wv, w_gates, b_gates):
    b, s, c = x_m.shape
    tc = CONV_CHANNEL_TILE
    nt = tc // LANES
    chan = pl.BlockSpec((1, s, tc), lambda bi, j: (bi, 0, j))
    bd = pl.BlockSpec((nt, LANES, LANES), lambda bi, j: (j, 0, 0))
    n_gates = 2 * N_HEADS_C
    wg = jnp.pad(w_gates.reshape(3, c, n_gates), ((0, 0), (0, 0), (0, LANES - n_gates)))
    bg = jnp.pad(b_gates.reshape(1, n_gates), ((0, 0), (0, LANES - n_gates)))
    act = jax.ShapeDtypeStruct((b, s, c), BF16)
    return pl.pallas_call(
        _conv_qkv_kernel,
        grid=(b, c // tc),
        in_specs=[chan,
                  pl.BlockSpec((CONV_WIDTH, tc), lambda bi, j: (0, j)),
                  pl.BlockSpec((1, tc), lambda bi, j: (0, j)),
                  bd, bd, bd,
                  pl.BlockSpec((3, tc, LANES), lambda bi, j: (0, j, 0)),
                  pl.BlockSpec((1, LANES), lambda bi, j: (0, 0))],
        out_specs=[chan, chan, chan, chan,
                   pl.BlockSpec((1, s, LANES), lambda bi, j: (bi, 0, 0)),
                   pl.BlockSpec((1, LANES, s), lambda bi, j: (bi, 0, 0))],
        out_shape=[act, act, act, act, jax.ShapeDtypeStruct((b, s, LANES), F32),
                   jax.ShapeDtypeStruct((b, LANES, s), F32)],
        compiler_params=_params("parallel", "arbitrary"),
        name="conv_qkv_gates",
    )(x_m, conv_w, conv_b.reshape(1, c),
      _block_diag_tiles(wq).astype(BF16), _block_diag_tiles(wk).astype(BF16),
      _block_diag_tiles(wv).astype(BF16), wg.astype(BF16), bg)


def _log_sigmoid(x):
    return jnp.minimum(x, 0.0) - jnp.log(1.0 + jnp.exp(-jnp.abs(x)))


def _mlstm_head(q, k, v, ig_c, fg_c, ig_r, fg_r, c_ref, n_ref, m_ref):
    chunk = q.shape[0]
    lf_c = _log_sigmoid(fg_c)
    lf_r = _log_sigmoid(fg_r)
    t_idx = lax.broadcasted_iota(jnp.int32, (chunk, chunk), 0)
    s_idx = lax.broadcasted_iota(jnp.int32, (chunk, chunk), 1)
    causal = s_idx <= t_idx
    bcum_c = jnp.sum(jnp.where(causal, lf_r, 0.0), axis=1, keepdims=True)
    bcum_r = jnp.sum(jnp.where(t_idx <= s_idx, lf_c, 0.0), axis=0, keepdims=True)
    b_last = jnp.sum(lf_r, axis=1, keepdims=True)
    d = jnp.where(causal, bcum_c - bcum_r + ig_r, MASKED)
    m_prev = m_ref[...]
    inter = bcum_c + m_prev
    m_t = jnp.maximum(inter, jnp.max(d, axis=1, keepdims=True))
    w_inter = jnp.exp(inter - m_t)
    s_qk = lax.dot_general(q, k, (((1,), (1,)), ((), ())), preferred_element_type=F32)
    s_qk = s_qk * jnp.exp(d - m_t)
    c_state = c_ref[...]
    n_state = n_ref[...]
    kf = k.astype(F32)
    num = w_inter * _dot(q, c_state.astype(BF16)) + _dot(s_qk.astype(BF16), v)
    den = (w_inter * jnp.sum(q.astype(F32) * n_state, axis=1, keepdims=True)
           + jnp.sum(s_qk, axis=1, keepdims=True))
    h = num / jnp.maximum(jnp.abs(den), jnp.exp(-m_t))

    a_r = b_last - bcum_r + ig_r
    a_c = b_last - bcum_c + ig_c
    m_new = jnp.maximum(b_last + m_prev, jnp.max(a_r, axis=1, keepdims=True))
    decay = jnp.exp(b_last + m_prev - m_new)
    kw = kf * jnp.exp(a_c - m_new)
    c_ref[...] = decay * c_state + lax.dot_general(
        kw.astype(BF16), v, (((0,), (0,)), ((), ())), preferred_element_type=F32)
    n_ref[...] = decay * n_state + jnp.sum(kw, axis=0, keepdims=True)
    m_ref[...] = m_new
    return h


def _mlstm_kernel(q_ref, k_ref, v_ref, gc_ref, gr_ref, og_ref, xc_ref, z_ref, skip_ref,
                  o_ref, c_ref, n_ref, m_ref):
    @pl.when(pl.program_id(1) == 0)
    def _():
        c_ref[...] = jnp.zeros_like(c_ref)
        n_ref[...] = jnp.zeros_like(n_ref)
        m_ref[...] = jnp.zeros_like(m_ref)

    e = HEAD_DIM_C
    nh = N_HEADS_C
    gcol = gc_ref[0]
    grow = gr_ref[0]
    for h in range(nh):
        sl = slice(h * e, (h + 1) * e)
        hh = _mlstm_head(q_ref[0, :, sl], k_ref[0, :, sl], v_ref[0, :, sl],
                         gcol[:, h:h + 1], gcol[:, nh + h:nh + h + 1],
                         grow[h:h + 1, :], grow[nh + h:nh + h + 1, :],
                         c_ref.at[h], n_ref.at[h], m_ref.at[h])
        y = _rms(hh, og_ref[:, sl]) + skip_ref[:, sl] * xc_ref[0, :, sl].astype(F32)
        o_ref[0, :, sl] = (y * _silu(z_ref[0, :, sl].astype(F32))).astype(BF16)


def _mlstm(q, k, v, gates, gates_t, out_gain, xc, z, skip):
    b, s, c = q.shape
    chunk = MLSTM_CHUNK
    e = HEAD_DIM_C
    rows = pl.BlockSpec((1, chunk, c), lambda bi, ci: (bi, ci, 0))
    return pl.pallas_call(
        _mlstm_kernel,
        grid=(b, s // chunk),
        in_specs=[rows, rows, rows,
                  pl.BlockSpec((1, chunk, LANES), lambda bi, ci: (bi, ci, 0)),
                  pl.BlockSpec((1, 2 * N_HEADS_C, chunk), lambda bi, ci: (bi, 0, ci)),
                  _resident((1, c)), rows, rows, _resident((1, c))],
        out_specs=rows,
        out_shape=jax.ShapeDtypeStruct((b, s, c), BF16),
        scratch_shapes=[pltpu.VMEM((N_HEADS_C, e, e), F32), pltpu.VMEM((N_HEADS_C, 1, e), F32),
                        pltpu.VMEM((N_HEADS_C, 1, 1), F32)],
        compiler_params=_params("parallel", "arbitrary"),
        name="mlstm_chunkwise",
    )(q, k, v, gates, gates_t, out_gain.reshape(1, c), xc, z, skip.reshape(1, c))


def _out_odd_kernel(x_ref, y_ref, w_ref, o_ref):
    o_ref[...] = x_ref[...] + _dot(y_ref[...], w_ref[...])


def _out_odd(x, y, w_out):
    t = x.shape[0]
    tm = TOKEN_TILE
    row = lambda n: pl.BlockSpec((tm, n), lambda i: (i, 0))
    return pl.pallas_call(
        _out_odd_kernel,
        grid=(t // tm,),
        in_specs=[row(D_MODEL), row(INNER_C), _resident((INNER_C, D_MODEL))],
        out_specs=row(D_MODEL),
        out_shape=jax.ShapeDtypeStruct((t, D_MODEL), F32),
        compiler_params=_params("parallel"),
        name="out_odd",
    )(x, y, w_out)


def kernel(x, rel_bias, norm_gains, ffn_w_gate, ffn_w_up, ffn_w_down, ev_w_in, ev_q_gain, ev_k_gain, ev_pool_w, ev_pool_scale, ev_w_out, od_w_in, od_conv_w, od_conv_b, od_wq, od_wk, od_wv, od_w_gates, od_b_gates, od_skip, od_out_gain, od_w_out):
    b, s, d = x.shape
    t = b * s
    x = x.reshape(t, d)
    bias_rows = _attn_bias_rows(rel_bias, s, ATTN_TILE)
    for layer in range(DEPTH):
        g = norm_gains[layer]
        i = layer // 2
        x = _ffn(x, g[0], ffn_w_gate[layer, 0].astype(BF16), ffn_w_up[layer, 0].astype(BF16),
                 ffn_w_down[layer, 0].astype(BF16))
        if layer % 2 == 0:
            q, k, v, u = _proj_even(x, g[1], ev_w_in[i].astype(BF16), ev_q_gain[i], ev_k_gain[i])
            seq = lambda a: a.reshape(b, s, a.shape[-1])
            attn = _attention(seq(q), seq(k), seq(v), bias_rows)
            pooled = _pool(seq(u), ev_pool_w[i].astype(BF16), ev_pool_scale[i])
            x = _out_even(x, attn.reshape(t, WIDTH_A), pooled.reshape(t, WIDTH_B),
                          ev_w_out[i].astype(BF16))
        else:
            x_m, z = _proj_odd(x, g[1], od_w_in[i].astype(BF16))
            xc, q, k, v, gates, gates_t = _conv_qkv(
                x_m.reshape(b, s, INNER_C), od_conv_w[i], od_conv_b[i], od_wq[i], od_wk[i],
                od_wv[i], od_w_gates[i], od_b_gates[i])
            y = _mlstm(q, k, v, gates, gates_t, od_out_gain[i], xc, z.reshape(b, s, INNER_C),
                       od_skip[i])
            x = _out_odd(x, y.reshape(t, INNER_C), od_w_out[i].astype(BF16))
        x = _ffn(x, g[2], ffn_w_gate[layer, 1].astype(BF16), ffn_w_up[layer, 1].astype(BF16),
                 ffn_w_down[layer, 1].astype(BF16))
    return x.reshape(b, s, d)
```

```python
import functools
import math

import numpy as np
import jax
import jax.numpy as jnp
from jax import lax
from jax.experimental import pallas as pl
from jax.experimental.pallas import tpu as pltpu

F32 = jnp.float32
BF16 = jnp.bfloat16

D_MODEL = 1024
DEPTH = 4
N_HEADS_A = 8
HEAD_DIM_A = 64
WIDTH_A = N_HEADS_A * HEAD_DIM_A
DILATED_PATTERNS = ((128, 1), (512, 4), (2048, 16))
N_POOL_GROUPS = 4
POOL_WINDOWS = (2, 4, 8, 16)
WIDTH_B = D_MODEL // 2
POOL_GROUP_DIM = WIDTH_B // N_POOL_GROUPS
IN_WIDTH_EVEN = 3 * WIDTH_A + WIDTH_B
REL_BUCKETS = 32
REL_MAX_DISTANCE = 2048
N_HEADS_C = 4
INNER_C = 2 * D_MODEL
HEAD_DIM_C = INNER_C // N_HEADS_C
CONV_WIDTH = 4
QKV_BLOCK = 4
D_FF = 256 * ((8 * D_MODEL // 3 + 255) // 256)
FFN_RESIDUAL = 0.5
RMS_EPS = 1e-6

LANES = 128
VMEM_LIMIT = 56 * 1024 * 1024
MASKED = -1e30

TOKEN_TILE = 512
ATTN_TILE = 256
MLSTM_CHUNK = 256
CONV_CHANNEL_TILE = 256


def _params(*sem):
    return pltpu.CompilerParams(dimension_semantics=sem, vmem_limit_bytes=VMEM_LIMIT)


def _resident(shape):
    nd = len(shape)
    return pl.BlockSpec(shape, lambda *_: (0,) * nd, pipeline_mode=pl.Buffered(1))


def _stacked(shape, *lead):
    nd = len(shape)
    return pl.BlockSpec((None,) * len(lead) + tuple(shape), lambda *_: tuple(lead) + (0,) * nd,
                        pipeline_mode=pl.Buffered(1))


def _rms(x, gain):
    return x * lax.rsqrt(jnp.mean(x * x, axis=-1, keepdims=True) + RMS_EPS) * gain


def _silu(x):
    return x / (1.0 + jnp.exp(-x))


def _dot(a, b):
    return jnp.dot(a, b, preferred_element_type=F32)


def _ffn_kernel(x_ref, g_ref, wg_ref, wu_ref, wd_ref, o_ref, *, ff_tile):
    x = x_ref[...]
    xn = _rms(x, g_ref[...]).astype(BF16)
    acc = None
    for c in range(D_FF // ff_tile):
        sl = slice(c * ff_tile, (c + 1) * ff_tile)
        gate = _dot(xn, wg_ref[:, sl])
        up = _dot(xn, wu_ref[:, sl])
        h = (_silu(gate) * up).astype(BF16)
        y = _dot(h, wd_ref[sl, :])
        acc = y if acc is None else acc + y
    o_ref[...] = x + FFN_RESIDUAL * acc


def _ffn(x, gain, w_gate, w_up, w_down, layer, half, *, ff_tile=D_FF):
    t = x.shape[0]
    tm = TOKEN_TILE
    row = pl.BlockSpec((tm, D_MODEL), lambda i: (i, 0))
    return pl.pallas_call(
        functools.partial(_ffn_kernel, ff_tile=ff_tile),
        grid=(t // tm,),
        in_specs=[row, _resident((1, D_MODEL)), _stacked((D_MODEL, D_FF), layer, half),
                  _stacked((D_MODEL, D_FF), layer, half), _stacked((D_FF, D_MODEL), layer, half)],
        out_specs=row,
        out_shape=jax.ShapeDtypeStruct((t, D_MODEL), F32),
        compiler_params=_params("parallel"),
        name="ffn",
    )(x, gain.reshape(1, D_MODEL), w_gate, w_up, w_down)


def _head_sum_matrix():
    r = lax.broadcasted_iota(jnp.int32, (LANES, LANES), 0) // HEAD_DIM_A
    c = lax.broadcasted_iota(jnp.int32, (LANES, LANES), 1) // HEAD_DIM_A
    return jnp.where(r == c, 1.0, 0.0).astype(BF16)


def _head_norm(t, same_head, gain, scale):
    sq = t * t
    hi = sq.astype(BF16)
    lo = (sq - hi.astype(F32)).astype(BF16)
    ss = _dot(hi, same_head) + _dot(lo, same_head)
    return t * lax.rsqrt(ss * (1.0 / HEAD_DIM_A) + RMS_EPS) * (gain * scale)


def _proj_even_kernel(x_ref, g_ref, w_ref, qg_ref, kg_ref, q_ref, k_ref, v_ref, u_ref):
    xn = _rms(x_ref[...], g_ref[...]).astype(BF16)
    proj = _dot(xn, w_ref[...])
    same_head = _head_sum_matrix()
    for j in range(WIDTH_A // LANES):
        sl = slice(j * LANES, (j + 1) * LANES)
        q_ref[:, sl] = _head_norm(proj[:, sl], same_head, qg_ref[...],
                                  HEAD_DIM_A ** -0.5).astype(BF16)
        ks = slice(WIDTH_A + j * LANES, WIDTH_A + (j + 1) * LANES)
        k_ref[:, sl] = _head_norm(proj[:, ks], same_head, kg_ref[...], 1.0).astype(BF16)
    v_ref[...] = proj[:, 2 * WIDTH_A:3 * WIDTH_A].astype(BF16)
    u_ref[...] = proj[:, 3 * WIDTH_A:]


def _proj_even(x, gain, w_in, i, q_gain, k_gain):
    t = x.shape[0]
    tm = TOKEN_TILE
    row = lambda n: pl.BlockSpec((tm, n), lambda i: (i, 0))
    pair = lambda g: jnp.tile(g, LANES // HEAD_DIM_A).reshape(1, LANES)
    return pl.pallas_call(
        _proj_even_kernel,
        grid=(t // tm,),
        in_specs=[row(D_MODEL), _resident((1, D_MODEL)), _stacked((D_MODEL, IN_WIDTH_EVEN), i),
                  _resident((1, LANES)), _resident((1, LANES))],
        out_specs=[row(WIDTH_A), row(WIDTH_A), row(WIDTH_A), row(WIDTH_B)],
        out_shape=[jax.ShapeDtypeStruct((t, WIDTH_A), BF16)] * 3
        + [jax.ShapeDtypeStruct((t, WIDTH_B), F32)],
        compiler_params=_params("parallel"),
        name="proj_even",
    )(x, gain.reshape(1, D_MODEL), w_in, pair(q_gain), pair(k_gain))


def _t5_bucket(distance):
    max_exact = REL_BUCKETS // 2
    d = jnp.maximum(distance.astype(F32), 1.0)
    large = max_exact + (jnp.log(d / max_exact) / math.log(REL_MAX_DISTANCE / max_exact)
                         * (REL_BUCKETS - max_exact)).astype(jnp.int32)
    large = jnp.minimum(large, REL_BUCKETS - 1)
    return jnp.where(distance < max_exact, distance, large)


def _attn_bias_rows(rel_bias, seq, tile):
    dist = np.arange(seq)
    mult = np.zeros(seq, np.int64)
    for window, dilation in DILATED_PATTERNS:
        mult += (dist % dilation == 0) & (dist <= window)
    log_mult = np.where(mult > 0, np.log(np.maximum(mult, 1)), MASKED).astype(np.float32)
    per_dist = rel_bias[_t5_bucket(jnp.asarray(dist, jnp.int32))].astype(F32)
    per_dist = jnp.where(jnp.asarray(mult > 0)[:, None], per_dist + log_mult[:, None], MASKED)
    per_dist = per_dist.T
    base = seq - tile
    heads = per_dist.shape[0]
    masked = jnp.full((heads, tile), MASKED, F32)
    v = jnp.concatenate([per_dist[:, base:], masked, per_dist[:, :base]], axis=1)
    w = jnp.roll(v[:, ::-1], 1, axis=1)
    return w.reshape(heads // 2, 2, 1, seq + tile)


def _attn_kernel(q_ref, k_ref, v_ref, w_ref, o_ref, tab_ref, vext_ref, *, tile):
    seq = q_ref.shape[1]
    nq = seq // tile

    vext_ref[:, :LANES] = v_ref[0]
    vext_ref[:, LANES:] = jnp.ones((seq, LANES), BF16)

    @pl.when(pl.program_id(1) == 0)
    def _():
        for hh in range(2):
            rows = jnp.broadcast_to(w_ref[0, hh], (tile, seq + tile))
            tab_ref[hh] = pltpu.roll(rows, 0, 1, stride=1, stride_axis=0)[:, :seq]

    first = lax.broadcasted_iota(jnp.int32, (tile, LANES), 1) < HEAD_DIM_A
    for qi in range(nq):
        n = (qi + 1) * tile
        off = (nq - 1 - qi) * tile
        q = q_ref[0, qi * tile:(qi + 1) * tile, :]
        zero = jnp.zeros_like(q)
        q2 = jnp.concatenate([jnp.where(first, q, zero), jnp.where(first, zero, q)], axis=0)
        s = lax.dot_general(q2, k_ref[0, :n, :], (((1,), (1,)), ((), ())),
                            preferred_element_type=F32)
        s = s + jnp.concatenate([tab_ref[0, :, off:off + n], tab_ref[1, :, off:off + n]], axis=0)
        p = jnp.exp(s - jnp.max(s, axis=-1, keepdims=True))
        o = _dot(p.astype(BF16), vext_ref[:n, :])
        o = o[:, :LANES] / o[:, LANES:]
        o_ref[0, qi * tile:(qi + 1) * tile, :] = jnp.where(first, o[:tile], o[tile:]).astype(BF16)


def _attention(q, k, v, bias_rows):
    b, s, _ = q.shape
    tile = ATTN_TILE
    seq_spec = pl.BlockSpec((1, s, LANES), lambda h, bi: (bi, 0, h))
    return pl.pallas_call(
        functools.partial(_attn_kernel, tile=tile),
        grid=(WIDTH_A // LANES, b),
        in_specs=[seq_spec, seq_spec, seq_spec,
                  pl.BlockSpec((1, 2, 1, s + tile), lambda h, bi: (h, 0, 0, 0))],
        out_specs=seq_spec,
        out_shape=jax.ShapeDtypeStruct((b, s, WIDTH_A), BF16),
        scratch_shapes=[pltpu.VMEM((2, tile, s), F32), pltpu.VMEM((s, 2 * LANES), BF16)],
        compiler_params=_params("arbitrary", "arbitrary"),
        name="dilated_attention",
    )(q, k, v, bias_rows)


def _shift_rows(a, row, shift):
    return jnp.where(row >= shift, pltpu.roll(a, shift, 0), 0.0)


def _pool_kernel(u_ref, w_ref, sc_ref, o_ref):
    g = pl.program_id(1)
    u = u_ref[0]
    row = lax.broadcasted_iota(jnp.int32, u.shape, 0)
    sums = [u]
    for half in (1, 2, 4, 8):
        sums.append(sums[-1] + _shift_rows(sums[-1], row, half))
    total = sums[4]
    window = jnp.int32(POOL_WINDOWS[3])
    for gi in (2, 1, 0):
        total = jnp.where(g == gi, sums[gi + 1], total)
        window = jnp.where(g == gi, POOL_WINDOWS[gi], window)
    count = jnp.minimum(row + 1, window).astype(F32)
    diff = total / count - u
    o_ref[0] = (_dot(diff.astype(BF16), w_ref[0]) * sc_ref[...]).astype(BF16)


def _pool(u, pool_w, pool_scale):
    b, s, _ = u.shape
    assert POOL_WINDOWS == (2, 4, 8, 16)
    spec = pl.BlockSpec((1, s, POOL_GROUP_DIM), lambda bi, g: (bi, 0, g))
    return pl.pallas_call(
        _pool_kernel,
        grid=(b, N_POOL_GROUPS),
        in_specs=[spec,
                  pl.BlockSpec((1, POOL_GROUP_DIM, POOL_GROUP_DIM), lambda bi, g: (g, 0, 0)),
                  pl.BlockSpec((1, POOL_GROUP_DIM), lambda bi, g: (0, g))],
        out_specs=spec,
        out_shape=jax.ShapeDtypeStruct((b, s, WIDTH_B), BF16),
        compiler_params=_params("parallel", "parallel"),
        name="multi_scale_pool",
    )(u, pool_w, pool_scale.reshape(1, WIDTH_B))


def _out_even_kernel(x_ref, a_ref, p_ref, w_ref, o_ref):
    o_ref[...] = (x_ref[...] + _dot(a_ref[...], w_ref[:WIDTH_A, :])
                  + _dot(p_ref[...], w_ref[WIDTH_A:, :]))


def _out_even(x, attn, pooled, w_out, i):
    t = x.shape[0]
    tm = TOKEN_TILE
    row = lambda n: pl.BlockSpec((tm, n), lambda i: (i, 0))
    return pl.pallas_call(
        _out_even_kernel,
        grid=(t // tm,),
        in_specs=[row(D_MODEL), row(WIDTH_A), row(WIDTH_B),
                  _stacked((WIDTH_A + WIDTH_B, D_MODEL), i)],
        out_specs=row(D_MODEL),
        out_shape=jax.ShapeDtypeStruct((t, D_MODEL), F32),
        compiler_params=_params("parallel"),
        name="out_even",
    )(x, attn, pooled, w_out)


def _odd_in_kernel(x_ref, g_ref, w_ref, cw_ref, cb_ref, wqk_ref, wv_ref, wg_ref, bg_ref,
                   xc_ref, q_ref, k_ref, v_ref, z_ref, gates_ref, gates_t_ref, pad_ref):
    ts = x_ref.shape[1]
    halo = pad_ref.shape[0] - ts

    @pl.when(pl.program_id(1) == 0)
    def _():
        pad_ref[:halo, :] = jnp.zeros((halo, INNER_C), F32)

    xn = _rms(x_ref[0], g_ref[...]).astype(BF16)
    z_ref[0] = _dot(xn, w_ref[:, INNER_C:]).astype(BF16)
    xm = _dot(xn, w_ref[:, :INNER_C])
    pad_ref[halo:, :] = xm
    conv = xm * cw_ref[CONV_WIDTH - 1:CONV_WIDTH, :]
    for back in range(1, CONV_WIDTH):
        tap = CONV_WIDTH - 1 - back
        conv = conv + pad_ref[halo - back:halo - back + ts, :] * cw_ref[tap:tap + 1, :]
    pad_ref[:halo, :] = xm[ts - halo:, :]
    xc16 = _silu(conv + cb_ref[...]).astype(BF16)
    xm16 = xm.astype(BF16)
    xc_ref[0] = xc16
    for gi in range(INNER_C // LANES):
        sl = slice(gi * LANES, (gi + 1) * LANES)
        qk = _dot(xc16[:, sl], wqk_ref[gi]).astype(BF16)
        q_ref[0, :, sl] = qk[:, :LANES]
        k_ref[0, :, sl] = qk[:, LANES:]
        v_ref[0, :, sl] = _dot(xm16[:, sl], wv_ref[gi]).astype(BF16)
    gates = (bg_ref[...] + _dot(q_ref[0], wg_ref[0]) + _dot(k_ref[0], wg_ref[1])
             + _dot(v_ref[0], wg_ref[2]))
    gates_ref[0] = gates
    gates_t_ref[0] = gates.T


def _block_diag_tiles(w):
    per_tile = LANES // QKV_BLOCK
    w = w.reshape(-1, per_tile, QKV_BLOCK, QKV_BLOCK)
    eye = jnp.eye(per_tile, dtype=w.dtype)
    dense = jnp.einsum("tgio,gh->tgiho", w, eye)
    return dense.reshape(-1, LANES, LANES)


def _odd_in(x, gain, w_in, i, conv_w, conv_b, wq, wk, wv, w_gates, b_gates):
    b, s, _ = x.shape
    c = INNER_C
    ts = TOKEN_TILE
    k_scale = HEAD_DIM_C ** -0.5
    n_gates = 2 * N_HEADS_C
    wqk = jnp.concatenate([_block_diag_tiles(wq), _block_diag_tiles(wk) * k_scale], axis=2)
    wg = w_gates.reshape(3, c, n_gates) * jnp.asarray([1.0, 1.0 / k_scale, 1.0], F32)[:, None, None]
    wg = jnp.pad(wg, ((0, 0), (0, 0), (0, LANES - n_gates)))
    bg = jnp.pad(b_gates.reshape(1, n_gates), ((0, 0), (0, LANES - n_gates)))
    rows = lambda n: pl.BlockSpec((1, ts, n), lambda bi, si: (bi, si, 0))
    act = jax.ShapeDtypeStruct((b, s, c), BF16)
    return pl.pallas_call(
        _odd_in_kernel,
        grid=(b, s // ts),
        in_specs=[rows(D_MODEL), _resident((1, D_MODEL)), _stacked((D_MODEL, 2 * c), i),
                  _resident((CONV_WIDTH, c)), _resident((1, c)),
                  _resident((c // LANES, LANES, 2 * LANES)), _resident((c // LANES, LANES, LANES)),
                  _resident((3, c, LANES)), _resident((1, LANES))],
        out_specs=[rows(c), rows(c), rows(c), rows(c), rows(c), rows(LANES),
                   pl.BlockSpec((1, LANES, ts), lambda bi, si: (bi, 0, si))],
        out_shape=[act, act, act, act, act, jax.ShapeDtypeStruct((b, s, LANES), F32),
                   jax.ShapeDtypeStruct((b, LANES, s), F32)],
        scratch_shapes=[pltpu.VMEM((ts + 8, c), F32)],
        compiler_params=_params("parallel", "arbitrary"),
        name="odd_in",
    )(x, gain.reshape(1, D_MODEL), w_in, conv_w, conv_b.reshape(1, c),
      wqk.astype(BF16), _block_diag_tiles(wv).astype(BF16), wg.astype(BF16), bg)


def _log_sigmoid(x):
    return jnp.minimum(x, 0.0) - jnp.log(1.0 + jnp.exp(-jnp.abs(x)))


def _mlstm_head(q, k, v, ig_c, fg_c, ig_r, fg_r, c_ref, n_ref, m_ref):
    chunk = q.shape[0]
    lf_c = _log_sigmoid(fg_c)
    lf_r = _log_sigmoid(fg_r)
    t_idx = lax.broadcasted_iota(jnp.int32, (chunk, chunk), 0)
    s_idx = lax.broadcasted_iota(jnp.int32, (chunk, chunk), 1)
    causal = s_idx <= t_idx
    bcum_c = jnp.sum(jnp.where(causal, lf_r, 0.0), axis=1, keepdims=True)
    bcum_r = jnp.sum(jnp.where(t_idx <= s_idx, lf_c, 0.0), axis=0, keepdims=True)
    b_last = jnp.sum(lf_r, axis=1, keepdims=True)
    d = jnp.where(causal, bcum_c - bcum_r + ig_r, MASKED)
    m_prev = m_ref[...]
    inter = bcum_c + m_prev
    m_t = jnp.maximum(inter, jnp.max(d, axis=1, keepdims=True))
    w_inter = jnp.exp(inter - m_t)
    s_qk = lax.dot_general(q, k, (((1,), (1,)), ((), ())), preferred_element_type=F32)
    s_qk = s_qk * jnp.exp(d - m_t)
    c_state = c_ref[...]
    n_state = n_ref[...]
    kf = k.astype(F32)
    num = w_inter * _dot(q, c_state.astype(BF16)) + _dot(s_qk.astype(BF16), v)
    den = (w_inter * jnp.sum(q.astype(F32) * n_state, axis=1, keepdims=True)
           + jnp.sum(s_qk, axis=1, keepdims=True))
    h = num / jnp.maximum(jnp.abs(den), jnp.exp(-m_t))

    a_r = b_last - bcum_r + ig_r
    a_c = b_last - bcum_c + ig_c
    m_new = jnp.maximum(b_last + m_prev, jnp.max(a_r, axis=1, keepdims=True))
    decay = jnp.exp(b_last + m_prev - m_new)
    kw = kf * jnp.exp(a_c - m_new)
    c_ref[...] = decay * c_state + lax.dot_general(
        kw.astype(BF16), v, (((0,), (0,)), ((), ())), preferred_element_type=F32)
    n_ref[...] = decay * n_state + jnp.sum(kw, axis=0, keepdims=True)
    m_ref[...] = m_new
    return h


def _mlstm_kernel(q_ref, k_ref, v_ref, gc_ref, gr_ref, og_ref, xc_ref, z_ref, skip_ref,
                  o_ref, c_ref, n_ref, m_ref):
    @pl.when(pl.program_id(1) == 0)
    def _():
        c_ref[...] = jnp.zeros_like(c_ref)
        n_ref[...] = jnp.zeros_like(n_ref)
        m_ref[...] = jnp.zeros_like(m_ref)

    e = HEAD_DIM_C
    nh = N_HEADS_C
    gcol = gc_ref[0]
    grow = gr_ref[0]
    for h in range(nh):
        sl = slice(h * e, (h + 1) * e)
        hh = _mlstm_head(q_ref[0, :, sl], k_ref[0, :, sl], v_ref[0, :, sl],
                         gcol[:, h:h + 1], gcol[:, nh + h:nh + h + 1],
                         grow[h:h + 1, :], grow[nh + h:nh + h + 1, :],
                         c_ref.at[h], n_ref.at[h], m_ref.at[h])
        y = _rms(hh, og_ref[:, sl]) + skip_ref[:, sl] * xc_ref[0, :, sl].astype(F32)
        o_ref[0, :, sl] = (y * _silu(z_ref[0, :, sl].astype(F32))).astype(BF16)


def _mlstm(q, k, v, gates, gates_t, out_gain, xc, z, skip):
    b, s, c = q.shape
    chunk = MLSTM_CHUNK
    e = HEAD_DIM_C
    rows = pl.BlockSpec((1, chunk, c), lambda bi, ci: (bi, ci, 0))
    return pl.pallas_call(
        _mlstm_kernel,
        grid=(b, s // chunk),
        in_specs=[rows, rows, rows,
                  pl.BlockSpec((1, chunk, LANES), lambda bi, ci: (bi, ci, 0)),
                  pl.BlockSpec((1, 2 * N_HEADS_C, chunk), lambda bi, ci: (bi, 0, ci)),
                  _resident((1, c)), rows, rows, _resident((1, c))],
        out_specs=rows,
        out_shape=jax.ShapeDtypeStruct((b, s, c), BF16),
        scratch_shapes=[pltpu.VMEM((N_HEADS_C, e, e), F32), pltpu.VMEM((N_HEADS_C, 1, e), F32),
                        pltpu.VMEM((N_HEADS_C, 1, 1), F32)],
        compiler_params=_params("parallel", "arbitrary"),
        name="mlstm_chunkwise",
    )(q, k, v, gates, gates_t, out_gain.reshape(1, c), xc, z, skip.reshape(1, c))


def _out_odd_kernel(x_ref, y_ref, w_ref, o_ref):
    o_ref[...] = x_ref[...] + _dot(y_ref[...], w_ref[...])


def _out_odd(x, y, w_out, i):
    t = x.shape[0]
    tm = TOKEN_TILE
    row = lambda n: pl.BlockSpec((tm, n), lambda i: (i, 0))
    return pl.pallas_call(
        _out_odd_kernel,
        grid=(t // tm,),
        in_specs=[row(D_MODEL), row(INNER_C), _stacked((INNER_C, D_MODEL), i)],
        out_specs=row(D_MODEL),
        out_shape=jax.ShapeDtypeStruct((t, D_MODEL), F32),
        compiler_params=_params("parallel"),
        name="out_odd",
    )(x, y, w_out)


def kernel(x, rel_bias, norm_gains, ffn_w_gate, ffn_w_up, ffn_w_down, ev_w_in, ev_q_gain, ev_k_gain, ev_pool_w, ev_pool_scale, ev_w_out, od_w_in, od_conv_w, od_conv_b, od_wq, od_wk, od_wv, od_w_gates, od_b_gates, od_skip, od_out_gain, od_w_out):
    b, s, d = x.shape
    t = b * s
    x = x.reshape(t, d)
    bias_rows = _attn_bias_rows(rel_bias, s, ATTN_TILE)
    ffn_w = (ffn_w_gate.astype(BF16), ffn_w_up.astype(BF16), ffn_w_down.astype(BF16))
    ev_w_in, ev_w_out = ev_w_in.astype(BF16), ev_w_out.astype(BF16)
    od_w_in, od_w_out = od_w_in.astype(BF16), od_w_out.astype(BF16)
    for layer in range(DEPTH):
        g = norm_gains[layer]
        i = layer // 2
        x = _ffn(x, g[0], *ffn_w, layer, 0)
        if layer % 2 == 0:
            q, k, v, u = _proj_even(x, g[1], ev_w_in, i, ev_q_gain[i], ev_k_gain[i])
            seq = lambda a: a.reshape(b, s, a.shape[-1])
            attn = _attention(seq(q), seq(k), seq(v), bias_rows)
            pooled = _pool(seq(u), ev_pool_w[i].astype(BF16), ev_pool_scale[i])
            x = _out_even(x, attn.reshape(t, WIDTH_A), pooled.reshape(t, WIDTH_B), ev_w_out, i)
        else:
            xc, q, k, v, z, gates, gates_t = _odd_in(
                x.reshape(b, s, d), g[1], od_w_in, i, od_conv_w[i], od_conv_b[i], od_wq[i],
                od_wk[i], od_wv[i], od_w_gates[i], od_b_gates[i])
            y = _mlstm(q, k, v, gates, gates_t, od_out_gain[i], xc, z, od_skip[i])
            x = _out_odd(x, y.reshape(t, INNER_C), od_w_out, i)
        x = _ffn(x, g[2], *ffn_w, layer, 1)
    return x.reshape(b, s, d)
```

```python
import functools
import math

import numpy as np
import jax
import jax.numpy as jnp
from jax import lax
from jax.experimental import pallas as pl
from jax.experimental.pallas import tpu as pltpu

F32 = jnp.float32
BF16 = jnp.bfloat16

D_MODEL = 1024
DEPTH = 4
N_HEADS_A = 8
HEAD_DIM_A = 64
WIDTH_A = N_HEADS_A * HEAD_DIM_A
DILATED_PATTERNS = ((128, 1), (512, 4), (2048, 16))
N_POOL_GROUPS = 4
POOL_WINDOWS = (2, 4, 8, 16)
WIDTH_B = D_MODEL // 2
POOL_GROUP_DIM = WIDTH_B // N_POOL_GROUPS
IN_WIDTH_EVEN = 3 * WIDTH_A + WIDTH_B
REL_BUCKETS = 32
REL_MAX_DISTANCE = 2048
N_HEADS_C = 4
INNER_C = 2 * D_MODEL
HEAD_DIM_C = INNER_C // N_HEADS_C
CONV_WIDTH = 4
QKV_BLOCK = 4
D_FF = 256 * ((8 * D_MODEL // 3 + 255) // 256)
FFN_RESIDUAL = 0.5
RMS_EPS = 1e-6

LANES = 128
VMEM_LIMIT = 56 * 1024 * 1024
MASKED = -1e30

TOKEN_TILE = 512
ATTN_TILE = 256
MLSTM_CHUNK = 256
ODD_CHANNEL_BLOCK = 256


def _params(*sem):
    return pltpu.CompilerParams(dimension_semantics=sem, vmem_limit_bytes=VMEM_LIMIT)


def _resident(shape):
    nd = len(shape)
    return pl.BlockSpec(shape, lambda *_: (0,) * nd, pipeline_mode=pl.Buffered(1))


def _stacked(shape, *lead):
    nd = len(shape)
    return pl.BlockSpec((None,) * len(lead) + tuple(shape), lambda *_: tuple(lead) + (0,) * nd,
                        pipeline_mode=pl.Buffered(1))


def _rms(x, gain):
    return x * lax.rsqrt(jnp.mean(x * x, axis=-1, keepdims=True) + RMS_EPS) * gain


def _silu(x):
    return x / (1.0 + jnp.exp(-x))


def _dot(a, b):
    return jnp.dot(a, b, preferred_element_type=F32)


def _ffn_kernel(x_ref, g_ref, wg_ref, wu_ref, wd_ref, o_ref, *, ff_tile):
    x = x_ref[...]
    xn = _rms(x, g_ref[...]).astype(BF16)
    acc = None
    for c in range(D_FF // ff_tile):
        sl = slice(c * ff_tile, (c + 1) * ff_tile)
        gate = _dot(xn, wg_ref[:, sl])
        up = _dot(xn, wu_ref[:, sl])
        h = (_silu(gate) * up).astype(BF16)
        y = _dot(h, wd_ref[sl, :])
        acc = y if acc is None else acc + y
    o_ref[...] = x + FFN_RESIDUAL * acc


def _ffn(x, gain, w_gate, w_up, w_down, layer, half, *, ff_tile=D_FF):
    t = x.shape[0]
    tm = TOKEN_TILE
    row = pl.BlockSpec((tm, D_MODEL), lambda i: (i, 0))
    return pl.pallas_call(
        functools.partial(_ffn_kernel, ff_tile=ff_tile),
        grid=(t // tm,),
        in_specs=[row, _resident((1, D_MODEL)), _stacked((D_MODEL, D_FF), layer, half),
                  _stacked((D_MODEL, D_FF), layer, half), _stacked((D_FF, D_MODEL), layer, half)],
        out_specs=row,
        out_shape=jax.ShapeDtypeStruct((t, D_MODEL), F32),
        compiler_params=_params("parallel"),
        name="ffn",
    )(x, gain.reshape(1, D_MODEL), w_gate, w_up, w_down)


def _head_sum_matrix():
    r = lax.broadcasted_iota(jnp.int32, (LANES, LANES), 0) // HEAD_DIM_A
    c = lax.broadcasted_iota(jnp.int32, (LANES, LANES), 1) // HEAD_DIM_A
    return jnp.where(r == c, 1.0, 0.0).astype(BF16)


def _head_norm(t, same_head, gain, scale):
    sq = t * t
    hi = sq.astype(BF16)
    lo = (sq - hi.astype(F32)).astype(BF16)
    ss = _dot(hi, same_head) + _dot(lo, same_head)
    return t * lax.rsqrt(ss * (1.0 / HEAD_DIM_A) + RMS_EPS) * (gain * scale)


def _proj_even_kernel(x_ref, g_ref, w_ref, qg_ref, kg_ref, q_ref, k_ref, v_ref, u_ref):
    xn = _rms(x_ref[...], g_ref[...]).astype(BF16)
    proj = _dot(xn, w_ref[...])
    same_head = _head_sum_matrix()
    for j in range(WIDTH_A // LANES):
        sl = slice(j * LANES, (j + 1) * LANES)
        q_ref[:, sl] = _head_norm(proj[:, sl], same_head, qg_ref[...],
                                  HEAD_DIM_A ** -0.5).astype(BF16)
        ks = slice(WIDTH_A + j * LANES, WIDTH_A + (j + 1) * LANES)
        k_ref[:, sl] = _head_norm(proj[:, ks], same_head, kg_ref[...], 1.0).astype(BF16)
    v_ref[...] = proj[:, 2 * WIDTH_A:3 * WIDTH_A].astype(BF16)
    u_ref[...] = proj[:, 3 * WIDTH_A:]


def _proj_even(x, gain, w_in, i, q_gain, k_gain):
    t = x.shape[0]
    tm = TOKEN_TILE
    row = lambda n: pl.BlockSpec((tm, n), lambda i: (i, 0))
    pair = lambda g: jnp.tile(g, LANES // HEAD_DIM_A).reshape(1, LANES)
    return pl.pallas_call(
        _proj_even_kernel,
        grid=(t // tm,),
        in_specs=[row(D_MODEL), _resident((1, D_MODEL)), _stacked((D_MODEL, IN_WIDTH_EVEN), i),
                  _resident((1, LANES)), _resident((1, LANES))],
        out_specs=[row(WIDTH_A), row(WIDTH_A), row(WIDTH_A), row(WIDTH_B)],
        out_shape=[jax.ShapeDtypeStruct((t, WIDTH_A), BF16)] * 3
        + [jax.ShapeDtypeStruct((t, WIDTH_B), F32)],
        compiler_params=_params("parallel"),
        name="proj_even",
    )(x, gain.reshape(1, D_MODEL), w_in, pair(q_gain), pair(k_gain))


def _t5_bucket(distance):
    max_exact = REL_BUCKETS // 2
    d = jnp.maximum(distance.astype(F32), 1.0)
    large = max_exact + (jnp.log(d / max_exact) / math.log(REL_MAX_DISTANCE / max_exact)
                         * (REL_BUCKETS - max_exact)).astype(jnp.int32)
    large = jnp.minimum(large, REL_BUCKETS - 1)
    return jnp.where(distance < max_exact, distance, large)


def _attn_bias_rows(rel_bias, seq, tile):
    dist = np.arange(seq)
    mult = np.zeros(seq, np.int64)
    for window, dilation in DILATED_PATTERNS:
        mult += (dist % dilation == 0) & (dist <= window)
    log_mult = np.where(mult > 0, np.log(np.maximum(mult, 1)), MASKED).astype(np.float32)
    per_dist = rel_bias[_t5_bucket(jnp.asarray(dist, jnp.int32))].astype(F32)
    per_dist = jnp.where(jnp.asarray(mult > 0)[:, None], per_dist + log_mult[:, None], MASKED)
    per_dist = per_dist.T
    base = seq - tile
    heads = per_dist.shape[0]
    masked = jnp.full((heads, tile), MASKED, F32)
    v = jnp.concatenate([per_dist[:, base:], masked, per_dist[:, :base]], axis=1)
    w = jnp.roll(v[:, ::-1], 1, axis=1)
    return w.reshape(heads // 2, 2, 1, seq + tile)


def _attn_kernel(q_ref, k_ref, v_ref, w_ref, o_ref, tab_ref, vext_ref, *, tile):
    seq = q_ref.shape[1]
    nq = seq // tile

    vext_ref[:, :LANES] = v_ref[0]
    vext_ref[:, LANES:] = jnp.ones((seq, LANES), BF16)

    @pl.when(pl.program_id(1) == 0)
    def _():
        for hh in range(2):
            rows = jnp.broadcast_to(w_ref[0, hh], (tile, seq + tile))
            tab_ref[hh] = pltpu.roll(rows, 0, 1, stride=1, stride_axis=0)[:, :seq]

    first = lax.broadcasted_iota(jnp.int32, (tile, LANES), 1) < HEAD_DIM_A
    for qi in range(nq):
        n = (qi + 1) * tile
        off = (nq - 1 - qi) * tile
        q = q_ref[0, qi * tile:(qi + 1) * tile, :]
        zero = jnp.zeros_like(q)
        q2 = jnp.concatenate([jnp.where(first, q, zero), jnp.where(first, zero, q)], axis=0)
        s = lax.dot_general(q2, k_ref[0, :n, :], (((1,), (1,)), ((), ())),
                            preferred_element_type=F32)
        s = s + jnp.concatenate([tab_ref[0, :, off:off + n], tab_ref[1, :, off:off + n]], axis=0)
        p = jnp.exp(s - jnp.max(s, axis=-1, keepdims=True))
        o = _dot(p.astype(BF16), vext_ref[:n, :])
        o = o[:, :LANES] / o[:, LANES:]
        o_ref[0, qi * tile:(qi + 1) * tile, :] = jnp.where(first, o[:tile], o[tile:]).astype(BF16)


def _attention(q, k, v, bias_rows):
    b, s, _ = q.shape
    tile = ATTN_TILE
    seq_spec = pl.BlockSpec((1, s, LANES), lambda h, bi: (bi, 0, h))
    return pl.pallas_call(
        functools.partial(_attn_kernel, tile=tile),
        grid=(WIDTH_A // LANES, b),
        in_specs=[seq_spec, seq_spec, seq_spec,
                  pl.BlockSpec((1, 2, 1, s + tile), lambda h, bi: (h, 0, 0, 0))],
        out_specs=seq_spec,
        out_shape=jax.ShapeDtypeStruct((b, s, WIDTH_A), BF16),
        scratch_shapes=[pltpu.VMEM((2, tile, s), F32), pltpu.VMEM((s, 2 * LANES), BF16)],
        compiler_params=_params("arbitrary", "arbitrary"),
        name="dilated_attention",
    )(q, k, v, bias_rows)


def _shift_rows(a, row, shift):
    return jnp.where(row >= shift, pltpu.roll(a, shift, 0), 0.0)


def _pool_kernel(u_ref, w_ref, sc_ref, o_ref):
    g = pl.program_id(1)
    u = u_ref[0]
    row = lax.broadcasted_iota(jnp.int32, u.shape, 0)
    sums = [u]
    for half in (1, 2, 4, 8):
        sums.append(sums[-1] + _shift_rows(sums[-1], row, half))
    total = sums[4]
    window = jnp.int32(POOL_WINDOWS[3])
    for gi in (2, 1, 0):
        total = jnp.where(g == gi, sums[gi + 1], total)
        window = jnp.where(g == gi, POOL_WINDOWS[gi], window)
    count = jnp.minimum(row + 1, window).astype(F32)
    diff = total / count - u
    o_ref[0] = (_dot(diff.astype(BF16), w_ref[0]) * sc_ref[...]).astype(BF16)


def _pool(u, pool_w, pool_scale):
    b, s, _ = u.shape
    assert POOL_WINDOWS == (2, 4, 8, 16)
    spec = pl.BlockSpec((1, s, POOL_GROUP_DIM), lambda bi, g: (bi, 0, g))
    return pl.pallas_call(
        _pool_kernel,
        grid=(b, N_POOL_GROUPS),
        in_specs=[spec,
                  pl.BlockSpec((1, POOL_GROUP_DIM, POOL_GROUP_DIM), lambda bi, g: (g, 0, 0)),
                  pl.BlockSpec((1, POOL_GROUP_DIM), lambda bi, g: (0, g))],
        out_specs=spec,
        out_shape=jax.ShapeDtypeStruct((b, s, WIDTH_B), BF16),
        compiler_params=_params("parallel", "parallel"),
        name="multi_scale_pool",
    )(u, pool_w, pool_scale.reshape(1, WIDTH_B))


def _out_even_kernel(x_ref, a_ref, p_ref, w_ref, o_ref):
    o_ref[...] = (x_ref[...] + _dot(a_ref[...], w_ref[:WIDTH_A, :])
                  + _dot(p_ref[...], w_ref[WIDTH_A:, :]))


def _out_even(x, attn, pooled, w_out, i):
    t = x.shape[0]
    tm = TOKEN_TILE
    row = lambda n: pl.BlockSpec((tm, n), lambda i: (i, 0))
    return pl.pallas_call(
        _out_even_kernel,
        grid=(t // tm,),
        in_specs=[row(D_MODEL), row(WIDTH_A), row(WIDTH_B),
                  _stacked((WIDTH_A + WIDTH_B, D_MODEL), i)],
        out_specs=row(D_MODEL),
        out_shape=jax.ShapeDtypeStruct((t, D_MODEL), F32),
        compiler_params=_params("parallel"),
        name="out_even",
    )(x, attn, pooled, w_out)


def _shift_rows_halo(a, tail, back, row8):
    rolled = pltpu.roll(a, back, 0)
    head = jnp.where(row8 < back, pltpu.roll(tail, back, 0), rolled[:8])
    return jnp.concatenate([head, rolled[8:]], axis=0)


def _odd_in_kernel(x_ref, g_ref, w_ref, cw_ref, cb_ref, wmix_ref, bg_ref,
                   xc_ref, q_ref, k_ref, v_ref, z_ref, gates_ref, gates_t_ref, tail_ref):
    ts = x_ref.shape[1]
    halo = tail_ref.shape[0]
    cbw = ODD_CHANNEL_BLOCK
    row8 = lax.broadcasted_iota(jnp.int32, (halo, cbw), 0)

    @pl.when(pl.program_id(1) == 0)
    def _():
        tail_ref[...] = jnp.zeros_like(tail_ref)

    xn = _rms(x_ref[0], g_ref[...]).astype(BF16)
    gates = jnp.broadcast_to(bg_ref[...], (ts, LANES))
    n_blocks = INNER_C // cbw
    xm_next = _dot(xn, w_ref[:, :cbw])
    for cb in range(n_blocks):
        cs = slice(cb * cbw, (cb + 1) * cbw)
        xm = xm_next
        if cb + 1 < n_blocks:
            xm_next = _dot(xn, w_ref[:, (cb + 1) * cbw:(cb + 2) * cbw])
        z_ref[0, :, cs] = _dot(xn, w_ref[:, INNER_C + cb * cbw:INNER_C + (cb + 1) * cbw]
                               ).astype(BF16)
        tail = tail_ref[:, cs]
        tail_ref[:, cs] = xm[ts - halo:, :]
        w0, w1, w2, w3 = (cw_ref[tap:tap + 1, cs] for tap in range(CONV_WIDTH))
        xm_1 = _shift_rows_halo(xm, tail, 1, row8)
        far = xm * w1 + xm_1 * w0
        far_tail = tail * w1 + pltpu.roll(tail, 1, 0) * w0
        conv = xm * w3 + xm_1 * w2 + _shift_rows_halo(far, far_tail, 2, row8)
        xc16 = _silu(conv + cb_ref[:, cs]).astype(BF16)
        xm16 = xm.astype(BF16)
        xc_ref[0, :, cs] = xc16
        for gl in range(cbw // LANES):
            ls = slice(gl * LANES, (gl + 1) * LANES)
            gi = cb * (cbw // LANES) + gl
            sl = slice(gi * LANES, (gi + 1) * LANES)
            out = _dot(jnp.concatenate([xc16[:, ls], xm16[:, ls]], axis=1), wmix_ref[gi])
            q_ref[0, :, sl] = out[:, :LANES].astype(BF16)
            k_ref[0, :, sl] = out[:, LANES:2 * LANES].astype(BF16)
            v_ref[0, :, sl] = out[:, 2 * LANES:3 * LANES].astype(BF16)
            gates = gates + out[:, 3 * LANES:]
    gates_ref[0] = gates
    gates_t_ref[0] = gates.T


def _block_diag_tiles(w):
    per_tile = LANES // QKV_BLOCK
    w = w.reshape(-1, per_tile, QKV_BLOCK, QKV_BLOCK)
    eye = jnp.eye(per_tile, dtype=w.dtype)
    dense = jnp.einsum("tgio,gh->tgiho", w, eye)
    return dense.reshape(-1, LANES, LANES)


def _odd_in(x, gain, w_in, i, conv_w, conv_b, wq, wk, wv, w_gates, b_gates):
    b, s, _ = x.shape
    c = INNER_C
    ts = TOKEN_TILE
    n_gates = 2 * N_HEADS_C
    wq_t, wk_t, wv_t = _block_diag_tiles(wq), _block_diag_tiles(wk), _block_diag_tiles(wv)
    wg = w_gates.reshape(3, c // LANES, LANES, n_gates)
    compose = lambda w_t, g: jnp.einsum("tij,tjn->tin", w_t, g, precision=lax.Precision.HIGHEST)
    pad_gates = lambda g: jnp.pad(g, ((0, 0), (0, 0), (0, LANES - n_gates)))
    zeros = jnp.zeros_like(wq_t)
    wmix = jnp.concatenate([
        jnp.concatenate([wq_t, wk_t * HEAD_DIM_C ** -0.5, zeros,
                         pad_gates(compose(wq_t, wg[0]) + compose(wk_t, wg[1]))], axis=2),
        jnp.concatenate([zeros, zeros, wv_t, pad_gates(compose(wv_t, wg[2]))], axis=2)], axis=1)
    bg = jnp.pad(b_gates.reshape(1, n_gates), ((0, 0), (0, LANES - n_gates)))
    rows = lambda n: pl.BlockSpec((1, ts, n), lambda bi, si: (bi, si, 0))
    act = jax.ShapeDtypeStruct((b, s, c), BF16)
    return pl.pallas_call(
        _odd_in_kernel,
        grid=(b, s // ts),
        in_specs=[rows(D_MODEL), _resident((1, D_MODEL)), _stacked((D_MODEL, 2 * c), i),
                  _resident((CONV_WIDTH, c)), _resident((1, c)),
                  _resident((c // LANES, 2 * LANES, 4 * LANES)), _resident((1, LANES))],
        out_specs=[rows(c), rows(c), rows(c), rows(c), rows(c), rows(LANES),
                   pl.BlockSpec((1, LANES, ts), lambda bi, si: (bi, 0, si))],
        out_shape=[act, act, act, act, act, jax.ShapeDtypeStruct((b, s, LANES), F32),
                   jax.ShapeDtypeStruct((b, LANES, s), F32)],
        scratch_shapes=[pltpu.VMEM((8, c), F32)],
        compiler_params=_params("parallel", "arbitrary"),
        name="odd_in",
    )(x, gain.reshape(1, D_MODEL), w_in, conv_w, conv_b.reshape(1, c), wmix.astype(BF16), bg)


def _log_sigmoid(x):
    return jnp.minimum(x, 0.0) - jnp.log(1.0 + jnp.exp(-jnp.abs(x)))


def _mlstm_head(q, k, v, ig_c, fg_c, ig_r, fg_r, c_ref, n_ref, m_ref):
    chunk = q.shape[0]
    lf_c = _log_sigmoid(fg_c)
    lf_r = _log_sigmoid(fg_r)
    t_idx = lax.broadcasted_iota(jnp.int32, (chunk, chunk), 0)
    s_idx = lax.broadcasted_iota(jnp.int32, (chunk, chunk), 1)
    causal = s_idx <= t_idx
    bcum_c = jnp.sum(jnp.where(causal, lf_r, 0.0), axis=1, keepdims=True)
    bcum_r = jnp.sum(jnp.where(t_idx <= s_idx, lf_c, 0.0), axis=0, keepdims=True)
    b_last = jnp.sum(lf_r, axis=1, keepdims=True)
    d = jnp.where(causal, bcum_c - bcum_r + ig_r, MASKED)
    m_prev = m_ref[...]
    inter = bcum_c + m_prev
    m_t = jnp.maximum(inter, jnp.max(d, axis=1, keepdims=True))
    w_inter = jnp.exp(inter - m_t)
    s_qk = lax.dot_general(q, k, (((1,), (1,)), ((), ())), preferred_element_type=F32)
    s_qk = s_qk * jnp.exp(d - m_t)
    c_state = c_ref[...]
    n_state = n_ref[...]
    kf = k.astype(F32)
    num = w_inter * _dot(q, c_state.astype(BF16)) + _dot(s_qk.astype(BF16), v)
    den = (w_inter * jnp.sum(q.astype(F32) * n_state, axis=1, keepdims=True)
           + jnp.sum(s_qk, axis=1, keepdims=True))
    h = num / jnp.maximum(jnp.abs(den), jnp.exp(-m_t))

    a_r = b_last - bcum_r + ig_r
    a_c = b_last - bcum_c + ig_c
    m_new = jnp.maximum(b_last + m_prev, jnp.max(a_r, axis=1, keepdims=True))
    decay = jnp.exp(b_last + m_prev - m_new)
    kw = kf * jnp.exp(a_c - m_new)
    c_ref[...] = decay * c_state + lax.dot_general(
        kw.astype(BF16), v, (((0,), (0,)), ((), ())), preferred_element_type=F32)
    n_ref[...] = decay * n_state + jnp.sum(kw, axis=0, keepdims=True)
    m_ref[...] = m_new
    return h


def _mlstm_kernel(q_ref, k_ref, v_ref, gc_ref, gr_ref, og_ref, xc_ref, z_ref, skip_ref,
                  o_ref, c_ref, n_ref, m_ref):
    @pl.when(pl.program_id(1) == 0)
    def _():
        c_ref[...] = jnp.zeros_like(c_ref)
        n_ref[...] = jnp.zeros_like(n_ref)
        m_ref[...] = jnp.zeros_like(m_ref)

    e = HEAD_DIM_C
    nh = N_HEADS_C
    gcol = gc_ref[0]
    grow = gr_ref[0]
    for h in range(nh):
        sl = slice(h * e, (h + 1) * e)
        hh = _mlstm_head(q_ref[0, :, sl], k_ref[0, :, sl], v_ref[0, :, sl],
                         gcol[:, h:h + 1], gcol[:, nh + h:nh + h + 1],
                         grow[h:h + 1, :], grow[nh + h:nh + h + 1, :],
                         c_ref.at[h], n_ref.at[h], m_ref.at[h])
        y = _rms(hh, og_ref[:, sl]) + skip_ref[:, sl] * xc_ref[0, :, sl].astype(F32)
        o_ref[0, :, sl] = (y * _silu(z_ref[0, :, sl].astype(F32))).astype(BF16)


def _mlstm(q, k, v, gates, gates_t, out_gain, xc, z, skip):
    b, s, c = q.shape
    chunk = MLSTM_CHUNK
    e = HEAD_DIM_C
    rows = pl.BlockSpec((1, chunk, c), lambda bi, ci: (bi, ci, 0))
    return pl.pallas_call(
        _mlstm_kernel,
        grid=(b, s // chunk),
        in_specs=[rows, rows, rows,
                  pl.BlockSpec((1, chunk, LANES), lambda bi, ci: (bi, ci, 0)),
                  pl.BlockSpec((1, 2 * N_HEADS_C, chunk), lambda bi, ci: (bi, 0, ci)),
                  _resident((1, c)), rows, rows, _resident((1, c))],
        out_specs=rows,
        out_shape=jax.ShapeDtypeStruct((b, s, c), BF16),
        scratch_shapes=[pltpu.VMEM((N_HEADS_C, e, e), F32), pltpu.VMEM((N_HEADS_C, 1, e), F32),
                        pltpu.VMEM((N_HEADS_C, 1, 1), F32)],
        compiler_params=_params("parallel", "arbitrary"),
        name="mlstm_chunkwise",
    )(q, k, v, gates, gates_t, out_gain.reshape(1, c), xc, z, skip.reshape(1, c))


def _out_odd_kernel(x_ref, y_ref, w_ref, o_ref):
    o_ref[...] = x_ref[...] + _dot(y_ref[...], w_ref[...])


def _out_odd(x, y, w_out, i):
    t = x.shape[0]
    tm = TOKEN_TILE
    row = lambda n: pl.BlockSpec((tm, n), lambda i: (i, 0))
    return pl.pallas_call(
        _out_odd_kernel,
        grid=(t // tm,),
        in_specs=[row(D_MODEL), row(INNER_C), _stacked((INNER_C, D_MODEL), i)],
        out_specs=row(D_MODEL),
        out_shape=jax.ShapeDtypeStruct((t, D_MODEL), F32),
        compiler_params=_params("parallel"),
        name="out_odd",
    )(x, y, w_out)


def kernel(x, rel_bias, norm_gains, ffn_w_gate, ffn_w_up, ffn_w_down, ev_w_in, ev_q_gain, ev_k_gain, ev_pool_w, ev_pool_scale, ev_w_out, od_w_in, od_conv_w, od_conv_b, od_wq, od_wk, od_wv, od_w_gates, od_b_gates, od_skip, od_out_gain, od_w_out):
    b, s, d = x.shape
    t = b * s
    x = x.reshape(t, d)
    bias_rows = _attn_bias_rows(rel_bias, s, ATTN_TILE)
    ffn_w = (ffn_w_gate.astype(BF16), ffn_w_up.astype(BF16), ffn_w_down.astype(BF16))
    ev_w_in, ev_w_out = ev_w_in.astype(BF16), ev_w_out.astype(BF16)
    od_w_in, od_w_out = od_w_in.astype(BF16), od_w_out.astype(BF16)
    for layer in range(DEPTH):
        g = norm_gains[layer]
        i = layer // 2
        x = _ffn(x, g[0], *ffn_w, layer, 0)
        if layer % 2 == 0:
            q, k, v, u = _proj_even(x, g[1], ev_w_in, i, ev_q_gain[i], ev_k_gain[i])
            seq = lambda a: a.reshape(b, s, a.shape[-1])
            attn = _attention(seq(q), seq(k), seq(v), bias_rows)
            pooled = _pool(seq(u), ev_pool_w[i].astype(BF16), ev_pool_scale[i])
            x = _out_even(x, attn.reshape(t, WIDTH_A), pooled.reshape(t, WIDTH_B), ev_w_out, i)
        else:
            xc, q, k, v, z, gates, gates_t = _odd_in(
                x.reshape(b, s, d), g[1], od_w_in, i, od_conv_w[i], od_conv_b[i], od_wq[i],
                od_wk[i], od_wv[i], od_w_gates[i], od_b_gates[i])
            y = _mlstm(q, k, v, gates, gates_t, od_out_gain[i], xc, z, od_skip[i])
            x = _out_odd(x, y.reshape(t, INNER_C), od_w_out, i)
        x = _ffn(x, g[2], *ffn_w, layer, 1)
    return x.reshape(b, s, d)
```

```python
import functools
import math

import numpy as np
import jax
import jax.numpy as jnp
from jax import lax
from jax.experimental import pallas as pl
from jax.experimental.pallas import tpu as pltpu

F32 = jnp.float32
BF16 = jnp.bfloat16

D_MODEL = 1024
DEPTH = 4
N_HEADS_A = 8
HEAD_DIM_A = 64
WIDTH_A = N_HEADS_A * HEAD_DIM_A
DILATED_PATTERNS = ((128, 1), (512, 4), (2048, 16))
N_POOL_GROUPS = 4
POOL_WINDOWS = (2, 4, 8, 16)
WIDTH_B = D_MODEL // 2
POOL_GROUP_DIM = WIDTH_B // N_POOL_GROUPS
IN_WIDTH_EVEN = 3 * WIDTH_A + WIDTH_B
REL_BUCKETS = 32
REL_MAX_DISTANCE = 2048
N_HEADS_C = 4
INNER_C = 2 * D_MODEL
HEAD_DIM_C = INNER_C // N_HEADS_C
CONV_WIDTH = 4
QKV_BLOCK = 4
D_FF = 256 * ((8 * D_MODEL // 3 + 255) // 256)
FFN_RESIDUAL = 0.5
RMS_EPS = 1e-6

LANES = 128
VMEM_LIMIT = 56 * 1024 * 1024
MASKED = -1e30

TOKEN_TILE = 512
ATTN_TILE = 256
MLSTM_CHUNK = 256
ODD_CHANNEL_BLOCK = 256


def _params(*sem):
    return pltpu.CompilerParams(dimension_semantics=sem, vmem_limit_bytes=VMEM_LIMIT)


def _resident(shape):
    nd = len(shape)
    return pl.BlockSpec(shape, lambda *_: (0,) * nd, pipeline_mode=pl.Buffered(1))


def _stacked(shape, *lead):
    nd = len(shape)
    return pl.BlockSpec((None,) * len(lead) + tuple(shape), lambda *_: tuple(lead) + (0,) * nd,
                        pipeline_mode=pl.Buffered(1))


def _rms(x, gain):
    return x * lax.rsqrt(jnp.mean(x * x, axis=-1, keepdims=True) + RMS_EPS) * gain


def _silu(x):
    return x / (1.0 + jnp.exp(-x))


def _dot(a, b):
    return jnp.dot(a, b, preferred_element_type=F32)


def _ffn_kernel(x_ref, g_ref, wg_ref, wu_ref, wd_ref, o_ref, *, ff_tile):
    x = x_ref[...]
    xn = _rms(x, g_ref[...]).astype(BF16)
    acc = None
    for c in range(D_FF // ff_tile):
        sl = slice(c * ff_tile, (c + 1) * ff_tile)
        gate = _dot(xn, wg_ref[:, sl])
        up = _dot(xn, wu_ref[:, sl])
        h = (_silu(gate) * up).astype(BF16)
        y = _dot(h, wd_ref[sl, :])
        acc = y if acc is None else acc + y
    o_ref[...] = x + FFN_RESIDUAL * acc


def _ffn(x, gain, w_gate, w_up, w_down, layer, half, *, ff_tile=D_FF):
    t = x.shape[0]
    tm = TOKEN_TILE
    row = pl.BlockSpec((tm, D_MODEL), lambda i: (i, 0))
    return pl.pallas_call(
        functools.partial(_ffn_kernel, ff_tile=ff_tile),
        grid=(t // tm,),
        in_specs=[row, _resident((1, D_MODEL)), _stacked((D_MODEL, D_FF), layer, half),
                  _stacked((D_MODEL, D_FF), layer, half), _stacked((D_FF, D_MODEL), layer, half)],
        out_specs=row,
        out_shape=jax.ShapeDtypeStruct((t, D_MODEL), F32),
        compiler_params=_params("parallel"),
        name="ffn",
    )(x, gain.reshape(1, D_MODEL), w_gate, w_up, w_down)


NORM_SLAB = 2 * LANES


def _head_sum_matrix():
    r = lax.broadcasted_iota(jnp.int32, (NORM_SLAB, NORM_SLAB), 0) // HEAD_DIM_A
    c = lax.broadcasted_iota(jnp.int32, (NORM_SLAB, NORM_SLAB), 1) // HEAD_DIM_A
    return jnp.where(r == c, 1.0, 0.0).astype(BF16)


def _head_norm(t, same_head, gain, scale):
    ss = _dot((t * t).astype(BF16), same_head)
    return t * lax.rsqrt(ss * (1.0 / HEAD_DIM_A) + RMS_EPS) * (gain * scale)


def _proj_even_kernel(x_ref, g_ref, w_ref, qg_ref, kg_ref, q_ref, k_ref, v_ref, u_ref):
    xn = _rms(x_ref[...], g_ref[...]).astype(BF16)
    proj = _dot(xn, w_ref[...])
    same_head = _head_sum_matrix()
    for j in range(WIDTH_A // NORM_SLAB):
        sl = slice(j * NORM_SLAB, (j + 1) * NORM_SLAB)
        q_ref[:, sl] = _head_norm(proj[:, sl], same_head, qg_ref[...],
                                  HEAD_DIM_A ** -0.5).astype(BF16)
        ks = slice(WIDTH_A + j * NORM_SLAB, WIDTH_A + (j + 1) * NORM_SLAB)
        k_ref[:, sl] = _head_norm(proj[:, ks], same_head, kg_ref[...], 1.0).astype(BF16)
    v_ref[...] = proj[:, 2 * WIDTH_A:3 * WIDTH_A].astype(BF16)
    u_ref[...] = proj[:, 3 * WIDTH_A:]


def _proj_even(x, gain, w_in, i, q_gain, k_gain):
    t = x.shape[0]
    tm = TOKEN_TILE
    row = lambda n: pl.BlockSpec((tm, n), lambda i: (i, 0))
    pair = lambda g: jnp.tile(g, NORM_SLAB // HEAD_DIM_A).reshape(1, NORM_SLAB)
    return pl.pallas_call(
        _proj_even_kernel,
        grid=(t // tm,),
        in_specs=[row(D_MODEL), _resident((1, D_MODEL)), _stacked((D_MODEL, IN_WIDTH_EVEN), i),
                  _resident((1, NORM_SLAB)), _resident((1, NORM_SLAB))],
        out_specs=[row(WIDTH_A), row(WIDTH_A), row(WIDTH_A), row(WIDTH_B)],
        out_shape=[jax.ShapeDtypeStruct((t, WIDTH_A), BF16)] * 3
        + [jax.ShapeDtypeStruct((t, WIDTH_B), F32)],
        compiler_params=_params("parallel"),
        name="proj_even",
    )(x, gain.reshape(1, D_MODEL), w_in, pair(q_gain), pair(k_gain))


def _t5_bucket(distance):
    max_exact = REL_BUCKETS // 2
    d = jnp.maximum(distance.astype(F32), 1.0)
    large = max_exact + (jnp.log(d / max_exact) / math.log(REL_MAX_DISTANCE / max_exact)
                         * (REL_BUCKETS - max_exact)).astype(jnp.int32)
    large = jnp.minimum(large, REL_BUCKETS - 1)
    return jnp.where(distance < max_exact, distance, large)


def _attn_bias_rows(rel_bias, seq, tile):
    dist = np.arange(seq)
    mult = np.zeros(seq, np.int64)
    for window, dilation in DILATED_PATTERNS:
        mult += (dist % dilation == 0) & (dist <= window)
    log_mult = np.where(mult > 0, np.log(np.maximum(mult, 1)), MASKED).astype(np.float32)
    per_dist = rel_bias[_t5_bucket(jnp.asarray(dist, jnp.int32))].astype(F32)
    per_dist = jnp.where(jnp.asarray(mult > 0)[:, None], per_dist + log_mult[:, None], MASKED)
    per_dist = per_dist.T
    base = seq - tile
    heads = per_dist.shape[0]
    masked = jnp.full((heads, tile), MASKED, F32)
    v = jnp.concatenate([per_dist[:, base:], masked, per_dist[:, :base]], axis=1)
    w = jnp.roll(v[:, ::-1], 1, axis=1)
    return w.reshape(heads // 2, 2, 1, seq + tile)


def _attn_kernel(q_ref, k_ref, v_ref, w_ref, o_ref, tab_ref, vext_ref, *, tile):
    seq = q_ref.shape[1]
    nq = seq // tile

    vext_ref[:, :LANES] = v_ref[0]
    vext_ref[:, LANES:] = jnp.ones((seq, LANES), BF16)

    @pl.when(pl.program_id(1) == 0)
    def _():
        for hh in range(2):
            rows = jnp.broadcast_to(w_ref[0, hh], (tile, seq + tile))
            tab_ref[hh] = pltpu.roll(rows, 0, 1, stride=1, stride_axis=0)[:, :seq]

    first = lax.broadcasted_iota(jnp.int32, (tile, LANES), 1) < HEAD_DIM_A
    for qi in range(nq):
        n = (qi + 1) * tile
        off = (nq - 1 - qi) * tile
        q = q_ref[0, qi * tile:(qi + 1) * tile, :]
        zero = jnp.zeros_like(q)
        q2 = jnp.concatenate([jnp.where(first, q, zero), jnp.where(first, zero, q)], axis=0)
        s = lax.dot_general(q2, k_ref[0, :n, :], (((1,), (1,)), ((), ())),
                            preferred_element_type=F32)
        s = s + jnp.concatenate([tab_ref[0, :, off:off + n], tab_ref[1, :, off:off + n]], axis=0)
        p = jnp.exp(s - jnp.max(s, axis=-1, keepdims=True))
        o = _dot(p.astype(BF16), vext_ref[:n, :])
        o = o[:, :LANES] / o[:, LANES:]
        o_ref[0, qi * tile:(qi + 1) * tile, :] = jnp.where(first, o[:tile], o[tile:]).astype(BF16)


def _attention(q, k, v, bias_rows):
    b, s, _ = q.shape
    tile = ATTN_TILE
    seq_spec = pl.BlockSpec((1, s, LANES), lambda h, bi: (bi, 0, h))
    return pl.pallas_call(
        functools.partial(_attn_kernel, tile=tile),
        grid=(WIDTH_A // LANES, b),
        in_specs=[seq_spec, seq_spec, seq_spec,
                  pl.BlockSpec((1, 2, 1, s + tile), lambda h, bi: (h, 0, 0, 0))],
        out_specs=seq_spec,
        out_shape=jax.ShapeDtypeStruct((b, s, WIDTH_A), BF16),
        scratch_shapes=[pltpu.VMEM((2, tile, s), F32), pltpu.VMEM((s, 2 * LANES), BF16)],
        compiler_params=_params("arbitrary", "arbitrary"),
        name="dilated_attention",
    )(q, k, v, bias_rows)


def _shift_rows(a, row, shift):
    return jnp.where(row >= shift, pltpu.roll(a, shift, 0), 0.0)


def _pool_kernel(u_ref, w_ref, sc_ref, o_ref):
    g = pl.program_id(1)
    u = u_ref[0]
    row = lax.broadcasted_iota(jnp.int32, u.shape, 0)
    sums = [u]
    for half in (1, 2, 4, 8):
        sums.append(sums[-1] + _shift_rows(sums[-1], row, half))
    total = sums[4]
    window = jnp.int32(POOL_WINDOWS[3])
    for gi in (2, 1, 0):
        total = jnp.where(g == gi, sums[gi + 1], total)
        window = jnp.where(g == gi, POOL_WINDOWS[gi], window)
    count = jnp.minimum(row + 1, window).astype(F32)
    diff = total / count - u
    o_ref[0] = (_dot(diff.astype(BF16), w_ref[0]) * sc_ref[...]).astype(BF16)


def _pool(u, pool_w, pool_scale):
    b, s, _ = u.shape
    assert POOL_WINDOWS == (2, 4, 8, 16)
    spec = pl.BlockSpec((1, s, POOL_GROUP_DIM), lambda bi, g: (bi, 0, g))
    return pl.pallas_call(
        _pool_kernel,
        grid=(b, N_POOL_GROUPS),
        in_specs=[spec,
                  pl.BlockSpec((1, POOL_GROUP_DIM, POOL_GROUP_DIM), lambda bi, g: (g, 0, 0)),
                  pl.BlockSpec((1, POOL_GROUP_DIM), lambda bi, g: (0, g))],
        out_specs=spec,
        out_shape=jax.ShapeDtypeStruct((b, s, WIDTH_B), BF16),
        compiler_params=_params("parallel", "parallel"),
        name="multi_scale_pool",
    )(u, pool_w, pool_scale.reshape(1, WIDTH_B))


def _out_even_kernel(x_ref, a_ref, p_ref, w_ref, o_ref):
    o_ref[...] = (x_ref[...] + _dot(a_ref[...], w_ref[:WIDTH_A, :])
                  + _dot(p_ref[...], w_ref[WIDTH_A:, :]))


def _out_even(x, attn, pooled, w_out, i):
    t = x.shape[0]
    tm = TOKEN_TILE
    row = lambda n: pl.BlockSpec((tm, n), lambda i: (i, 0))
    return pl.pallas_call(
        _out_even_kernel,
        grid=(t // tm,),
        in_specs=[row(D_MODEL), row(WIDTH_A), row(WIDTH_B),
                  _stacked((WIDTH_A + WIDTH_B, D_MODEL), i)],
        out_specs=row(D_MODEL),
        out_shape=jax.ShapeDtypeStruct((t, D_MODEL), F32),
        compiler_params=_params("parallel"),
        name="out_even",
    )(x, attn, pooled, w_out)


def _shift_rows_halo(a, tail, back, row8):
    rolled = pltpu.roll(a, back, 0)
    head = jnp.where(row8 < back, pltpu.roll(tail, back, 0), rolled[:8])
    return jnp.concatenate([head, rolled[8:]], axis=0)


def _odd_in_kernel(x_ref, g_ref, w_ref, cw_ref, cb_ref, wmix_ref, bg_ref,
                   xc_ref, q_ref, k_ref, v_ref, z_ref, gates_ref, gates_t_ref, tail_ref):
    ts = x_ref.shape[1]
    halo = tail_ref.shape[0]
    cbw = ODD_CHANNEL_BLOCK
    row8 = lax.broadcasted_iota(jnp.int32, (halo, cbw), 0)

    @pl.when(pl.program_id(1) == 0)
    def _():
        tail_ref[...] = jnp.zeros_like(tail_ref)

    xn = _rms(x_ref[0], g_ref[...]).astype(BF16)
    gates = jnp.broadcast_to(bg_ref[...], (ts, LANES))
    n_blocks = INNER_C // cbw
    xm_next = _dot(xn, w_ref[:, :cbw])
    for cb in range(n_blocks):
        cs = slice(cb * cbw, (cb + 1) * cbw)
        xm = xm_next
        if cb + 1 < n_blocks:
            xm_next = _dot(xn, w_ref[:, (cb + 1) * cbw:(cb + 2) * cbw])
        z_ref[0, :, cs] = _dot(xn, w_ref[:, INNER_C + cb * cbw:INNER_C + (cb + 1) * cbw]
                               ).astype(BF16)
        tail = tail_ref[:, cs]
        tail_ref[:, cs] = xm[ts - halo:, :]
        w0, w1, w2, w3 = (cw_ref[tap:tap + 1, cs] for tap in range(CONV_WIDTH))
        xm_1 = _shift_rows_halo(xm, tail, 1, row8)
        far = xm * w1 + xm_1 * w0
        far_tail = tail * w1 + pltpu.roll(tail, 1, 0) * w0
        conv = xm * w3 + xm_1 * w2 + _shift_rows_halo(far, far_tail, 2, row8)
        xc16 = _silu(conv + cb_ref[:, cs]).astype(BF16)
        xm16 = xm.astype(BF16)
        xc_ref[0, :, cs] = xc16
        for gl in range(cbw // LANES):
            ls = slice(gl * LANES, (gl + 1) * LANES)
            gi = cb * (cbw // LANES) + gl
            sl = slice(gi * LANES, (gi + 1) * LANES)
            out = _dot(jnp.concatenate([xc16[:, ls], xm16[:, ls]], axis=1), wmix_ref[gi])
            q_ref[0, :, sl] = out[:, :LANES].astype(BF16)
            k_ref[0, :, sl] = out[:, LANES:2 * LANES].astype(BF16)
            v_ref[0, :, sl] = out[:, 2 * LANES:3 * LANES].astype(BF16)
            gates = gates + out[:, 3 * LANES:]
    gates_ref[0] = gates
    gates_t_ref[0] = gates.T


def _block_diag_tiles(w):
    per_tile = LANES // QKV_BLOCK
    w = w.reshape(-1, per_tile, QKV_BLOCK, QKV_BLOCK)
    eye = jnp.eye(per_tile, dtype=w.dtype)
    dense = jnp.einsum("tgio,gh->tgiho", w, eye)
    return dense.reshape(-1, LANES, LANES)


def _odd_in(x, gain, w_in, i, conv_w, conv_b, wq, wk, wv, w_gates, b_gates):
    b, s, _ = x.shape
    c = INNER_C
    ts = TOKEN_TILE
    n_gates = 2 * N_HEADS_C
    wq_t, wk_t, wv_t = _block_diag_tiles(wq), _block_diag_tiles(wk), _block_diag_tiles(wv)
    wg = w_gates.reshape(3, c // LANES, LANES, n_gates)
    compose = lambda w_t, g: jnp.einsum("tij,tjn->tin", w_t, g, precision=lax.Precision.HIGHEST)
    pad_gates = lambda g: jnp.pad(g, ((0, 0), (0, 0), (0, LANES - n_gates)))
    zeros = jnp.zeros_like(wq_t)
    wmix = jnp.concatenate([
        jnp.concatenate([wq_t, wk_t * HEAD_DIM_C ** -0.5, zeros,
                         pad_gates(compose(wq_t, wg[0]) + compose(wk_t, wg[1]))], axis=2),
        jnp.concatenate([zeros, zeros, wv_t, pad_gates(compose(wv_t, wg[2]))], axis=2)], axis=1)
    bg = jnp.pad(b_gates.reshape(1, n_gates), ((0, 0), (0, LANES - n_gates)))
    rows = lambda n: pl.BlockSpec((1, ts, n), lambda bi, si: (bi, si, 0))
    act = jax.ShapeDtypeStruct((b, s, c), BF16)
    return pl.pallas_call(
        _odd_in_kernel,
        grid=(b, s // ts),
        in_specs=[rows(D_MODEL), _resident((1, D_MODEL)), _stacked((D_MODEL, 2 * c), i),
                  _resident((CONV_WIDTH, c)), _resident((1, c)),
                  _resident((c // LANES, 2 * LANES, 4 * LANES)), _resident((1, LANES))],
        out_specs=[rows(c), rows(c), rows(c), rows(c), rows(c), rows(LANES),
                   pl.BlockSpec((1, LANES, ts), lambda bi, si: (bi, 0, si))],
        out_shape=[act, act, act, act, act, jax.ShapeDtypeStruct((b, s, LANES), F32),
                   jax.ShapeDtypeStruct((b, LANES, s), F32)],
        scratch_shapes=[pltpu.VMEM((8, c), F32)],
        compiler_params=_params("parallel", "arbitrary"),
        name="odd_in",
    )(x, gain.reshape(1, D_MODEL), w_in, conv_w, conv_b.reshape(1, c), wmix.astype(BF16), bg)


def _log_sigmoid(x):
    return jnp.minimum(x, 0.0) - jnp.log(1.0 + jnp.exp(-jnp.abs(x)))


def _mlstm_head(q, k, v, ig_c, fg_c, ig_r, fg_r, c_ref, m_ref):
    chunk, e = q.shape
    v_ext = jnp.concatenate([v, jnp.ones((chunk, LANES), BF16)], axis=1)
    lf_c = _log_sigmoid(fg_c)
    lf_r = _log_sigmoid(fg_r)
    t_idx = lax.broadcasted_iota(jnp.int32, (chunk, chunk), 0)
    s_idx = lax.broadcasted_iota(jnp.int32, (chunk, chunk), 1)
    causal = s_idx <= t_idx
    bcum_c = jnp.sum(jnp.where(causal, lf_r, 0.0), axis=1, keepdims=True)
    bcum_r = jnp.sum(jnp.where(t_idx <= s_idx, lf_c, 0.0), axis=0, keepdims=True)
    b_last = jnp.sum(lf_r, axis=1, keepdims=True)
    d = jnp.where(causal, bcum_c - bcum_r + ig_r, MASKED)
    m_prev = m_ref[...]
    inter = bcum_c + m_prev
    m_t = jnp.maximum(inter, jnp.max(d, axis=1, keepdims=True))
    w_inter = jnp.exp(inter - m_t)
    s_qk = lax.dot_general(q, k, (((1,), (1,)), ((), ())), preferred_element_type=F32)
    s_qk = (s_qk * jnp.exp(d - m_t)).astype(BF16)
    state = c_ref[...]
    both = w_inter * _dot(q, state.astype(BF16)) + _dot(s_qk, v_ext)
    inv = 1.0 / jnp.maximum(jnp.abs(both[:, e:]), jnp.exp(-m_t))

    a_r = b_last - bcum_r + ig_r
    a_c = b_last - bcum_c + ig_c
    m_new = jnp.maximum(b_last + m_prev, jnp.max(a_r, axis=1, keepdims=True))
    decay = jnp.exp(b_last + m_prev - m_new)
    kw = k * jnp.exp(a_c - m_new).astype(BF16)
    c_ref[...] = decay * state + lax.dot_general(
        kw, v_ext, (((0,), (0,)), ((), ())), preferred_element_type=F32)
    m_ref[...] = m_new
    return both[:, :e], inv


def _mlstm_kernel(q_ref, k_ref, v_ref, gc_ref, gr_ref, og_ref, xc_ref, z_ref, skip_ref,
                  o_ref, c_ref, m_ref):
    @pl.when(pl.program_id(1) == 0)
    def _():
        c_ref[...] = jnp.zeros_like(c_ref)
        m_ref[...] = jnp.zeros_like(m_ref)

    e = HEAD_DIM_C
    nh = N_HEADS_C
    gcol = gc_ref[0]
    grow = gr_ref[0]
    for h in range(nh):
        sl = slice(h * e, (h + 1) * e)
        num, inv = _mlstm_head(q_ref[0, :, sl], k_ref[0, :, sl], v_ref[0, :, sl],
                               gcol[:, h:h + 1], gcol[:, nh + h:nh + h + 1],
                               grow[h:h + 1, :], grow[nh + h:nh + h + 1, :],
                               c_ref.at[h], m_ref.at[h])
        msq = jnp.mean(num * num, axis=1, keepdims=True)
        f = inv * lax.rsqrt(inv * inv * msq + RMS_EPS)
        hn = (num * jnp.concatenate([f] * (e // LANES), axis=1) * og_ref[:, sl]).astype(BF16)
        y = hn + skip_ref[:, sl].astype(BF16) * xc_ref[0, :, sl]
        o_ref[0, :, sl] = y * _silu(z_ref[0, :, sl])


def _mlstm(q, k, v, gates, gates_t, out_gain, xc, z, skip):
    b, s, c = q.shape
    chunk = MLSTM_CHUNK
    e = HEAD_DIM_C
    rows = pl.BlockSpec((1, chunk, c), lambda bi, ci: (bi, ci, 0))
    return pl.pallas_call(
        _mlstm_kernel,
        grid=(b, s // chunk),
        in_specs=[rows, rows, rows,
                  pl.BlockSpec((1, chunk, LANES), lambda bi, ci: (bi, ci, 0)),
                  pl.BlockSpec((1, 2 * N_HEADS_C, chunk), lambda bi, ci: (bi, 0, ci)),
                  _resident((1, c)), rows, rows, _resident((1, c))],
        out_specs=rows,
        out_shape=jax.ShapeDtypeStruct((b, s, c), BF16),
        scratch_shapes=[pltpu.VMEM((N_HEADS_C, e, e + LANES), F32),
                        pltpu.VMEM((N_HEADS_C, 1, 1), F32)],
        compiler_params=_params("parallel", "arbitrary"),
        name="mlstm_chunkwise",
    )(q, k, v, gates, gates_t, out_gain.reshape(1, c), xc, z, skip.reshape(1, c))


def _out_odd_kernel(x_ref, y_ref, w_ref, o_ref):
    o_ref[...] = x_ref[...] + _dot(y_ref[...], w_ref[...])


def _out_odd(x, y, w_out, i):
    t = x.shape[0]
    tm = TOKEN_TILE
    row = lambda n: pl.BlockSpec((tm, n), lambda i: (i, 0))
    return pl.pallas_call(
        _out_odd_kernel,
        grid=(t // tm,),
        in_specs=[row(D_MODEL), row(INNER_C), _stacked((INNER_C, D_MODEL), i)],
        out_specs=row(D_MODEL),
        out_shape=jax.ShapeDtypeStruct((t, D_MODEL), F32),
        compiler_params=_params("parallel"),
        name="out_odd",
    )(x, y, w_out)


def kernel(x, rel_bias, norm_gains, ffn_w_gate, ffn_w_up, ffn_w_down, ev_w_in, ev_q_gain, ev_k_gain, ev_pool_w, ev_pool_scale, ev_w_out, od_w_in, od_conv_w, od_conv_b, od_wq, od_wk, od_wv, od_w_gates, od_b_gates, od_skip, od_out_gain, od_w_out):
    b, s, d = x.shape
    t = b * s
    x = x.reshape(t, d)
    bias_rows = _attn_bias_rows(rel_bias, s, ATTN_TILE)
    ffn_w = (ffn_w_gate.astype(BF16), ffn_w_up.astype(BF16), ffn_w_down.astype(BF16))
    ev_w_in, ev_w_out = ev_w_in.astype(BF16), ev_w_out.astype(BF16)
    od_w_in, od_w_out = od_w_in.astype(BF16), od_w_out.astype(BF16)
    for layer in range(DEPTH):
        g = norm_gains[layer]
        i = layer // 2
        x = _ffn(x, g[0], *ffn_w, layer, 0)
        if layer % 2 == 0:
            q, k, v, u = _proj_even(x, g[1], ev_w_in, i, ev_q_gain[i], ev_k_gain[i])
            seq = lambda a: a.reshape(b, s, a.shape[-1])
            attn = _attention(seq(q), seq(k), seq(v), bias_rows)
            pooled = _pool(seq(u), ev_pool_w[i].astype(BF16), ev_pool_scale[i])
            x = _out_even(x, attn.reshape(t, WIDTH_A), pooled.reshape(t, WIDTH_B), ev_w_out, i)
        else:
            xc, q, k, v, z, gates, gates_t = _odd_in(
                x.reshape(b, s, d), g[1], od_w_in, i, od_conv_w[i], od_conv_b[i], od_wq[i],
                od_wk[i], od_wv[i], od_w_gates[i], od_b_gates[i])
            y = _mlstm(q, k, v, gates, gates_t, od_out_gain[i], xc, z, od_skip[i])
            x = _out_odd(x, y.reshape(t, INNER_C), od_w_out, i)
        x = _ffn(x, g[2], *ffn_w, layer, 1)
    return x.reshape(b, s, d)
```

```python
import functools
import math

import numpy as np
import jax
import jax.numpy as jnp
from jax import lax
from jax.experimental import pallas as pl
from jax.experimental.pallas import tpu as pltpu

F32 = jnp.float32
BF16 = jnp.bfloat16

D_MODEL = 1024
DEPTH = 4
N_HEADS_A = 8
HEAD_DIM_A = 64
WIDTH_A = N_HEADS_A * HEAD_DIM_A
DILATED_PATTERNS = ((128, 1), (512, 4), (2048, 16))
N_POOL_GROUPS = 4
POOL_WINDOWS = (2, 4, 8, 16)
WIDTH_B = D_MODEL // 2
POOL_GROUP_DIM = WIDTH_B // N_POOL_GROUPS
IN_WIDTH_EVEN = 3 * WIDTH_A + WIDTH_B
REL_BUCKETS = 32
REL_MAX_DISTANCE = 2048
N_HEADS_C = 4
INNER_C = 2 * D_MODEL
HEAD_DIM_C = INNER_C // N_HEADS_C
CONV_WIDTH = 4
QKV_BLOCK = 4
D_FF = 256 * ((8 * D_MODEL // 3 + 255) // 256)
FFN_RESIDUAL = 0.5
RMS_EPS = 1e-6

LANES = 128
SUBLANES = 8
VMEM_LIMIT = 56 * 1024 * 1024
MASKED = -1e30

TOKEN_TILE = 512
ATTN_TILE = 256
MLSTM_CHUNK = 256
ODD_CHANNEL_BLOCK = 256


def _params(*sem):
    return pltpu.CompilerParams(dimension_semantics=sem, vmem_limit_bytes=VMEM_LIMIT)


def _resident(shape):
    nd = len(shape)
    return pl.BlockSpec(shape, lambda *_: (0,) * nd, pipeline_mode=pl.Buffered(1))


def _stacked(shape, *lead):
    nd = len(shape)
    return pl.BlockSpec((None,) * len(lead) + tuple(shape), lambda *_: tuple(lead) + (0,) * nd,
                        pipeline_mode=pl.Buffered(1))


def _rms(x, gain):
    return x * lax.rsqrt(jnp.mean(x * x, axis=-1, keepdims=True) + RMS_EPS) * gain


def _silu(x):
    return x / (1.0 + jnp.exp(-x))


def _dot(a, b):
    return jnp.dot(a, b, preferred_element_type=F32)


def _ffn_kernel(*refs, n_mix):
    x_ref, mix_refs, rest = refs[0], refs[1:1 + n_mix], refs[1 + n_mix:]
    if n_mix:
        wo_ref, rest = rest[0], rest[1:]
    g_ref, wg_ref, wu_ref, wd_ref, o_ref = rest
    x = x_ref[...]
    row = 0
    for m_ref in mix_refs:
        width = m_ref.shape[1]
        x = x + _dot(m_ref[...], wo_ref[row:row + width, :])
        row += width
    xn = _rms(x, g_ref[...]).astype(BF16)
    gate = _dot(xn, wg_ref[...])
    up = _dot(xn, wu_ref[...])
    h = (_silu(gate) * up).astype(BF16)
    o_ref[...] = x + FFN_RESIDUAL * _dot(h, wd_ref[...])


def _ffn(x, gain, w_gate, w_up, w_down, layer, half, mix=(), w_out=None, w_out_index=None):
    t = x.shape[0]
    tm = TOKEN_TILE
    row = lambda n: pl.BlockSpec((tm, n), lambda i: (i, 0))
    mix_specs = [row(m.shape[1]) for m in mix]
    mix_args = list(mix)
    if mix:
        mix_specs.append(_stacked(w_out.shape[1:], w_out_index))
        mix_args.append(w_out)
    return pl.pallas_call(
        functools.partial(_ffn_kernel, n_mix=len(mix)),
        grid=(t // tm,),
        in_specs=[row(D_MODEL), *mix_specs, _resident((1, D_MODEL)),
                  _stacked((D_MODEL, D_FF), layer, half), _stacked((D_MODEL, D_FF), layer, half),
                  _stacked((D_FF, D_MODEL), layer, half)],
        out_specs=row(D_MODEL),
        out_shape=jax.ShapeDtypeStruct((t, D_MODEL), F32),
        compiler_params=_params("parallel"),
        name="ffn",
    )(x, *mix_args, gain.reshape(1, D_MODEL), w_gate, w_up, w_down)


NORM_SLAB = 2 * LANES


def _head_sum_matrix():
    r = lax.broadcasted_iota(jnp.int32, (NORM_SLAB, NORM_SLAB), 0) // HEAD_DIM_A
    c = lax.broadcasted_iota(jnp.int32, (NORM_SLAB, NORM_SLAB), 1) // HEAD_DIM_A
    return jnp.where(r == c, 1.0, 0.0).astype(BF16)


def _head_norm(t, same_head, gain, scale):
    ss = _dot((t * t).astype(BF16), same_head)
    return t * lax.rsqrt(ss * (1.0 / HEAD_DIM_A) + RMS_EPS) * (gain * scale)


POOL_HALO = 16


def _pool_diffs(u, tail_ref, first_pos):
    ts = u.shape[0]
    pos = first_pos + lax.broadcasted_iota(jnp.int32, (ts, POOL_GROUP_DIM), 0)
    diffs = []
    for gi, window in enumerate(POOL_WINDOWS):
        gs = slice(gi * POOL_GROUP_DIM, (gi + 1) * POOL_GROUP_DIM)
        acc = jnp.concatenate([tail_ref[:, gs], u[:, gs]], axis=0)
        span = 1
        while span < window:
            acc = acc + pltpu.roll(acc, span, 0)
            span *= 2
        count = jnp.minimum(pos, window).astype(F32)
        diffs.append(acc[POOL_HALO:] / count - u[:, gs])
    tail_ref[...] = u[ts - POOL_HALO:, :]
    return jnp.concatenate(diffs, axis=1)


def _proj_even_kernel(x_ref, g_ref, w_ref, qg_ref, kg_ref, wp_ref, ps_ref,
                      q_ref, k_ref, v_ref, p_ref, tail_ref):
    si = pl.program_id(1)
    ts = x_ref.shape[1]

    @pl.when(si == 0)
    def _():
        tail_ref[...] = jnp.zeros_like(tail_ref)

    xn = _rms(x_ref[0], g_ref[...]).astype(BF16)
    proj = _dot(xn, w_ref[...])
    same_head = _head_sum_matrix()
    for j in range(WIDTH_A // NORM_SLAB):
        sl = slice(j * NORM_SLAB, (j + 1) * NORM_SLAB)
        q_ref[0, :, sl] = _head_norm(proj[:, sl], same_head, qg_ref[...],
                                     HEAD_DIM_A ** -0.5).astype(BF16)
        ks = slice(WIDTH_A + j * NORM_SLAB, WIDTH_A + (j + 1) * NORM_SLAB)
        k_ref[0, :, sl] = _head_norm(proj[:, ks], same_head, kg_ref[...], 1.0).astype(BF16)
    v_ref[0] = proj[:, 2 * WIDTH_A:3 * WIDTH_A].astype(BF16)
    diff = _pool_diffs(proj[:, 3 * WIDTH_A:], tail_ref, si * ts + 1).astype(BF16)
    for j in range(WIDTH_B // NORM_SLAB):
        sl = slice(j * NORM_SLAB, (j + 1) * NORM_SLAB)
        p_ref[0, :, sl] = (_dot(diff[:, sl], wp_ref[j]) * ps_ref[:, sl]).astype(BF16)


def _proj_even(x, gain, w_in, i, q_gain, k_gain, pool_w, pool_scale):
    b, s, _ = x.shape
    ts = TOKEN_TILE
    assert POOL_WINDOWS == (2, 4, 8, 16) and max(POOL_WINDOWS) <= POOL_HALO
    rows = lambda n: pl.BlockSpec((1, ts, n), lambda bi, si: (bi, si, 0))
    pair = lambda g: jnp.tile(g, NORM_SLAB // HEAD_DIM_A).reshape(1, NORM_SLAB)
    zero = jnp.zeros_like(pool_w[0])
    wp = jnp.stack([jnp.block([[pool_w[2 * j], zero], [zero, pool_w[2 * j + 1]]])
                    for j in range(N_POOL_GROUPS // 2)])
    act = jax.ShapeDtypeStruct((b, s, WIDTH_A), BF16)
    return pl.pallas_call(
        _proj_even_kernel,
        grid=(b, s // ts),
        in_specs=[rows(D_MODEL), _resident((1, D_MODEL)), _stacked((D_MODEL, IN_WIDTH_EVEN), i),
                  _resident((1, NORM_SLAB)), _resident((1, NORM_SLAB)),
                  _resident((N_POOL_GROUPS // 2, NORM_SLAB, NORM_SLAB)), _resident((1, WIDTH_B))],
        out_specs=[rows(WIDTH_A), rows(WIDTH_A), rows(WIDTH_A), rows(WIDTH_B)],
        out_shape=[act, act, act, jax.ShapeDtypeStruct((b, s, WIDTH_B), BF16)],
        scratch_shapes=[pltpu.VMEM((POOL_HALO, WIDTH_B), F32)],
        compiler_params=_params("parallel", "arbitrary"),
        name="proj_even",
    )(x, gain.reshape(1, D_MODEL), w_in, pair(q_gain), pair(k_gain), wp.astype(BF16),
      pool_scale.reshape(1, WIDTH_B))


def _t5_bucket(distance):
    max_exact = REL_BUCKETS // 2
    d = jnp.maximum(distance.astype(F32), 1.0)
    large = max_exact + (jnp.log(d / max_exact) / math.log(REL_MAX_DISTANCE / max_exact)
                         * (REL_BUCKETS - max_exact)).astype(jnp.int32)
    large = jnp.minimum(large, REL_BUCKETS - 1)
    return jnp.where(distance < max_exact, distance, large)


def _attn_bias_rows(rel_bias, seq, tile):
    dist = np.arange(seq)
    mult = np.zeros(seq, np.int64)
    for window, dilation in DILATED_PATTERNS:
        mult += (dist % dilation == 0) & (dist <= window)
    log_mult = np.where(mult > 0, np.log(np.maximum(mult, 1)), MASKED).astype(np.float32)
    per_dist = rel_bias[_t5_bucket(jnp.asarray(dist, jnp.int32))].astype(F32)
    per_dist = jnp.where(jnp.asarray(mult > 0)[:, None], per_dist + log_mult[:, None], MASKED)
    per_dist = per_dist.T
    base = seq - tile
    heads = per_dist.shape[0]
    masked = jnp.full((heads, tile), MASKED, F32)
    v = jnp.concatenate([per_dist[:, base:], masked, per_dist[:, :base]], axis=1)
    w = jnp.roll(v[:, ::-1], 1, axis=1)
    return w.reshape(heads // 2, 2, 1, seq + tile)


def _attn_kernel(q_ref, k_ref, v_ref, w_ref, o_ref, tab_ref, vext_ref, *, tile):
    seq = q_ref.shape[1]
    nq = seq // tile

    vext_ref[:, :LANES] = v_ref[0]
    vext_ref[:, LANES:] = jnp.ones((seq, LANES), BF16)

    @pl.when(pl.program_id(1) == 0)
    def _():
        for hh in range(2):
            rows = jnp.broadcast_to(w_ref[0, hh], (tile, seq + tile))
            tab_ref[hh] = pltpu.roll(rows, 0, 1, stride=1, stride_axis=0)[:, :seq]

    first = lax.broadcasted_iota(jnp.int32, (tile, LANES), 1) < HEAD_DIM_A
    for qi in range(nq):
        n = (qi + 1) * tile
        off = (nq - 1 - qi) * tile
        q = q_ref[0, qi * tile:(qi + 1) * tile, :]
        zero = jnp.zeros_like(q)
        q2 = jnp.concatenate([jnp.where(first, q, zero), jnp.where(first, zero, q)], axis=0)
        s = lax.dot_general(q2, k_ref[0, :n, :], (((1,), (1,)), ((), ())),
                            preferred_element_type=F32)
        s = s + jnp.concatenate([tab_ref[0, :, off:off + n], tab_ref[1, :, off:off + n]], axis=0)
        p = jnp.exp(s - jnp.max(s, axis=-1, keepdims=True))
        o = _dot(p.astype(BF16), vext_ref[:n, :])
        o = o[:, :LANES] / o[:, LANES:]
        o_ref[0, qi * tile:(qi + 1) * tile, :] = jnp.where(first, o[:tile], o[tile:]).astype(BF16)


def _attention(q, k, v, bias_rows):
    b, s, _ = q.shape
    tile = ATTN_TILE
    seq_spec = pl.BlockSpec((1, s, LANES), lambda h, bi: (bi, 0, h))
    return pl.pallas_call(
        functools.partial(_attn_kernel, tile=tile),
        grid=(WIDTH_A // LANES, b),
        in_specs=[seq_spec, seq_spec, seq_spec,
                  pl.BlockSpec((1, 2, 1, s + tile), lambda h, bi: (h, 0, 0, 0))],
        out_specs=seq_spec,
        out_shape=jax.ShapeDtypeStruct((b, s, WIDTH_A), BF16),
        scratch_shapes=[pltpu.VMEM((2, tile, s), F32), pltpu.VMEM((s, 2 * LANES), BF16)],
        compiler_params=_params("arbitrary", "arbitrary"),
        name="dilated_attention",
    )(q, k, v, bias_rows)


def _shift_rows_halo(a, tail, back, row8):
    halo = tail.shape[0]
    rolled = pltpu.roll(a, back, 0)
    head = jnp.where(row8 < back, pltpu.roll(tail, back, 0), rolled[:halo])
    return jnp.concatenate([head, rolled[halo:]], axis=0)


def _odd_in_kernel(x_ref, g_ref, w_ref, cw_ref, cb_ref, wmix_ref, bg_ref,
                   xc_ref, q_ref, k_ref, v_ref, z_ref, gates_ref, gates_t_ref, tail_ref):
    ts = x_ref.shape[1]
    halo = tail_ref.shape[0]
    cbw = ODD_CHANNEL_BLOCK
    row8 = lax.broadcasted_iota(jnp.int32, (halo, cbw), 0)

    @pl.when(pl.program_id(1) == 0)
    def _():
        tail_ref[...] = jnp.zeros_like(tail_ref)

    xn = _rms(x_ref[0], g_ref[...]).astype(BF16)
    gates = jnp.broadcast_to(bg_ref[...], (ts, LANES))
    n_blocks = INNER_C // cbw
    xm_next = _dot(xn, w_ref[:, :cbw])
    for cb in range(n_blocks):
        cs = slice(cb * cbw, (cb + 1) * cbw)
        xm = xm_next
        if cb + 1 < n_blocks:
            xm_next = _dot(xn, w_ref[:, (cb + 1) * cbw:(cb + 2) * cbw])
        z_ref[0, :, cs] = _dot(xn, w_ref[:, INNER_C + cb * cbw:INNER_C + (cb + 1) * cbw]
                               ).astype(BF16)
        tail = tail_ref[:, cs]
        tail_ref[:, cs] = xm[ts - halo:, :]
        w0, w1, w2, w3 = (cw_ref[tap:tap + 1, cs] for tap in range(CONV_WIDTH))
        xm_1 = _shift_rows_halo(xm, tail, 1, row8)
        far = xm * w1 + xm_1 * w0
        far_tail = tail * w1 + pltpu.roll(tail, 1, 0) * w0
        conv = xm * w3 + xm_1 * w2 + _shift_rows_halo(far, far_tail, 2, row8)
        xc16 = _silu(conv + cb_ref[:, cs]).astype(BF16)
        xm16 = xm.astype(BF16)
        xc_ref[0, :, cs] = xc16
        for gl in range(cbw // LANES):
            ls = slice(gl * LANES, (gl + 1) * LANES)
            gi = cb * (cbw // LANES) + gl
            sl = slice(gi * LANES, (gi + 1) * LANES)
            out = _dot(jnp.concatenate([xc16[:, ls], xm16[:, ls]], axis=1), wmix_ref[gi])
            q_ref[0, :, sl] = out[:, :LANES].astype(BF16)
            k_ref[0, :, sl] = out[:, LANES:2 * LANES].astype(BF16)
            v_ref[0, :, sl] = out[:, 2 * LANES:3 * LANES].astype(BF16)
            gates = gates + out[:, 3 * LANES:]
    gates_ref[0] = gates
    gates_t_ref[0] = gates.T


def _block_diag_tiles(w):
    per_tile = LANES // QKV_BLOCK
    w = w.reshape(-1, per_tile, QKV_BLOCK, QKV_BLOCK)
    eye = jnp.eye(per_tile, dtype=w.dtype)
    dense = jnp.einsum("tgio,gh->tgiho", w, eye)
    return dense.reshape(-1, LANES, LANES)


def _odd_in(x, gain, w_in, i, conv_w, conv_b, wq, wk, wv, w_gates, b_gates):
    b, s, _ = x.shape
    c = INNER_C
    ts = TOKEN_TILE
    n_gates = 2 * N_HEADS_C
    wq_t, wk_t, wv_t = _block_diag_tiles(wq), _block_diag_tiles(wk), _block_diag_tiles(wv)
    wg = w_gates.reshape(3, c // LANES, LANES, n_gates)
    compose = lambda w_t, g: jnp.einsum("tij,tjn->tin", w_t, g, precision=lax.Precision.HIGHEST)
    pad_gates = lambda g: jnp.pad(g, ((0, 0), (0, 0), (0, LANES - n_gates)))
    zeros = jnp.zeros_like(wq_t)
    wmix = jnp.concatenate([
        jnp.concatenate([wq_t, wk_t * HEAD_DIM_C ** -0.5, zeros,
                         pad_gates(compose(wq_t, wg[0]) + compose(wk_t, wg[1]))], axis=2),
        jnp.concatenate([zeros, zeros, wv_t, pad_gates(compose(wv_t, wg[2]))], axis=2)], axis=1)
    bg = jnp.pad(b_gates.reshape(1, n_gates), ((0, 0), (0, LANES - n_gates)))
    rows = lambda n: pl.BlockSpec((1, ts, n), lambda bi, si: (bi, si, 0))
    act = jax.ShapeDtypeStruct((b, s, c), BF16)
    return pl.pallas_call(
        _odd_in_kernel,
        grid=(b, s // ts),
        in_specs=[rows(D_MODEL), _resident((1, D_MODEL)), _stacked((D_MODEL, 2 * c), i),
                  _resident((CONV_WIDTH, c)), _resident((1, c)),
                  _resident((c // LANES, 2 * LANES, 4 * LANES)), _resident((1, LANES))],
        out_specs=[rows(c), rows(c), rows(c), rows(c), rows(c), rows(LANES),
                   pl.BlockSpec((1, LANES, ts), lambda bi, si: (bi, 0, si))],
        out_shape=[act, act, act, act, act, jax.ShapeDtypeStruct((b, s, LANES), F32),
                   jax.ShapeDtypeStruct((b, LANES, s), F32)],
        scratch_shapes=[pltpu.VMEM((SUBLANES, c), F32)],
        compiler_params=_params("parallel", "arbitrary"),
        name="odd_in",
    )(x, gain.reshape(1, D_MODEL), w_in, conv_w, conv_b.reshape(1, c), wmix.astype(BF16), bg)


def _log_sigmoid(x):
    return jnp.minimum(x, 0.0) - jnp.log(1.0 + jnp.exp(-jnp.abs(x)))


def _mlstm_head(q, k, v, ig_c, fg_c, ig_r, fg_r, c_ref, m_ref):
    chunk, e = q.shape
    v_ext = jnp.concatenate([v, jnp.ones((chunk, LANES), BF16)], axis=1)
    lf_c = _log_sigmoid(fg_c)
    lf_r = _log_sigmoid(fg_r)
    t_idx = lax.broadcasted_iota(jnp.int32, (chunk, chunk), 0)
    s_idx = lax.broadcasted_iota(jnp.int32, (chunk, chunk), 1)
    causal = s_idx <= t_idx
    bcum_c = jnp.sum(jnp.where(causal, lf_r, 0.0), axis=1, keepdims=True)
    bcum_r = jnp.sum(jnp.where(t_idx <= s_idx, lf_c, 0.0), axis=0, keepdims=True)
    b_last = jnp.sum(lf_r, axis=1, keepdims=True)
    d = jnp.where(causal, bcum_c - bcum_r + ig_r, MASKED)
    m_prev = m_ref[...]
    inter = bcum_c + m_prev
    m_t = jnp.maximum(inter, jnp.max(d, axis=1, keepdims=True))
    w_inter = jnp.exp(inter - m_t)
    s_qk = lax.dot_general(q, k, (((1,), (1,)), ((), ())), preferred_element_type=F32)
    s_qk = (s_qk * jnp.exp(d - m_t)).astype(BF16)
    state = c_ref[...]
    both = w_inter * _dot(q, state.astype(BF16)) + _dot(s_qk, v_ext)
    inv = 1.0 / jnp.maximum(jnp.abs(both[:, e:]), jnp.exp(-m_t))

    a_r = b_last - bcum_r + ig_r
    a_c = b_last - bcum_c + ig_c
    m_new = jnp.maximum(b_last + m_prev, jnp.max(a_r, axis=1, keepdims=True))
    decay = jnp.exp(b_last + m_prev - m_new)
    kw = k * jnp.exp(a_c - m_new).astype(BF16)
    c_ref[...] = decay * state + lax.dot_general(
        kw, v_ext, (((0,), (0,)), ((), ())), preferred_element_type=F32)
    m_ref[...] = m_new
    return both[:, :e], inv


def _mlstm_kernel(q_ref, k_ref, v_ref, gc_ref, gr_ref, og_ref, xc_ref, z_ref, skip_ref,
                  o_ref, c_ref, m_ref):
    @pl.when(pl.program_id(1) == 0)
    def _():
        c_ref[...] = jnp.zeros_like(c_ref)
        m_ref[...] = jnp.zeros_like(m_ref)

    e = HEAD_DIM_C
    nh = N_HEADS_C
    gcol = gc_ref[0]
    grow = gr_ref[0]
    for h in range(nh):
        sl = slice(h * e, (h + 1) * e)
        num, inv = _mlstm_head(q_ref[0, :, sl], k_ref[0, :, sl], v_ref[0, :, sl],
                               gcol[:, h:h + 1], gcol[:, nh + h:nh + h + 1],
                               grow[h:h + 1, :], grow[nh + h:nh + h + 1, :],
                               c_ref.at[h], m_ref.at[h])
        msq = jnp.mean(num * num, axis=1, keepdims=True)
        f = inv * lax.rsqrt(inv * inv * msq + RMS_EPS)
        hn = (num * jnp.concatenate([f] * (e // LANES), axis=1) * og_ref[:, sl]).astype(BF16)
        y = hn + skip_ref[:, sl].astype(BF16) * xc_ref[0, :, sl]
        o_ref[0, :, sl] = y * _silu(z_ref[0, :, sl])


def _mlstm(q, k, v, gates, gates_t, out_gain, xc, z, skip):
    b, s, c = q.shape
    chunk = MLSTM_CHUNK
    e = HEAD_DIM_C
    rows = pl.BlockSpec((1, chunk, c), lambda bi, ci: (bi, ci, 0))
    return pl.pallas_call(
        _mlstm_kernel,
        grid=(b, s // chunk),
        in_specs=[rows, rows, rows,
                  pl.BlockSpec((1, chunk, LANES), lambda bi, ci: (bi, ci, 0)),
                  pl.BlockSpec((1, 2 * N_HEADS_C, chunk), lambda bi, ci: (bi, 0, ci)),
                  _resident((1, c)), rows, rows, _resident((1, c))],
        out_specs=rows,
        out_shape=jax.ShapeDtypeStruct((b, s, c), BF16),
        scratch_shapes=[pltpu.VMEM((N_HEADS_C, e, e + LANES), F32),
                        pltpu.VMEM((N_HEADS_C, 1, 1), F32)],
        compiler_params=_params("parallel", "arbitrary"),
        name="mlstm_chunkwise",
    )(q, k, v, gates, gates_t, out_gain.reshape(1, c), xc, z, skip.reshape(1, c))


def kernel(x, rel_bias, norm_gains, ffn_w_gate, ffn_w_up, ffn_w_down, ev_w_in, ev_q_gain, ev_k_gain, ev_pool_w, ev_pool_scale, ev_w_out, od_w_in, od_conv_w, od_conv_b, od_wq, od_wk, od_wv, od_w_gates, od_b_gates, od_skip, od_out_gain, od_w_out):
    b, s, d = x.shape
    t = b * s
    x = x.reshape(t, d)
    bias_rows = _attn_bias_rows(rel_bias, s, ATTN_TILE)
    ffn_w = (ffn_w_gate.astype(BF16), ffn_w_up.astype(BF16), ffn_w_down.astype(BF16))
    ev_w_in, ev_w_out = ev_w_in.astype(BF16), ev_w_out.astype(BF16)
    od_w_in, od_w_out = od_w_in.astype(BF16), od_w_out.astype(BF16)
    for layer in range(DEPTH):
        g = norm_gains[layer]
        i = layer // 2
        x = _ffn(x, g[0], *ffn_w, layer, 0)
        if layer % 2 == 0:
            q, k, v, pooled = _proj_even(x.reshape(b, s, d), g[1], ev_w_in, i, ev_q_gain[i],
                                         ev_k_gain[i], ev_pool_w[i], ev_pool_scale[i])
            attn = _attention(q, k, v, bias_rows)
            mix, w_out = (attn.reshape(t, WIDTH_A), pooled.reshape(t, WIDTH_B)), ev_w_out
        else:
            xc, q, k, v, z, gates, gates_t = _odd_in(
                x.reshape(b, s, d), g[1], od_w_in, i, od_conv_w[i], od_conv_b[i], od_wq[i],
                od_wk[i], od_wv[i], od_w_gates[i], od_b_gates[i])
            y = _mlstm(q, k, v, gates, gates_t, od_out_gain[i], xc, z, od_skip[i])
            mix, w_out = (y.reshape(t, INNER_C),), od_w_out
        x = _ffn(x, g[2], *ffn_w, layer, 1, mix, w_out, i)
    return x.reshape(b, s, d)
```

```python
import functools
import math

import numpy as np
import jax
import jax.numpy as jnp
from jax import lax
from jax.experimental import pallas as pl
from jax.experimental.pallas import tpu as pltpu

F32 = jnp.float32
BF16 = jnp.bfloat16

D_MODEL = 1024
DEPTH = 4
N_HEADS_A = 8
HEAD_DIM_A = 64
WIDTH_A = N_HEADS_A * HEAD_DIM_A
DILATED_PATTERNS = ((128, 1), (512, 4), (2048, 16))
N_POOL_GROUPS = 4
POOL_WINDOWS = (2, 4, 8, 16)
WIDTH_B = D_MODEL // 2
POOL_GROUP_DIM = WIDTH_B // N_POOL_GROUPS
IN_WIDTH_EVEN = 3 * WIDTH_A + WIDTH_B
REL_BUCKETS = 32
REL_MAX_DISTANCE = 2048
N_HEADS_C = 4
INNER_C = 2 * D_MODEL
HEAD_DIM_C = INNER_C // N_HEADS_C
CONV_WIDTH = 4
QKV_BLOCK = 4
D_FF = 256 * ((8 * D_MODEL // 3 + 255) // 256)
FFN_RESIDUAL = 0.5
RMS_EPS = 1e-6

LANES = 128
SUBLANES = 8
VMEM_LIMIT = 56 * 1024 * 1024
MASKED = -1e30

TOKEN_TILE = 512
EVEN_IN_TILE = 1024
ATTN_TILE = 256
ATTN_STEP_BATCH = 2
MLSTM_CHUNK = 256
MLSTM_STEP_CHUNKS = 2
ODD_CHANNEL_BLOCK = 256


def _params(*sem):
    return pltpu.CompilerParams(dimension_semantics=sem, vmem_limit_bytes=VMEM_LIMIT)


def _resident(shape):
    nd = len(shape)
    return pl.BlockSpec(shape, lambda *_: (0,) * nd, pipeline_mode=pl.Buffered(1))


def _stacked(shape, *lead):
    nd = len(shape)
    return pl.BlockSpec((None,) * len(lead) + tuple(shape), lambda *_: tuple(lead) + (0,) * nd,
                        pipeline_mode=pl.Buffered(1))


def _rms(x, gain):
    return x * lax.rsqrt(jnp.mean(x * x, axis=-1, keepdims=True) + RMS_EPS) * gain


def _silu(x):
    return x / (1.0 + jnp.exp(-x))


def _dot(a, b):
    return jnp.dot(a, b, preferred_element_type=F32)


def _ffn_kernel(*refs, n_mix):
    x_ref, mix_refs, rest = refs[0], refs[1:1 + n_mix], refs[1 + n_mix:]
    if n_mix:
        wo_ref, rest = rest[0], rest[1:]
    g_ref, wg_ref, wu_ref, wd_ref, o_ref = rest
    x = x_ref[...]
    row = 0
    for m_ref in mix_refs:
        width = m_ref.shape[1]
        x = x + _dot(m_ref[...], wo_ref[row:row + width, :])
        row += width
    xn = _rms(x, g_ref[...]).astype(BF16)
    gate = _dot(xn, wg_ref[...])
    up = _dot(xn, wu_ref[...])
    h = (_silu(gate) * up).astype(BF16)
    o_ref[...] = x + FFN_RESIDUAL * _dot(h, wd_ref[...])


def _ffn(x, gain, w_gate, w_up, w_down, layer, half, mix=(), w_out=None, w_out_index=None):
    t = x.shape[0]
    tm = TOKEN_TILE
    row = lambda n: pl.BlockSpec((tm, n), lambda i: (i, 0))
    mix_specs = [row(m.shape[1]) for m in mix]
    mix_args = list(mix)
    if mix:
        mix_specs.append(_stacked(w_out.shape[1:], w_out_index))
        mix_args.append(w_out)
    return pl.pallas_call(
        functools.partial(_ffn_kernel, n_mix=len(mix)),
        grid=(t // tm,),
        in_specs=[row(D_MODEL), *mix_specs, _resident((1, D_MODEL)),
                  _stacked((D_MODEL, D_FF), layer, half), _stacked((D_MODEL, D_FF), layer, half),
                  _stacked((D_FF, D_MODEL), layer, half)],
        out_specs=row(D_MODEL),
        out_shape=jax.ShapeDtypeStruct((t, D_MODEL), F32),
        compiler_params=_params("parallel"),
        name="ffn",
    )(x, *mix_args, gain.reshape(1, D_MODEL), w_gate, w_up, w_down)


NORM_SLAB = 2 * LANES


def _head_sum_matrix():
    r = lax.broadcasted_iota(jnp.int32, (NORM_SLAB, NORM_SLAB), 0) // HEAD_DIM_A
    c = lax.broadcasted_iota(jnp.int32, (NORM_SLAB, NORM_SLAB), 1) // HEAD_DIM_A
    return jnp.where(r == c, 1.0, 0.0).astype(BF16)


def _head_norm(t, same_head, gain, scale):
    ss = _dot((t * t).astype(BF16), same_head)
    return t * lax.rsqrt(ss * (1.0 / HEAD_DIM_A) + RMS_EPS) * (gain * scale)


POOL_HALO = 16


def _pool_diffs(u, tail_ref, first_pos):
    ts = u.shape[0]
    pos = first_pos + lax.broadcasted_iota(jnp.int32, (ts, POOL_GROUP_DIM), 0)
    diffs = []
    for gi, window in enumerate(POOL_WINDOWS):
        gs = slice(gi * POOL_GROUP_DIM, (gi + 1) * POOL_GROUP_DIM)
        acc = jnp.concatenate([tail_ref[:, gs], u[:, gs]], axis=0)
        span = 1
        while span < window:
            acc = acc + pltpu.roll(acc, span, 0)
            span *= 2
        count = jnp.minimum(pos, window).astype(F32)
        diffs.append(acc[POOL_HALO:] / count - u[:, gs])
    tail_ref[...] = u[ts - POOL_HALO:, :]
    return jnp.concatenate(diffs, axis=1)


def _proj_even_kernel(x_ref, g_ref, w_ref, qg_ref, kg_ref, wp_ref, ps_ref,
                      q_ref, k_ref, v_ref, p_ref, tail_ref):
    si = pl.program_id(1)
    ts = x_ref.shape[1]

    @pl.when(si == 0)
    def _():
        tail_ref[...] = jnp.zeros_like(tail_ref)

    xn = _rms(x_ref[0], g_ref[...]).astype(BF16)
    proj = _dot(xn, w_ref[...])
    same_head = _head_sum_matrix()
    for j in range(WIDTH_A // NORM_SLAB):
        sl = slice(j * NORM_SLAB, (j + 1) * NORM_SLAB)
        q_ref[0, :, sl] = _head_norm(proj[:, sl], same_head, qg_ref[...],
                                     HEAD_DIM_A ** -0.5).astype(BF16)
        ks = slice(WIDTH_A + j * NORM_SLAB, WIDTH_A + (j + 1) * NORM_SLAB)
        k_ref[0, :, sl] = _head_norm(proj[:, ks], same_head, kg_ref[...], 1.0).astype(BF16)
    v_ref[0] = proj[:, 2 * WIDTH_A:3 * WIDTH_A].astype(BF16)
    diff = _pool_diffs(proj[:, 3 * WIDTH_A:], tail_ref, si * ts + 1).astype(BF16)
    for j in range(WIDTH_B // NORM_SLAB):
        sl = slice(j * NORM_SLAB, (j + 1) * NORM_SLAB)
        p_ref[0, :, sl] = (_dot(diff[:, sl], wp_ref[j]) * ps_ref[:, sl]).astype(BF16)


def _proj_even(x, gain, w_in, i, q_gain, k_gain, pool_w, pool_scale):
    b, s, _ = x.shape
    ts = EVEN_IN_TILE
    assert POOL_WINDOWS == (2, 4, 8, 16) and max(POOL_WINDOWS) <= POOL_HALO
    rows = lambda n: pl.BlockSpec((1, ts, n), lambda bi, si: (bi, si, 0))
    pair = lambda g: jnp.tile(g, NORM_SLAB // HEAD_DIM_A).reshape(1, NORM_SLAB)
    zero = jnp.zeros_like(pool_w[0])
    wp = jnp.stack([jnp.block([[pool_w[2 * j], zero], [zero, pool_w[2 * j + 1]]])
                    for j in range(N_POOL_GROUPS // 2)])
    act = jax.ShapeDtypeStruct((b, s, WIDTH_A), BF16)
    return pl.pallas_call(
        _proj_even_kernel,
        grid=(b, s // ts),
        in_specs=[rows(D_MODEL), _resident((1, D_MODEL)), _stacked((D_MODEL, IN_WIDTH_EVEN), i),
                  _resident((1, NORM_SLAB)), _resident((1, NORM_SLAB)),
                  _resident((N_POOL_GROUPS // 2, NORM_SLAB, NORM_SLAB)), _resident((1, WIDTH_B))],
        out_specs=[rows(WIDTH_A), rows(WIDTH_A), rows(WIDTH_A), rows(WIDTH_B)],
        out_shape=[act, act, act, jax.ShapeDtypeStruct((b, s, WIDTH_B), BF16)],
        scratch_shapes=[pltpu.VMEM((POOL_HALO, WIDTH_B), F32)],
        compiler_params=_params("parallel", "arbitrary"),
        name="proj_even",
    )(x, gain.reshape(1, D_MODEL), w_in, pair(q_gain), pair(k_gain), wp.astype(BF16),
      pool_scale.reshape(1, WIDTH_B))


def _t5_bucket(distance):
    max_exact = REL_BUCKETS // 2
    d = jnp.maximum(distance.astype(F32), 1.0)
    large = max_exact + (jnp.log(d / max_exact) / math.log(REL_MAX_DISTANCE / max_exact)
                         * (REL_BUCKETS - max_exact)).astype(jnp.int32)
    large = jnp.minimum(large, REL_BUCKETS - 1)
    return jnp.where(distance < max_exact, distance, large)


def _attn_bias_rows(rel_bias, seq, tile):
    dist = np.arange(seq)
    mult = np.zeros(seq, np.int64)
    for window, dilation in DILATED_PATTERNS:
        mult += (dist % dilation == 0) & (dist <= window)
    log_mult = np.where(mult > 0, np.log(np.maximum(mult, 1)), MASKED).astype(np.float32)
    per_dist = rel_bias[_t5_bucket(jnp.asarray(dist, jnp.int32))].astype(F32)
    per_dist = jnp.where(jnp.asarray(mult > 0)[:, None], per_dist + log_mult[:, None], MASKED)
    per_dist = per_dist.T
    base = seq - tile
    heads = per_dist.shape[0]
    masked = jnp.full((heads, tile), MASKED, F32)
    v = jnp.concatenate([per_dist[:, base:], masked, per_dist[:, :base]], axis=1)
    w = jnp.roll(v[:, ::-1], 1, axis=1)
    return w.reshape(heads // 2, 2, 1, seq + tile)


def _attn_kernel(q_ref, k_ref, v_ref, w_ref, o_ref, tab_ref, vext_ref, *, tile):
    seq = q_ref.shape[1]
    nq = seq // tile

    nb = q_ref.shape[0]
    for bb in range(nb):
        vext_ref[bb, :, :LANES] = v_ref[bb]
        vext_ref[bb, :, LANES:] = jnp.ones((seq, LANES), BF16)

    @pl.when(pl.program_id(1) == 0)
    def _():
        for hh in range(2):
            rows = jnp.broadcast_to(w_ref[0, hh], (tile, seq + tile))
            tab_ref[hh] = pltpu.roll(rows, 0, 1, stride=1, stride_axis=0)[:, :seq]

    first = lax.broadcasted_iota(jnp.int32, (tile, LANES), 1) < HEAD_DIM_A
    for qi, bb in ((qi, bb) for qi in range(nq) for bb in range(nb)):
        n = (qi + 1) * tile
        off = (nq - 1 - qi) * tile
        q = q_ref[bb, qi * tile:(qi + 1) * tile, :]
        zero = jnp.zeros_like(q)
        q2 = jnp.concatenate([jnp.where(first, q, zero), jnp.where(first, zero, q)], axis=0)
        s = lax.dot_general(q2, k_ref[bb, :n, :], (((1,), (1,)), ((), ())),
                            preferred_element_type=F32)
        s = s + jnp.concatenate([tab_ref[0, :, off:off + n], tab_ref[1, :, off:off + n]], axis=0)
        p = jnp.exp(s - jnp.max(s, axis=-1, keepdims=True))
        o = _dot(p.astype(BF16), vext_ref[bb, :n, :])
        o = o[:, :LANES] / o[:, LANES:]
        o_ref[bb, qi * tile:(qi + 1) * tile, :] = jnp.where(first, o[:tile], o[tile:]).astype(BF16)


def _attention(q, k, v, bias_rows):
    b, s, _ = q.shape
    tile = ATTN_TILE
    nb = ATTN_STEP_BATCH
    seq_spec = pl.BlockSpec((nb, s, LANES), lambda h, bi: (bi, 0, h))
    return pl.pallas_call(
        functools.partial(_attn_kernel, tile=tile),
        grid=(WIDTH_A // LANES, b // nb),
        in_specs=[seq_spec, seq_spec, seq_spec,
                  pl.BlockSpec((1, 2, 1, s + tile), lambda h, bi: (h, 0, 0, 0))],
        out_specs=seq_spec,
        out_shape=jax.ShapeDtypeStruct((b, s, WIDTH_A), BF16),
        scratch_shapes=[pltpu.VMEM((2, tile, s), F32), pltpu.VMEM((nb, s, 2 * LANES), BF16)],
        compiler_params=_params("arbitrary", "arbitrary"),
        name="dilated_attention",
    )(q, k, v, bias_rows)


def _shift_rows_halo(a, tail, back, row8):
    halo = tail.shape[0]
    rolled = pltpu.roll(a, back, 0)
    head = jnp.where(row8 < back, pltpu.roll(tail, back, 0), rolled[:halo])
    return jnp.concatenate([head, rolled[halo:]], axis=0)


def _odd_in_kernel(x_ref, g_ref, w_ref, cw_ref, cb_ref, wmix_ref, bg_ref,
                   xc_ref, q_ref, k_ref, v_ref, z_ref, gates_ref, gates_t_ref, tail_ref):
    ts = x_ref.shape[1]
    halo = tail_ref.shape[0]
    cbw = ODD_CHANNEL_BLOCK
    row8 = lax.broadcasted_iota(jnp.int32, (halo, cbw), 0)

    @pl.when(pl.program_id(1) == 0)
    def _():
        tail_ref[...] = jnp.zeros_like(tail_ref)

    xn = _rms(x_ref[0], g_ref[...]).astype(BF16)
    gates = jnp.broadcast_to(bg_ref[...], (ts, LANES))
    n_blocks = INNER_C // cbw
    xm_next = _dot(xn, w_ref[:, :cbw])
    for cb in range(n_blocks):
        cs = slice(cb * cbw, (cb + 1) * cbw)
        xm = xm_next
        if cb + 1 < n_blocks:
            xm_next = _dot(xn, w_ref[:, (cb + 1) * cbw:(cb + 2) * cbw])
        z_ref[0, :, cs] = _dot(xn, w_ref[:, INNER_C + cb * cbw:INNER_C + (cb + 1) * cbw]
                               ).astype(BF16)
        tail = tail_ref[:, cs]
        tail_ref[:, cs] = xm[ts - halo:, :]
        w0, w1, w2, w3 = (cw_ref[tap:tap + 1, cs] for tap in range(CONV_WIDTH))
        xm_1 = _shift_rows_halo(xm, tail, 1, row8)
        far = xm * w1 + xm_1 * w0
        far_tail = tail * w1 + pltpu.roll(tail, 1, 0) * w0
        conv = xm * w3 + xm_1 * w2 + _shift_rows_halo(far, far_tail, 2, row8)
        xc16 = _silu(conv + cb_ref[:, cs]).astype(BF16)
        xm16 = xm.astype(BF16)
        xc_ref[0, :, cs] = xc16
        for gl in range(cbw // LANES):
            ls = slice(gl * LANES, (gl + 1) * LANES)
            gi = cb * (cbw // LANES) + gl
            sl = slice(gi * LANES, (gi + 1) * LANES)
            out = _dot(jnp.concatenate([xc16[:, ls], xm16[:, ls]], axis=1), wmix_ref[gi])
            q_ref[0, :, sl] = out[:, :LANES].astype(BF16)
            k_ref[0, :, sl] = out[:, LANES:2 * LANES].astype(BF16)
            v_ref[0, :, sl] = out[:, 2 * LANES:3 * LANES].astype(BF16)
            gates = gates + out[:, 3 * LANES:]
    gates_ref[0] = gates
    gates_t_ref[0] = gates.T


def _block_diag_tiles(w):
    per_tile = LANES // QKV_BLOCK
    w = w.reshape(-1, per_tile, QKV_BLOCK, QKV_BLOCK)
    eye = jnp.eye(per_tile, dtype=w.dtype)
    dense = jnp.einsum("tgio,gh->tgiho", w, eye)
    return dense.reshape(-1, LANES, LANES)


def _odd_in(x, gain, w_in, i, conv_w, conv_b, wq, wk, wv, w_gates, b_gates):
    b, s, _ = x.shape
    c = INNER_C
    ts = TOKEN_TILE
    n_gates = 2 * N_HEADS_C
    wq_t, wk_t, wv_t = _block_diag_tiles(wq), _block_diag_tiles(wk), _block_diag_tiles(wv)
    wg = w_gates.reshape(3, c // LANES, LANES, n_gates)
    compose = lambda w_t, g: jnp.einsum("tij,tjn->tin", w_t, g, precision=lax.Precision.HIGHEST)
    pad_gates = lambda g: jnp.pad(g, ((0, 0), (0, 0), (0, LANES - n_gates)))
    zeros = jnp.zeros_like(wq_t)
    wmix = jnp.concatenate([
        jnp.concatenate([wq_t, wk_t * HEAD_DIM_C ** -0.5, zeros,
                         pad_gates(compose(wq_t, wg[0]) + compose(wk_t, wg[1]))], axis=2),
        jnp.concatenate([zeros, zeros, wv_t, pad_gates(compose(wv_t, wg[2]))], axis=2)], axis=1)
    bg = jnp.pad(b_gates.reshape(1, n_gates), ((0, 0), (0, LANES - n_gates)))
    rows = lambda n: pl.BlockSpec((1, ts, n), lambda bi, si: (bi, si, 0))
    act = jax.ShapeDtypeStruct((b, s, c), BF16)
    return pl.pallas_call(
        _odd_in_kernel,
        grid=(b, s // ts),
        in_specs=[rows(D_MODEL), _resident((1, D_MODEL)), _stacked((D_MODEL, 2 * c), i),
                  _resident((CONV_WIDTH, c)), _resident((1, c)),
                  _resident((c // LANES, 2 * LANES, 4 * LANES)), _resident((1, LANES))],
        out_specs=[rows(c), rows(c), rows(c), rows(c), rows(c), rows(LANES),
                   pl.BlockSpec((1, LANES, ts), lambda bi, si: (bi, 0, si))],
        out_shape=[act, act, act, act, act, jax.ShapeDtypeStruct((b, s, LANES), F32),
                   jax.ShapeDtypeStruct((b, LANES, s), F32)],
        scratch_shapes=[pltpu.VMEM((SUBLANES, c), F32)],
        compiler_params=_params("parallel", "arbitrary"),
        name="odd_in",
    )(x, gain.reshape(1, D_MODEL), w_in, conv_w, conv_b.reshape(1, c), wmix.astype(BF16), bg)


def _log_sigmoid(x):
    return jnp.minimum(x, 0.0) - jnp.log(1.0 + jnp.exp(-jnp.abs(x)))


def _mlstm_head(q, k, v, ig_c, fg_c, ig_r, fg_r, c_ref, m_ref):
    chunk, e = q.shape
    v_ext = jnp.concatenate([v, jnp.ones((chunk, LANES), BF16)], axis=1)
    lf_c = _log_sigmoid(fg_c)
    lf_r = _log_sigmoid(fg_r)
    t_idx = lax.broadcasted_iota(jnp.int32, (chunk, chunk), 0)
    s_idx = lax.broadcasted_iota(jnp.int32, (chunk, chunk), 1)
    causal = s_idx <= t_idx
    bcum_c = jnp.sum(jnp.where(causal, lf_r, 0.0), axis=1, keepdims=True)
    bcum_r = jnp.sum(jnp.where(t_idx <= s_idx, lf_c, 0.0), axis=0, keepdims=True)
    b_last = jnp.sum(lf_r, axis=1, keepdims=True)
    d = jnp.where(causal, bcum_c - bcum_r + ig_r, MASKED)
    m_prev = m_ref[...]
    inter = bcum_c + m_prev
    m_t = jnp.maximum(inter, jnp.max(d, axis=1, keepdims=True))
    w_inter = jnp.exp(inter - m_t)
    s_qk = lax.dot_general(q, k, (((1,), (1,)), ((), ())), preferred_element_type=F32)
    s_qk = (s_qk * jnp.exp(d - m_t)).astype(BF16)
    state = c_ref[...]
    both = w_inter * _dot(q, state.astype(BF16)) + _dot(s_qk, v_ext)
    inv = 1.0 / jnp.maximum(jnp.abs(both[:, e:]), jnp.exp(-m_t))

    a_r = b_last - bcum_r + ig_r
    a_c = b_last - bcum_c + ig_c
    m_new = jnp.maximum(b_last + m_prev, jnp.max(a_r, axis=1, keepdims=True))
    decay = jnp.exp(b_last + m_prev - m_new)
    kw = k * jnp.exp(a_c - m_new).astype(BF16)
    c_ref[...] = decay * state + lax.dot_general(
        kw, v_ext, (((0,), (0,)), ((), ())), preferred_element_type=F32)
    m_ref[...] = m_new
    return both[:, :e], inv


def _mlstm_kernel(q_ref, k_ref, v_ref, gc_ref, gr_ref, og_ref, xc_ref, z_ref, skip_ref,
                  o_ref, c_ref, m_ref):
    @pl.when(pl.program_id(1) == 0)
    def _():
        c_ref[...] = jnp.zeros_like(c_ref)
        m_ref[...] = jnp.zeros_like(m_ref)

    e = HEAD_DIM_C
    nh = N_HEADS_C
    chunk = MLSTM_CHUNK
    for sub in range(q_ref.shape[1] // chunk):
        rs = slice(sub * chunk, (sub + 1) * chunk)
        gcol = gc_ref[0, rs, :]
        grow = gr_ref[0, :, rs]
        for h in range(nh):
            sl = slice(h * e, (h + 1) * e)
            num, inv = _mlstm_head(q_ref[0, rs, sl], k_ref[0, rs, sl], v_ref[0, rs, sl],
                                   gcol[:, h:h + 1], gcol[:, nh + h:nh + h + 1],
                                   grow[h:h + 1, :], grow[nh + h:nh + h + 1, :],
                                   c_ref.at[h], m_ref.at[h])
            msq = jnp.mean(num * num, axis=1, keepdims=True)
            f = inv * lax.rsqrt(inv * inv * msq + RMS_EPS)
            hn = (num * jnp.concatenate([f] * (e // LANES), axis=1) * og_ref[:, sl]).astype(BF16)
            y = hn + skip_ref[:, sl].astype(BF16) * xc_ref[0, rs, sl]
            o_ref[0, rs, sl] = y * _silu(z_ref[0, rs, sl])


def _mlstm(q, k, v, gates, gates_t, out_gain, xc, z, skip):
    b, s, c = q.shape
    step = MLSTM_CHUNK * MLSTM_STEP_CHUNKS
    e = HEAD_DIM_C
    rows = pl.BlockSpec((1, step, c), lambda bi, ci: (bi, ci, 0))
    return pl.pallas_call(
        _mlstm_kernel,
        grid=(b, s // step),
        in_specs=[rows, rows, rows,
                  pl.BlockSpec((1, step, LANES), lambda bi, ci: (bi, ci, 0)),
                  pl.BlockSpec((1, 2 * N_HEADS_C, step), lambda bi, ci: (bi, 0, ci)),
                  _resident((1, c)), rows, rows, _resident((1, c))],
        out_specs=rows,
        out_shape=jax.ShapeDtypeStruct((b, s, c), BF16),
        scratch_shapes=[pltpu.VMEM((N_HEADS_C, e, e + LANES), F32),
                        pltpu.VMEM((N_HEADS_C, 1, 1), F32)],
        compiler_params=_params("parallel", "arbitrary"),
        name="mlstm_chunkwise",
    )(q, k, v, gates, gates_t, out_gain.reshape(1, c), xc, z, skip.reshape(1, c))


def kernel(x, rel_bias, norm_gains, ffn_w_gate, ffn_w_up, ffn_w_down, ev_w_in, ev_q_gain, ev_k_gain, ev_pool_w, ev_pool_scale, ev_w_out, od_w_in, od_conv_w, od_conv_b, od_wq, od_wk, od_wv, od_w_gates, od_b_gates, od_skip, od_out_gain, od_w_out):
    b, s, d = x.shape
    t = b * s
    x = x.reshape(t, d)
    bias_rows = _attn_bias_rows(rel_bias, s, ATTN_TILE)
    ffn_w = (ffn_w_gate.astype(BF16), ffn_w_up.astype(BF16), ffn_w_down.astype(BF16))
    ev_w_in, ev_w_out = ev_w_in.astype(BF16), ev_w_out.astype(BF16)
    od_w_in, od_w_out = od_w_in.astype(BF16), od_w_out.astype(BF16)
    for layer in range(DEPTH):
        g = norm_gains[layer]
        i = layer // 2
        x = _ffn(x, g[0], *ffn_w, layer, 0)
        if layer % 2 == 0:
            q, k, v, pooled = _proj_even(x.reshape(b, s, d), g[1], ev_w_in, i, ev_q_gain[i],
                                         ev_k_gain[i], ev_pool_w[i], ev_pool_scale[i])
            attn = _attention(q, k, v, bias_rows)
            mix, w_out = (attn.reshape(t, WIDTH_A), pooled.reshape(t, WIDTH_B)), ev_w_out
        else:
            xc, q, k, v, z, gates, gates_t = _odd_in(
                x.reshape(b, s, d), g[1], od_w_in, i, od_conv_w[i], od_conv_b[i], od_wq[i],
                od_wk[i], od_wv[i], od_w_gates[i], od_b_gates[i])
            y = _mlstm(q, k, v, gates, gates_t, od_out_gain[i], xc, z, od_skip[i])
            mix, w_out = (y.reshape(t, INNER_C),), od_w_out
        x = _ffn(x, g[2], *ffn_w, layer, 1, mix, w_out, i)
    return x.reshape(b, s, d)
```

```python
import functools
import math

import numpy as np
import jax
import jax.numpy as jnp
from jax import lax
from jax.experimental import pallas as pl
from jax.experimental.pallas import tpu as pltpu

F32 = jnp.float32
BF16 = jnp.bfloat16

D_MODEL = 1024
DEPTH = 4
N_HEADS_A = 8
HEAD_DIM_A = 64
WIDTH_A = N_HEADS_A * HEAD_DIM_A
DILATED_PATTERNS = ((128, 1), (512, 4), (2048, 16))
N_POOL_GROUPS = 4
POOL_WINDOWS = (2, 4, 8, 16)
WIDTH_B = D_MODEL // 2
POOL_GROUP_DIM = WIDTH_B // N_POOL_GROUPS
IN_WIDTH_EVEN = 3 * WIDTH_A + WIDTH_B
REL_BUCKETS = 32
REL_MAX_DISTANCE = 2048
N_HEADS_C = 4
INNER_C = 2 * D_MODEL
HEAD_DIM_C = INNER_C // N_HEADS_C
CONV_WIDTH = 4
QKV_BLOCK = 4
D_FF = 256 * ((8 * D_MODEL // 3 + 255) // 256)
FFN_RESIDUAL = 0.5
RMS_EPS = 1e-6

LANES = 128
SUBLANES = 8
VMEM_LIMIT = 56 * 1024 * 1024
MASKED = -1e30

TOKEN_TILE = 512
EVEN_IN_TILE = 1024
ATTN_TILE = 256
MLSTM_CHUNK = 256
ODD_CHANNEL_BLOCK = 256


def _params(*sem):
    return pltpu.CompilerParams(dimension_semantics=sem, vmem_limit_bytes=VMEM_LIMIT)


def _resident(shape):
    nd = len(shape)
    return pl.BlockSpec(shape, lambda *_: (0,) * nd, pipeline_mode=pl.Buffered(1))


def _stacked(shape, *lead):
    nd = len(shape)
    return pl.BlockSpec((None,) * len(lead) + tuple(shape), lambda *_: tuple(lead) + (0,) * nd,
                        pipeline_mode=pl.Buffered(1))


def _rms(x, gain):
    return x * lax.rsqrt(jnp.mean(x * x, axis=-1, keepdims=True) + RMS_EPS) * gain


def _silu(x):
    return x / (1.0 + jnp.exp(-x))


def _dot(a, b):
    return jnp.dot(a, b, preferred_element_type=F32)


def _ffn_kernel(*refs, n_mix):
    x_ref, mix_refs, rest = refs[0], refs[1:1 + n_mix], refs[1 + n_mix:]
    if n_mix:
        wo_ref, rest = rest[0], rest[1:]
    g_ref, wg_ref, wu_ref, wd_ref, o_ref = rest
    tm = x_ref.shape[0]
    parts = [slice(p * tm // 2, (p + 1) * tm // 2) for p in range(2)]
    xs = []
    for rs in parts:
        x = x_ref[rs, :]
        row = 0
        for m_ref in mix_refs:
            width = m_ref.shape[1]
            x = x + _dot(m_ref[rs, :], wo_ref[row:row + width, :])
            row += width
        xs.append(x)
    xn = [_rms(x, g_ref[...]).astype(BF16) for x in xs]
    h = [(_silu(_dot(a, wg_ref[...])) * _dot(a, wu_ref[...])).astype(BF16) for a in xn]
    for rs, x, hh in zip(parts, xs, h):
        o_ref[rs, :] = x + FFN_RESIDUAL * _dot(hh, wd_ref[...])


def _ffn(x, gain, w_gate, w_up, w_down, layer, half, mix=(), w_out=None, w_out_index=None):
    t = x.shape[0]
    tm = TOKEN_TILE
    row = lambda n: pl.BlockSpec((tm, n), lambda i: (i, 0))
    mix_specs = [row(m.shape[1]) for m in mix]
    mix_args = list(mix)
    if mix:
        mix_specs.append(_stacked(w_out.shape[1:], w_out_index))
        mix_args.append(w_out)
    return pl.pallas_call(
        functools.partial(_ffn_kernel, n_mix=len(mix)),
        grid=(t // tm,),
        in_specs=[row(D_MODEL), *mix_specs, _resident((1, D_MODEL)),
                  _stacked((D_MODEL, D_FF), layer, half), _stacked((D_MODEL, D_FF), layer, half),
                  _stacked((D_FF, D_MODEL), layer, half)],
        out_specs=row(D_MODEL),
        out_shape=jax.ShapeDtypeStruct((t, D_MODEL), F32),
        compiler_params=_params("parallel"),
        name="ffn",
    )(x, *mix_args, gain.reshape(1, D_MODEL), w_gate, w_up, w_down)


NORM_SLAB = 2 * LANES


def _head_sum_matrix():
    r = lax.broadcasted_iota(jnp.int32, (NORM_SLAB, NORM_SLAB), 0) // HEAD_DIM_A
    c = lax.broadcasted_iota(jnp.int32, (NORM_SLAB, NORM_SLAB), 1) // HEAD_DIM_A
    return jnp.where(r == c, 1.0, 0.0).astype(BF16)


def _head_norm(t, same_head, gain, scale):
    ss = _dot((t * t).astype(BF16), same_head)
    return t * lax.rsqrt(ss * (1.0 / HEAD_DIM_A) + RMS_EPS) * (gain * scale)


POOL_HALO = 16


def _pool_diffs(u, tail_ref, first_pos):
    ts = u.shape[0]
    pos = first_pos + lax.broadcasted_iota(jnp.int32, (ts, POOL_GROUP_DIM), 0)
    diffs = []
    for gi, window in enumerate(POOL_WINDOWS):
        gs = slice(gi * POOL_GROUP_DIM, (gi + 1) * POOL_GROUP_DIM)
        acc = jnp.concatenate([tail_ref[:, gs], u[:, gs]], axis=0)
        span = 1
        while span < window:
            acc = acc + pltpu.roll(acc, span, 0)
            span *= 2
        count = jnp.minimum(pos, window).astype(F32)
        diffs.append(acc[POOL_HALO:] / count - u[:, gs])
    tail_ref[...] = u[ts - POOL_HALO:, :]
    return jnp.concatenate(diffs, axis=1)


def _proj_even_kernel(x_ref, g_ref, w_ref, qg_ref, kg_ref, wp_ref, ps_ref,
                      q_ref, k_ref, v_ref, p_ref, tail_ref):
    si = pl.program_id(1)
    ts = x_ref.shape[1]

    @pl.when(si == 0)
    def _():
        tail_ref[...] = jnp.zeros_like(tail_ref)

    xn = _rms(x_ref[0], g_ref[...]).astype(BF16)
    proj = _dot(xn, w_ref[...])
    same_head = _head_sum_matrix()
    for j in range(WIDTH_A // NORM_SLAB):
        sl = slice(j * NORM_SLAB, (j + 1) * NORM_SLAB)
        q_ref[0, :, sl] = _head_norm(proj[:, sl], same_head, qg_ref[...],
                                     HEAD_DIM_A ** -0.5).astype(BF16)
        ks = slice(WIDTH_A + j * NORM_SLAB, WIDTH_A + (j + 1) * NORM_SLAB)
        k_ref[0, :, sl] = _head_norm(proj[:, ks], same_head, kg_ref[...], 1.0).astype(BF16)
    v_ref[0] = proj[:, 2 * WIDTH_A:3 * WIDTH_A].astype(BF16)
    diff = _pool_diffs(proj[:, 3 * WIDTH_A:], tail_ref, si * ts + 1).astype(BF16)
    for j in range(WIDTH_B // NORM_SLAB):
        sl = slice(j * NORM_SLAB, (j + 1) * NORM_SLAB)
        p_ref[0, :, sl] = (_dot(diff[:, sl], wp_ref[j]) * ps_ref[:, sl]).astype(BF16)


def _proj_even(x, gain, w_in, i, q_gain, k_gain, pool_w, pool_scale):
    b, s, _ = x.shape
    ts = EVEN_IN_TILE
    assert POOL_WINDOWS == (2, 4, 8, 16) and max(POOL_WINDOWS) <= POOL_HALO
    rows = lambda n: pl.BlockSpec((1, ts, n), lambda bi, si: (bi, si, 0))
    pair = lambda g: jnp.tile(g, NORM_SLAB // HEAD_DIM_A).reshape(1, NORM_SLAB)
    zero = jnp.zeros_like(pool_w[0])
    wp = jnp.stack([jnp.block([[pool_w[2 * j], zero], [zero, pool_w[2 * j + 1]]])
                    for j in range(N_POOL_GROUPS // 2)])
    act = jax.ShapeDtypeStruct((b, s, WIDTH_A), BF16)
    return pl.pallas_call(
        _proj_even_kernel,
        grid=(b, s // ts),
        in_specs=[rows(D_MODEL), _resident((1, D_MODEL)), _stacked((D_MODEL, IN_WIDTH_EVEN), i),
                  _resident((1, NORM_SLAB)), _resident((1, NORM_SLAB)),
                  _resident((N_POOL_GROUPS // 2, NORM_SLAB, NORM_SLAB)), _resident((1, WIDTH_B))],
        out_specs=[rows(WIDTH_A), rows(WIDTH_A), rows(WIDTH_A), rows(WIDTH_B)],
        out_shape=[act, act, act, jax.ShapeDtypeStruct((b, s, WIDTH_B), BF16)],
        scratch_shapes=[pltpu.VMEM((POOL_HALO, WIDTH_B), F32)],
        compiler_params=_params("parallel", "arbitrary"),
        name="proj_even",
    )(x, gain.reshape(1, D_MODEL), w_in, pair(q_gain), pair(k_gain), wp.astype(BF16),
      pool_scale.reshape(1, WIDTH_B))


def _t5_bucket(distance):
    max_exact = REL_BUCKETS // 2
    d = jnp.maximum(distance.astype(F32), 1.0)
    large = max_exact + (jnp.log(d / max_exact) / math.log(REL_MAX_DISTANCE / max_exact)
                         * (REL_BUCKETS - max_exact)).astype(jnp.int32)
    large = jnp.minimum(large, REL_BUCKETS - 1)
    return jnp.where(distance < max_exact, distance, large)


def _attn_bias_rows(rel_bias, seq, tile):
    dist = np.arange(seq)
    mult = np.zeros(seq, np.int64)
    for window, dilation in DILATED_PATTERNS:
        mult += (dist % dilation == 0) & (dist <= window)
    log_mult = np.where(mult > 0, np.log(np.maximum(mult, 1)), MASKED).astype(np.float32)
    per_dist = rel_bias[_t5_bucket(jnp.asarray(dist, jnp.int32))].astype(F32)
    per_dist = jnp.where(jnp.asarray(mult > 0)[:, None], per_dist + log_mult[:, None], MASKED)
    per_dist = per_dist.T
    base = seq - tile
    heads = per_dist.shape[0]
    masked = jnp.full((heads, tile), MASKED, F32)
    v = jnp.concatenate([per_dist[:, base:], masked, per_dist[:, :base]], axis=1)
    w = jnp.roll(v[:, ::-1], 1, axis=1)
    return w.reshape(heads // 2, 2, 1, seq + tile)


def _attn_kernel(q_ref, k_ref, v_ref, w_ref, o_ref, tab_ref, vext_ref, *, tile):
    seq = q_ref.shape[1]
    nq = seq // tile

    vext_ref[:, :LANES] = v_ref[0]
    vext_ref[:, LANES:] = jnp.ones((seq, LANES), BF16)

    @pl.when(pl.program_id(1) == 0)
    def _():
        for hh in range(2):
            rows = jnp.broadcast_to(w_ref[0, hh], (tile, seq + tile))
            tab_ref[hh] = pltpu.roll(rows, 0, 1, stride=1, stride_axis=0)[:, :seq]

    first = lax.broadcasted_iota(jnp.int32, (tile, LANES), 1) < HEAD_DIM_A
    for qi in reversed(range(nq)):
        n = (qi + 1) * tile
        off = (nq - 1 - qi) * tile
        q = q_ref[0, qi * tile:(qi + 1) * tile, :]
        zero = jnp.zeros_like(q)
        q2 = jnp.concatenate([jnp.where(first, q, zero), jnp.where(first, zero, q)], axis=0)
        s = lax.dot_general(q2, k_ref[0, :n, :], (((1,), (1,)), ((), ())),
                            preferred_element_type=F32)
        s = s + jnp.concatenate([tab_ref[0, :, off:off + n], tab_ref[1, :, off:off + n]], axis=0)
        p = jnp.exp(s - jnp.max(s, axis=-1, keepdims=True))
        o = _dot(p.astype(BF16), vext_ref[:n, :])
        o = o[:, :LANES] / o[:, LANES:]
        o_ref[0, qi * tile:(qi + 1) * tile, :] = jnp.where(first, o[:tile], o[tile:]).astype(BF16)


def _attention(q, k, v, bias_rows):
    b, s, _ = q.shape
    tile = ATTN_TILE
    seq_spec = pl.BlockSpec((1, s, LANES), lambda h, bi: (bi, 0, h))
    return pl.pallas_call(
        functools.partial(_attn_kernel, tile=tile),
        grid=(WIDTH_A // LANES, b),
        in_specs=[seq_spec, seq_spec, seq_spec,
                  pl.BlockSpec((1, 2, 1, s + tile), lambda h, bi: (h, 0, 0, 0))],
        out_specs=seq_spec,
        out_shape=jax.ShapeDtypeStruct((b, s, WIDTH_A), BF16),
        scratch_shapes=[pltpu.VMEM((2, tile, s), F32), pltpu.VMEM((s, 2 * LANES), BF16)],
        compiler_params=_params("arbitrary", "arbitrary"),
        name="dilated_attention",
    )(q, k, v, bias_rows)


def _shift_rows_halo(a, tail, back, row8):
    halo = tail.shape[0]
    rolled = pltpu.roll(a, back, 0)
    head = jnp.where(row8 < back, pltpu.roll(tail, back, 0), rolled[:halo])
    return jnp.concatenate([head, rolled[halo:]], axis=0)


def _odd_in_kernel(x_ref, g_ref, w_ref, cw_ref, cb_ref, wmix_ref, bg_ref,
                   xc_ref, q_ref, k_ref, v_ref, z_ref, gates_ref, gates_t_ref, tail_ref):
    ts = x_ref.shape[1]
    halo = tail_ref.shape[0]
    cbw = ODD_CHANNEL_BLOCK
    row8 = lax.broadcasted_iota(jnp.int32, (halo, cbw), 0)

    @pl.when(pl.program_id(1) == 0)
    def _():
        tail_ref[...] = jnp.zeros_like(tail_ref)

    xn = _rms(x_ref[0], g_ref[...]).astype(BF16)
    gates = jnp.broadcast_to(bg_ref[...], (ts, LANES))
    n_blocks = INNER_C // cbw
    xm_next = _dot(xn, w_ref[:, :cbw])
    for cb in range(n_blocks):
        cs = slice(cb * cbw, (cb + 1) * cbw)
        xm = xm_next
        if cb + 1 < n_blocks:
            xm_next = _dot(xn, w_ref[:, (cb + 1) * cbw:(cb + 2) * cbw])
        z_ref[0, :, cs] = _dot(xn, w_ref[:, INNER_C + cb * cbw:INNER_C + (cb + 1) * cbw]
                               ).astype(BF16)
        tail = tail_ref[:, cs]
        tail_ref[:, cs] = xm[ts - halo:, :]
        w0, w1, w2, w3 = (cw_ref[tap:tap + 1, cs] for tap in range(CONV_WIDTH))
        xm_1 = _shift_rows_halo(xm, tail, 1, row8)
        far = xm * w1 + xm_1 * w0
        far_tail = tail * w1 + pltpu.roll(tail, 1, 0) * w0
        conv = xm * w3 + xm_1 * w2 + _shift_rows_halo(far, far_tail, 2, row8)
        xc16 = _silu(conv + cb_ref[:, cs]).astype(BF16)
        xm16 = xm.astype(BF16)
        xc_ref[0, :, cs] = xc16
        for gl in range(cbw // LANES):
            ls = slice(gl * LANES, (gl + 1) * LANES)
            gi = cb * (cbw // LANES) + gl
            sl = slice(gi * LANES, (gi + 1) * LANES)
            out = _dot(jnp.concatenate([xc16[:, ls], xm16[:, ls]], axis=1), wmix_ref[gi])
            q_ref[0, :, sl] = out[:, :LANES].astype(BF16)
            k_ref[0, :, sl] = out[:, LANES:2 * LANES].astype(BF16)
            v_ref[0, :, sl] = out[:, 2 * LANES:3 * LANES].astype(BF16)
            gates = gates + out[:, 3 * LANES:]
    gates_ref[0] = gates
    gates_t_ref[0] = gates.T


def _block_diag_tiles(w):
    per_tile = LANES // QKV_BLOCK
    w = w.reshape(-1, per_tile, QKV_BLOCK, QKV_BLOCK)
    eye = jnp.eye(per_tile, dtype=w.dtype)
    dense = jnp.einsum("tgio,gh->tgiho", w, eye)
    return dense.reshape(-1, LANES, LANES)


def _odd_in(x, gain, w_in, i, conv_w, conv_b, wq, wk, wv, w_gates, b_gates):
    b, s, _ = x.shape
    c = INNER_C
    ts = TOKEN_TILE
    n_gates = 2 * N_HEADS_C
    wq_t, wk_t, wv_t = _block_diag_tiles(wq), _block_diag_tiles(wk), _block_diag_tiles(wv)
    wg = w_gates.reshape(3, c // LANES, LANES, n_gates)
    compose = lambda w_t, g: jnp.einsum("tij,tjn->tin", w_t, g, precision=lax.Precision.HIGHEST)
    pad_gates = lambda g: jnp.pad(g, ((0, 0), (0, 0), (0, LANES - n_gates)))
    zeros = jnp.zeros_like(wq_t)
    wmix = jnp.concatenate([
        jnp.concatenate([wq_t, wk_t * HEAD_DIM_C ** -0.5, zeros,
                         pad_gates(compose(wq_t, wg[0]) + compose(wk_t, wg[1]))], axis=2),
        jnp.concatenate([zeros, zeros, wv_t, pad_gates(compose(wv_t, wg[2]))], axis=2)], axis=1)
    bg = jnp.pad(b_gates.reshape(1, n_gates), ((0, 0), (0, LANES - n_gates)))
    rows = lambda n: pl.BlockSpec((1, ts, n), lambda bi, si: (bi, si, 0))
    act = jax.ShapeDtypeStruct((b, s, c), BF16)
    return pl.pallas_call(
        _odd_in_kernel,
        grid=(b, s // ts),
        in_specs=[rows(D_MODEL), _resident((1, D_MODEL)), _stacked((D_MODEL, 2 * c), i),
                  _resident((CONV_WIDTH, c)), _resident((1, c)),
                  _resident((c // LANES, 2 * LANES, 4 * LANES)), _resident((1, LANES))],
        out_specs=[rows(c), rows(c), rows(c), rows(c), rows(c), rows(LANES),
                   pl.BlockSpec((1, LANES, ts), lambda bi, si: (bi, 0, si))],
        out_shape=[act, act, act, act, act, jax.ShapeDtypeStruct((b, s, LANES), F32),
                   jax.ShapeDtypeStruct((b, LANES, s), F32)],
        scratch_shapes=[pltpu.VMEM((SUBLANES, c), F32)],
        compiler_params=_params("parallel", "arbitrary"),
        name="odd_in",
    )(x, gain.reshape(1, D_MODEL), w_in, conv_w, conv_b.reshape(1, c), wmix.astype(BF16), bg)


def _log_sigmoid(x):
    return jnp.minimum(x, 0.0) - jnp.log(1.0 + jnp.exp(-jnp.abs(x)))


def _mlstm_head(q, k, v, ig_c, fg_c, ig_r, fg_r, c_ref, m_ref):
    chunk, e = q.shape
    v_ext = jnp.concatenate([v, jnp.ones((chunk, LANES), BF16)], axis=1)
    lf_c = _log_sigmoid(fg_c)
    lf_r = _log_sigmoid(fg_r)
    t_idx = lax.broadcasted_iota(jnp.int32, (chunk, chunk), 0)
    s_idx = lax.broadcasted_iota(jnp.int32, (chunk, chunk), 1)
    causal = s_idx <= t_idx
    bcum_c = jnp.sum(jnp.where(causal, lf_r, 0.0), axis=1, keepdims=True)
    bcum_r = jnp.sum(jnp.where(t_idx <= s_idx, lf_c, 0.0), axis=0, keepdims=True)
    b_last = jnp.sum(lf_r, axis=1, keepdims=True)
    d = jnp.where(causal, bcum_c - bcum_r + ig_r, MASKED)
    m_prev = m_ref[...]
    inter = bcum_c + m_prev
    m_t = jnp.maximum(inter, jnp.max(d, axis=1, keepdims=True))
    w_inter = jnp.exp(inter - m_t)
    s_qk = lax.dot_general(q, k, (((1,), (1,)), ((), ())), preferred_element_type=F32)
    s_qk = (s_qk * jnp.exp(d - m_t)).astype(BF16)
    state = c_ref[...]
    both = w_inter * _dot(q, state.astype(BF16)) + _dot(s_qk, v_ext)
    inv = 1.0 / jnp.maximum(jnp.abs(both[:, e:]), jnp.exp(-m_t))

    a_r = b_last - bcum_r + ig_r
    a_c = b_last - bcum_c + ig_c
    m_new = jnp.maximum(b_last + m_prev, jnp.max(a_r, axis=1, keepdims=True))
    decay = jnp.exp(b_last + m_prev - m_new)
    kw = k * jnp.exp(a_c - m_new).astype(BF16)
    c_ref[...] = decay * state + lax.dot_general(
        kw, v_ext, (((0,), (0,)), ((), ())), preferred_element_type=F32)
    m_ref[...] = m_new
    return both[:, :e], inv


def _mlstm_kernel(q_ref, k_ref, v_ref, gc_ref, gr_ref, og_ref, xc_ref, z_ref, skip_ref,
                  o_ref, c_ref, m_ref):
    @pl.when(pl.program_id(1) == 0)
    def _():
        c_ref[...] = jnp.zeros_like(c_ref)
        m_ref[...] = jnp.zeros_like(m_ref)

    e = HEAD_DIM_C
    nh = N_HEADS_C
    gcol = gc_ref[0]
    grow = gr_ref[0]
    for h in range(nh):
        sl = slice(h * e, (h + 1) * e)
        num, inv = _mlstm_head(q_ref[0, :, sl], k_ref[0, :, sl], v_ref[0, :, sl],
                               gcol[:, h:h + 1], gcol[:, nh + h:nh + h + 1],
                               grow[h:h + 1, :], grow[nh + h:nh + h + 1, :],
                               c_ref.at[h], m_ref.at[h])
        msq = jnp.mean(num * num, axis=1, keepdims=True)
        f = inv * lax.rsqrt(inv * inv * msq + RMS_EPS)
        hn = (num * jnp.concatenate([f] * (e // LANES), axis=1) * og_ref[:, sl]).astype(BF16)
        y = hn + skip_ref[:, sl].astype(BF16) * xc_ref[0, :, sl]
        o_ref[0, :, sl] = y * _silu(z_ref[0, :, sl])


def _mlstm(q, k, v, gates, gates_t, out_gain, xc, z, skip):
    b, s, c = q.shape
    chunk = MLSTM_CHUNK
    e = HEAD_DIM_C
    rows = pl.BlockSpec((1, chunk, c), lambda bi, ci: (bi, ci, 0))
    return pl.pallas_call(
        _mlstm_kernel,
        grid=(b, s // chunk),
        in_specs=[rows, rows, rows,
                  pl.BlockSpec((1, chunk, LANES), lambda bi, ci: (bi, ci, 0)),
                  pl.BlockSpec((1, 2 * N_HEADS_C, chunk), lambda bi, ci: (bi, 0, ci)),
                  _resident((1, c)), rows, rows, _resident((1, c))],
        out_specs=rows,
        out_shape=jax.ShapeDtypeStruct((b, s, c), BF16),
        scratch_shapes=[pltpu.VMEM((N_HEADS_C, e, e + LANES), F32),
                        pltpu.VMEM((N_HEADS_C, 1, 1), F32)],
        compiler_params=_params("parallel", "arbitrary"),
        name="mlstm_chunkwise",
    )(q, k, v, gates, gates_t, out_gain.reshape(1, c), xc, z, skip.reshape(1, c))


def kernel(x, rel_bias, norm_gains, ffn_w_gate, ffn_w_up, ffn_w_down, ev_w_in, ev_q_gain, ev_k_gain, ev_pool_w, ev_pool_scale, ev_w_out, od_w_in, od_conv_w, od_conv_b, od_wq, od_wk, od_wv, od_w_gates, od_b_gates, od_skip, od_out_gain, od_w_out):
    b, s, d = x.shape
    t = b * s
    x = x.reshape(t, d)
    bias_rows = _attn_bias_rows(rel_bias, s, ATTN_TILE)
    ffn_w = (ffn_w_gate.astype(BF16), ffn_w_up.astype(BF16), ffn_w_down.astype(BF16))
    ev_w_in, ev_w_out = ev_w_in.astype(BF16), ev_w_out.astype(BF16)
    od_w_in, od_w_out = od_w_in.astype(BF16), od_w_out.astype(BF16)
    for layer in range(DEPTH):
        g = norm_gains[layer]
        i = layer // 2
        x = _ffn(x, g[0], *ffn_w, layer, 0)
        if layer % 2 == 0:
            q, k, v, pooled = _proj_even(x.reshape(b, s, d), g[1], ev_w_in, i, ev_q_gain[i],
                                         ev_k_gain[i], ev_pool_w[i], ev_pool_scale[i])
            attn = _attention(q, k, v, bias_rows)
            mix, w_out = (attn.reshape(t, WIDTH_A), pooled.reshape(t, WIDTH_B)), ev_w_out
        else:
            xc, q, k, v, z, gates, gates_t = _odd_in(
                x.reshape(b, s, d), g[1], od_w_in, i, od_conv_w[i], od_conv_b[i], od_wq[i],
                od_wk[i], od_wv[i], od_w_gates[i], od_b_gates[i])
            y = _mlstm(q, k, v, gates, gates_t, od_out_gain[i], xc, z, od_skip[i])
            mix, w_out = (y.reshape(t, INNER_C),), od_w_out
        x = _ffn(x, g[2], *ffn_w, layer, 1, mix, w_out, i)
    return x.reshape(b, s, d)
```

```python
import functools
import math

import numpy as np
import jax
import jax.numpy as jnp
from jax import lax
from jax.experimental import pallas as pl
from jax.experimental.pallas import tpu as pltpu

F32 = jnp.float32
BF16 = jnp.bfloat16

D_MODEL = 1024
DEPTH = 4
N_HEADS_A = 8
HEAD_DIM_A = 64
WIDTH_A = N_HEADS_A * HEAD_DIM_A
DILATED_PATTERNS = ((128, 1), (512, 4), (2048, 16))
N_POOL_GROUPS = 4
POOL_WINDOWS = (2, 4, 8, 16)
WIDTH_B = D_MODEL // 2
POOL_GROUP_DIM = WIDTH_B // N_POOL_GROUPS
IN_WIDTH_EVEN = 3 * WIDTH_A + WIDTH_B
REL_BUCKETS = 32
REL_MAX_DISTANCE = 2048
N_HEADS_C = 4
INNER_C = 2 * D_MODEL
HEAD_DIM_C = INNER_C // N_HEADS_C
CONV_WIDTH = 4
QKV_BLOCK = 4
D_FF = 256 * ((8 * D_MODEL // 3 + 255) // 256)
FFN_RESIDUAL = 0.5
RMS_EPS = 1e-6

LANES = 128
SUBLANES = 8
VMEM_LIMIT = 56 * 1024 * 1024
MASKED = -1e30

TOKEN_TILE = 512
EVEN_IN_TILE = 1024
ATTN_TILE = 256
MLSTM_CHUNK = 256
ODD_CHANNEL_BLOCK = 256


def _params(*sem):
    return pltpu.CompilerParams(dimension_semantics=sem, vmem_limit_bytes=VMEM_LIMIT)


def _resident(shape):
    nd = len(shape)
    return pl.BlockSpec(shape, lambda *_: (0,) * nd, pipeline_mode=pl.Buffered(1))


def _stacked(shape, *lead):
    nd = len(shape)
    return pl.BlockSpec((None,) * len(lead) + tuple(shape), lambda *_: tuple(lead) + (0,) * nd,
                        pipeline_mode=pl.Buffered(1))


def _rms(x, gain):
    return x * lax.rsqrt(jnp.mean(x * x, axis=-1, keepdims=True) + RMS_EPS) * gain


def _silu(x):
    return x / (1.0 + jnp.exp(-x))


def _dot(a, b):
    return jnp.dot(a, b, preferred_element_type=F32)


def _ffn_kernel(*refs, n_mix):
    x_ref, mix_refs, rest = refs[0], refs[1:1 + n_mix], refs[1 + n_mix:]
    if n_mix:
        wo_ref, rest = rest[0], rest[1:]
    g_ref, wg_ref, wu_ref, wd_ref, o_ref = rest
    tm = x_ref.shape[0]
    parts = [slice(p * tm // 2, (p + 1) * tm // 2) for p in range(2)]
    xs = []
    for rs in parts:
        x = x_ref[rs, :]
        row = 0
        for m_ref in mix_refs:
            width = m_ref.shape[1]
            x = x + _dot(m_ref[rs, :], wo_ref[row:row + width, :])
            row += width
        xs.append(x)
    xn = [_rms(x, g_ref[...]).astype(BF16) for x in xs]
    h = [(_silu(_dot(a, wg_ref[...])) * _dot(a, wu_ref[...])).astype(BF16) for a in xn]
    for rs, x, hh in zip(parts, xs, h):
        o_ref[rs, :] = x + FFN_RESIDUAL * _dot(hh, wd_ref[...])


def _ffn(x, gain, w_gate, w_up, w_down, layer, half, mix=(), w_out=None, w_out_index=None):
    t = x.shape[0]
    tm = TOKEN_TILE
    row = lambda n: pl.BlockSpec((tm, n), lambda i: (i, 0))
    mix_specs = [row(m.shape[1]) for m in mix]
    mix_args = list(mix)
    if mix:
        mix_specs.append(_stacked(w_out.shape[1:], w_out_index))
        mix_args.append(w_out)
    return pl.pallas_call(
        functools.partial(_ffn_kernel, n_mix=len(mix)),
        grid=(t // tm,),
        in_specs=[row(D_MODEL), *mix_specs, _resident((1, D_MODEL)),
                  _stacked((D_MODEL, D_FF), layer, half), _stacked((D_MODEL, D_FF), layer, half),
                  _stacked((D_FF, D_MODEL), layer, half)],
        out_specs=row(D_MODEL),
        out_shape=jax.ShapeDtypeStruct((t, D_MODEL), F32),
        compiler_params=_params("parallel"),
        name="ffn",
    )(x, *mix_args, gain.reshape(1, D_MODEL), w_gate, w_up, w_down)


NORM_SLAB = 2 * LANES


def _head_sum_matrix():
    r = lax.broadcasted_iota(jnp.int32, (NORM_SLAB, NORM_SLAB), 0) // HEAD_DIM_A
    c = lax.broadcasted_iota(jnp.int32, (NORM_SLAB, NORM_SLAB), 1) // HEAD_DIM_A
    return jnp.where(r == c, 1.0, 0.0).astype(BF16)


def _head_norm(t, same_head, gain, scale):
    ss = _dot((t * t).astype(BF16), same_head)
    return t * lax.rsqrt(ss * (1.0 / HEAD_DIM_A) + RMS_EPS) * (gain * scale)


POOL_HALO = 16


def _pool_diffs(u, tail_ref, first_pos):
    ts = u.shape[0]
    pos = first_pos + lax.broadcasted_iota(jnp.int32, (ts, POOL_GROUP_DIM), 0)
    diffs = []
    for gi, window in enumerate(POOL_WINDOWS):
        gs = slice(gi * POOL_GROUP_DIM, (gi + 1) * POOL_GROUP_DIM)
        acc = jnp.concatenate([tail_ref[:, gs], u[:, gs]], axis=0)
        span = 1
        while span < window:
            acc = acc + pltpu.roll(acc, span, 0)
            span *= 2
        count = jnp.minimum(pos, window).astype(F32)
        diffs.append(acc[POOL_HALO:] / count - u[:, gs])
    tail_ref[...] = u[ts - POOL_HALO:, :]
    return jnp.concatenate(diffs, axis=1)


def _proj_even_kernel(x_ref, g_ref, w_ref, qg_ref, kg_ref, wp_ref, ps_ref,
                      q_ref, k_ref, v_ref, p_ref, tail_ref):
    si = pl.program_id(1)
    ts = x_ref.shape[1]

    @pl.when(si == 0)
    def _():
        tail_ref[...] = jnp.zeros_like(tail_ref)

    xn = _rms(x_ref[0], g_ref[...]).astype(BF16)
    u = _dot(xn, w_ref[:, 3 * WIDTH_A:])
    diff = _pool_diffs(u, tail_ref, si * ts + 1).astype(BF16)
    same_head = _head_sum_matrix()
    q = _dot(xn, w_ref[:, :WIDTH_A])
    k = _dot(xn, w_ref[:, WIDTH_A:2 * WIDTH_A])
    v_ref[0] = _dot(xn, w_ref[:, 2 * WIDTH_A:3 * WIDTH_A]).astype(BF16)
    for j in range(WIDTH_A // NORM_SLAB):
        sl = slice(j * NORM_SLAB, (j + 1) * NORM_SLAB)
        q_ref[0, :, sl] = _head_norm(q[:, sl], same_head, qg_ref[...],
                                     HEAD_DIM_A ** -0.5).astype(BF16)
        k_ref[0, :, sl] = _head_norm(k[:, sl], same_head, kg_ref[...], 1.0).astype(BF16)
    for j in range(WIDTH_B // NORM_SLAB):
        sl = slice(j * NORM_SLAB, (j + 1) * NORM_SLAB)
        p_ref[0, :, sl] = (_dot(diff[:, sl], wp_ref[j]) * ps_ref[:, sl]).astype(BF16)


def _proj_even(x, gain, w_in, i, q_gain, k_gain, pool_w, pool_scale):
    b, s, _ = x.shape
    ts = EVEN_IN_TILE
    assert POOL_WINDOWS == (2, 4, 8, 16) and max(POOL_WINDOWS) <= POOL_HALO
    rows = lambda n: pl.BlockSpec((1, ts, n), lambda bi, si: (bi, si, 0))
    pair = lambda g: jnp.tile(g, NORM_SLAB // HEAD_DIM_A).reshape(1, NORM_SLAB)
    zero = jnp.zeros_like(pool_w[0])
    wp = jnp.stack([jnp.block([[pool_w[2 * j], zero], [zero, pool_w[2 * j + 1]]])
                    for j in range(N_POOL_GROUPS // 2)])
    act = jax.ShapeDtypeStruct((b, s, WIDTH_A), BF16)
    return pl.pallas_call(
        _proj_even_kernel,
        grid=(b, s // ts),
        in_specs=[rows(D_MODEL), _resident((1, D_MODEL)), _stacked((D_MODEL, IN_WIDTH_EVEN), i),
                  _resident((1, NORM_SLAB)), _resident((1, NORM_SLAB)),
                  _resident((N_POOL_GROUPS // 2, NORM_SLAB, NORM_SLAB)), _resident((1, WIDTH_B))],
        out_specs=[rows(WIDTH_A), rows(WIDTH_A), rows(WIDTH_A), rows(WIDTH_B)],
        out_shape=[act, act, act, jax.ShapeDtypeStruct((b, s, WIDTH_B), BF16)],
        scratch_shapes=[pltpu.VMEM((POOL_HALO, WIDTH_B), F32)],
        compiler_params=_params("parallel", "arbitrary"),
        name="proj_even",
    )(x, gain.reshape(1, D_MODEL), w_in, pair(q_gain), pair(k_gain), wp.astype(BF16),
      pool_scale.reshape(1, WIDTH_B))


def _t5_bucket(distance):
    max_exact = REL_BUCKETS // 2
    d = jnp.maximum(distance.astype(F32), 1.0)
    large = max_exact + (jnp.log(d / max_exact) / math.log(REL_MAX_DISTANCE / max_exact)
                         * (REL_BUCKETS - max_exact)).astype(jnp.int32)
    large = jnp.minimum(large, REL_BUCKETS - 1)
    return jnp.where(distance < max_exact, distance, large)


def _attn_bias_rows(rel_bias, seq, tile):
    dist = np.arange(seq)
    mult = np.zeros(seq, np.int64)
    for window, dilation in DILATED_PATTERNS:
        mult += (dist % dilation == 0) & (dist <= window)
    log_mult = np.where(mult > 0, np.log(np.maximum(mult, 1)), MASKED).astype(np.float32)
    per_dist = rel_bias[_t5_bucket(jnp.asarray(dist, jnp.int32))].astype(F32)
    per_dist = jnp.where(jnp.asarray(mult > 0)[:, None], per_dist + log_mult[:, None], MASKED)
    per_dist = per_dist.T
    base = seq - tile
    heads = per_dist.shape[0]
    masked = jnp.full((heads, tile), MASKED, F32)
    v = jnp.concatenate([per_dist[:, base:], masked, per_dist[:, :base]], axis=1)
    w = jnp.roll(v[:, ::-1], 1, axis=1)
    return w.reshape(heads // 2, 2, 1, seq + tile)


def _attn_kernel(q_ref, k_ref, v_ref, w_ref, o_ref, tab_ref, vext_ref, *, tile):
    seq = q_ref.shape[1]
    nq = seq // tile

    vext_ref[:, :LANES] = v_ref[0]
    vext_ref[:, LANES:] = jnp.ones((seq, LANES), BF16)

    @pl.when(pl.program_id(1) == 0)
    def _():
        for hh in range(2):
            rows = jnp.broadcast_to(w_ref[0, hh], (tile, seq + tile))
            tab_ref[hh] = pltpu.roll(rows, 0, 1, stride=1, stride_axis=0)[:, :seq]

    first = lax.broadcasted_iota(jnp.int32, (tile, LANES), 1) < HEAD_DIM_A
    for qi in reversed(range(nq)):
        n = (qi + 1) * tile
        off = (nq - 1 - qi) * tile
        q = q_ref[0, qi * tile:(qi + 1) * tile, :]
        zero = jnp.zeros_like(q)
        q2 = jnp.concatenate([jnp.where(first, q, zero), jnp.where(first, zero, q)], axis=0)
        s = lax.dot_general(q2, k_ref[0, :n, :], (((1,), (1,)), ((), ())),
                            preferred_element_type=F32)
        s = s + jnp.concatenate([tab_ref[0, :, off:off + n], tab_ref[1, :, off:off + n]], axis=0)
        p = jnp.exp(s - jnp.max(s, axis=-1, keepdims=True))
        o = _dot(p.astype(BF16), vext_ref[:n, :])
        o = o[:, :LANES] / o[:, LANES:]
        o_ref[0, qi * tile:(qi + 1) * tile, :] = jnp.where(first, o[:tile], o[tile:]).astype(BF16)


def _attention(q, k, v, bias_rows):
    b, s, _ = q.shape
    tile = ATTN_TILE
    seq_spec = pl.BlockSpec((1, s, LANES), lambda h, bi: (bi, 0, h))
    return pl.pallas_call(
        functools.partial(_attn_kernel, tile=tile),
        grid=(WIDTH_A // LANES, b),
        in_specs=[seq_spec, seq_spec, seq_spec,
                  pl.BlockSpec((1, 2, 1, s + tile), lambda h, bi: (h, 0, 0, 0))],
        out_specs=seq_spec,
        out_shape=jax.ShapeDtypeStruct((b, s, WIDTH_A), BF16),
        scratch_shapes=[pltpu.VMEM((2, tile, s), F32), pltpu.VMEM((s, 2 * LANES), BF16)],
        compiler_params=_params("arbitrary", "arbitrary"),
        name="dilated_attention",
    )(q, k, v, bias_rows)


def _shift_rows_halo(a, tail, back, row8):
    halo = tail.shape[0]
    rolled = pltpu.roll(a, back, 0)
    head = jnp.where(row8 < back, pltpu.roll(tail, back, 0), rolled[:halo])
    return jnp.concatenate([head, rolled[halo:]], axis=0)


def _odd_in_kernel(x_ref, g_ref, w_ref, cw_ref, cb_ref, wmix_ref, bg_ref,
                   xc_ref, q_ref, k_ref, v_ref, z_ref, gates_ref, gates_t_ref, tail_ref):
    ts = x_ref.shape[1]
    halo = tail_ref.shape[0]
    cbw = ODD_CHANNEL_BLOCK
    row8 = lax.broadcasted_iota(jnp.int32, (halo, cbw), 0)

    @pl.when(pl.program_id(1) == 0)
    def _():
        tail_ref[...] = jnp.zeros_like(tail_ref)

    xn = _rms(x_ref[0], g_ref[...]).astype(BF16)
    gates = jnp.broadcast_to(bg_ref[...], (ts, LANES))
    n_blocks = INNER_C // cbw
    xm_next = _dot(xn, w_ref[:, :cbw])
    for cb in range(n_blocks):
        cs = slice(cb * cbw, (cb + 1) * cbw)
        xm = xm_next
        if cb + 1 < n_blocks:
            xm_next = _dot(xn, w_ref[:, (cb + 1) * cbw:(cb + 2) * cbw])
        tail = tail_ref[:, cs]
        tail_ref[:, cs] = xm[ts - halo:, :]
        w0, w1, w2, w3 = (cw_ref[tap:tap + 1, cs] for tap in range(CONV_WIDTH))
        xm_1 = _shift_rows_halo(xm, tail, 1, row8)
        far = xm * w1 + xm_1 * w0
        far_tail = tail * w1 + pltpu.roll(tail, 1, 0) * w0
        conv = xm * w3 + xm_1 * w2 + _shift_rows_halo(far, far_tail, 2, row8)
        xc16 = _silu(conv + cb_ref[:, cs]).astype(BF16)
        xm16 = xm.astype(BF16)
        xc_ref[0, :, cs] = xc16
        for gl in range(cbw // LANES):
            ls = slice(gl * LANES, (gl + 1) * LANES)
            gi = cb * (cbw // LANES) + gl
            sl = slice(gi * LANES, (gi + 1) * LANES)
            out = _dot(jnp.concatenate([xc16[:, ls], xm16[:, ls]], axis=1), wmix_ref[gi])
            q_ref[0, :, sl] = out[:, :LANES].astype(BF16)
            k_ref[0, :, sl] = out[:, LANES:2 * LANES].astype(BF16)
            v_ref[0, :, sl] = out[:, 2 * LANES:3 * LANES].astype(BF16)
            gates = gates + out[:, 3 * LANES:]
    gates_ref[0] = gates
    gates_t_ref[0] = gates.T
    z_ref[0] = _dot(xn, w_ref[:, INNER_C:]).astype(BF16)


def _block_diag_tiles(w):
    per_tile = LANES // QKV_BLOCK
    w = w.reshape(-1, per_tile, QKV_BLOCK, QKV_BLOCK)
    eye = jnp.eye(per_tile, dtype=w.dtype)
    dense = jnp.einsum("tgio,gh->tgiho", w, eye)
    return dense.reshape(-1, LANES, LANES)


def _odd_in(x, gain, w_in, i, conv_w, conv_b, wq, wk, wv, w_gates, b_gates):
    b, s, _ = x.shape
    c = INNER_C
    ts = TOKEN_TILE
    n_gates = 2 * N_HEADS_C
    wq_t, wk_t, wv_t = _block_diag_tiles(wq), _block_diag_tiles(wk), _block_diag_tiles(wv)
    wg = w_gates.reshape(3, c // LANES, LANES, n_gates)
    compose = lambda w_t, g: jnp.einsum("tij,tjn->tin", w_t, g, precision=lax.Precision.HIGHEST)
    pad_gates = lambda g: jnp.pad(g, ((0, 0), (0, 0), (0, LANES - n_gates)))
    zeros = jnp.zeros_like(wq_t)
    wmix = jnp.concatenate([
        jnp.concatenate([wq_t, wk_t * HEAD_DIM_C ** -0.5, zeros,
                         pad_gates(compose(wq_t, wg[0]) + compose(wk_t, wg[1]))], axis=2),
        jnp.concatenate([zeros, zeros, wv_t, pad_gates(compose(wv_t, wg[2]))], axis=2)], axis=1)
    bg = jnp.pad(b_gates.reshape(1, n_gates), ((0, 0), (0, LANES - n_gates)))
    rows = lambda n: pl.BlockSpec((1, ts, n), lambda bi, si: (bi, si, 0))
    act = jax.ShapeDtypeStruct((b, s, c), BF16)
    return pl.pallas_call(
        _odd_in_kernel,
        grid=(b, s // ts),
        in_specs=[rows(D_MODEL), _resident((1, D_MODEL)), _stacked((D_MODEL, 2 * c), i),
                  _resident((CONV_WIDTH, c)), _resident((1, c)),
                  _resident((c // LANES, 2 * LANES, 4 * LANES)), _resident((1, LANES))],
        out_specs=[rows(c), rows(c), rows(c), rows(c), rows(c), rows(LANES),
                   pl.BlockSpec((1, LANES, ts), lambda bi, si: (bi, 0, si))],
        out_shape=[act, act, act, act, act, jax.ShapeDtypeStruct((b, s, LANES), F32),
                   jax.ShapeDtypeStruct((b, LANES, s), F32)],
        scratch_shapes=[pltpu.VMEM((SUBLANES, c), F32)],
        compiler_params=_params("parallel", "arbitrary"),
        name="odd_in",
    )(x, gain.reshape(1, D_MODEL), w_in, conv_w, conv_b.reshape(1, c), wmix.astype(BF16), bg)


def _log_sigmoid(x):
    return jnp.minimum(x, 0.0) - jnp.log(1.0 + jnp.exp(-jnp.abs(x)))


def _mlstm_head(q, k, v, ig_c, fg_c, ig_r, fg_r, c_ref, m_ref):
    chunk, e = q.shape
    v_ext = jnp.concatenate([v, jnp.ones((chunk, LANES), BF16)], axis=1)
    lf_c = _log_sigmoid(fg_c)
    lf_r = _log_sigmoid(fg_r)
    t_idx = lax.broadcasted_iota(jnp.int32, (chunk, chunk), 0)
    s_idx = lax.broadcasted_iota(jnp.int32, (chunk, chunk), 1)
    causal = s_idx <= t_idx
    bcum_c = jnp.sum(jnp.where(causal, lf_r, 0.0), axis=1, keepdims=True)
    bcum_r = jnp.sum(jnp.where(t_idx <= s_idx, lf_c, 0.0), axis=0, keepdims=True)
    b_last = jnp.sum(lf_r, axis=1, keepdims=True)
    d = jnp.where(causal, bcum_c - bcum_r + ig_r, MASKED)
    m_prev = m_ref[...]
    inter = bcum_c + m_prev
    m_t = jnp.maximum(inter, jnp.max(d, axis=1, keepdims=True))
    w_inter = jnp.exp(inter - m_t)
    s_qk = lax.dot_general(q, k, (((1,), (1,)), ((), ())), preferred_element_type=F32)
    s_qk = (s_qk * jnp.exp(d - m_t)).astype(BF16)
    state = c_ref[...]
    both = w_inter * _dot(q, state.astype(BF16)) + _dot(s_qk, v_ext)
    inv = 1.0 / jnp.maximum(jnp.abs(both[:, e:]), jnp.exp(-m_t))

    a_r = b_last - bcum_r + ig_r
    a_c = b_last - bcum_c + ig_c
    m_new = jnp.maximum(b_last + m_prev, jnp.max(a_r, axis=1, keepdims=True))
    decay = jnp.exp(b_last + m_prev - m_new)
    kw = k * jnp.exp(a_c - m_new).astype(BF16)
    c_ref[...] = decay * state + lax.dot_general(
        kw, v_ext, (((0,), (0,)), ((), ())), preferred_element_type=F32)
    m_ref[...] = m_new
    return both[:, :e], inv


def _mlstm_kernel(q_ref, k_ref, v_ref, gc_ref, gr_ref, og_ref, xc_ref, z_ref, skip_ref,
                  o_ref, c_ref, m_ref):
    @pl.when(pl.program_id(1) == 0)
    def _():
        c_ref[...] = jnp.zeros_like(c_ref)
        m_ref[...] = jnp.zeros_like(m_ref)

    e = HEAD_DIM_C
    nh = N_HEADS_C
    gcol = gc_ref[0]
    grow = gr_ref[0]
    for h in range(nh):
        sl = slice(h * e, (h + 1) * e)
        num, inv = _mlstm_head(q_ref[0, :, sl], k_ref[0, :, sl], v_ref[0, :, sl],
                               gcol[:, h:h + 1], gcol[:, nh + h:nh + h + 1],
                               grow[h:h + 1, :], grow[nh + h:nh + h + 1, :],
                               c_ref.at[h], m_ref.at[h])
        msq = jnp.mean(num * num, axis=1, keepdims=True)
        f = inv * lax.rsqrt(inv * inv * msq + RMS_EPS)
        hn = (num * jnp.concatenate([f] * (e // LANES), axis=1) * og_ref[:, sl]).astype(BF16)
        y = hn + skip_ref[:, sl].astype(BF16) * xc_ref[0, :, sl]
        o_ref[0, :, sl] = y * _silu(z_ref[0, :, sl])


def _mlstm(q, k, v, gates, gates_t, out_gain, xc, z, skip):
    b, s, c = q.shape
    chunk = MLSTM_CHUNK
    e = HEAD_DIM_C
    rows = pl.BlockSpec((1, chunk, c), lambda bi, ci: (bi, ci, 0))
    return pl.pallas_call(
        _mlstm_kernel,
        grid=(b, s // chunk),
        in_specs=[rows, rows, rows,
                  pl.BlockSpec((1, chunk, LANES), lambda bi, ci: (bi, ci, 0)),
                  pl.BlockSpec((1, 2 * N_HEADS_C, chunk), lambda bi, ci: (bi, 0, ci)),
                  _resident((1, c)), rows, rows, _resident((1, c))],
        out_specs=rows,
        out_shape=jax.ShapeDtypeStruct((b, s, c), BF16),
        scratch_shapes=[pltpu.VMEM((N_HEADS_C, e, e + LANES), F32),
                        pltpu.VMEM((N_HEADS_C, 1, 1), F32)],
        compiler_params=_params("parallel", "arbitrary"),
        name="mlstm_chunkwise",
    )(q, k, v, gates, gates_t, out_gain.reshape(1, c), xc, z, skip.reshape(1, c))


def kernel(x, rel_bias, norm_gains, ffn_w_gate, ffn_w_up, ffn_w_down, ev_w_in, ev_q_gain, ev_k_gain, ev_pool_w, ev_pool_scale, ev_w_out, od_w_in, od_conv_w, od_conv_b, od_wq, od_wk, od_wv, od_w_gates, od_b_gates, od_skip, od_out_gain, od_w_out):
    b, s, d = x.shape
    t = b * s
    x = x.reshape(t, d)
    bias_rows = _attn_bias_rows(rel_bias, s, ATTN_TILE)
    ffn_w = (ffn_w_gate.astype(BF16), ffn_w_up.astype(BF16), ffn_w_down.astype(BF16))
    ev_w_in, ev_w_out = ev_w_in.astype(BF16), ev_w_out.astype(BF16)
    od_w_in, od_w_out = od_w_in.astype(BF16), od_w_out.astype(BF16)
    for layer in range(DEPTH):
        g = norm_gains[layer]
        i = layer // 2
        x = _ffn(x, g[0], *ffn_w, layer, 0)
        if layer % 2 == 0:
            q, k, v, pooled = _proj_even(x.reshape(b, s, d), g[1], ev_w_in, i, ev_q_gain[i],
                                         ev_k_gain[i], ev_pool_w[i], ev_pool_scale[i])
            attn = _attention(q, k, v, bias_rows)
            mix, w_out = (attn.reshape(t, WIDTH_A), pooled.reshape(t, WIDTH_B)), ev_w_out
        else:
            xc, q, k, v, z, gates, gates_t = _odd_in(
                x.reshape(b, s, d), g[1], od_w_in, i, od_conv_w[i], od_conv_b[i], od_wq[i],
                od_wk[i], od_wv[i], od_w_gates[i], od_b_gates[i])
            y = _mlstm(q, k, v, gates, gates_t, od_out_gain[i], xc, z, od_skip[i])
            mix, w_out = (y.reshape(t, INNER_C),), od_w_out
        x = _ffn(x, g[2], *ffn_w, layer, 1, mix, w_out, i)
    return x.reshape(b, s, d)
```

```python
import functools
import math

import numpy as np
import jax
import jax.numpy as jnp
from jax import lax
from jax.experimental import pallas as pl
from jax.experimental.pallas import tpu as pltpu

F32 = jnp.float32
BF16 = jnp.bfloat16

D_MODEL = 1024
DEPTH = 4
N_HEADS_A = 8
HEAD_DIM_A = 64
WIDTH_A = N_HEADS_A * HEAD_DIM_A
DILATED_PATTERNS = ((128, 1), (512, 4), (2048, 16))
N_POOL_GROUPS = 4
POOL_WINDOWS = (2, 4, 8, 16)
WIDTH_B = D_MODEL // 2
POOL_GROUP_DIM = WIDTH_B // N_POOL_GROUPS
IN_WIDTH_EVEN = 3 * WIDTH_A + WIDTH_B
REL_BUCKETS = 32
REL_MAX_DISTANCE = 2048
N_HEADS_C = 4
INNER_C = 2 * D_MODEL
HEAD_DIM_C = INNER_C // N_HEADS_C
CONV_WIDTH = 4
QKV_BLOCK = 4
D_FF = 256 * ((8 * D_MODEL // 3 + 255) // 256)
FFN_RESIDUAL = 0.5
RMS_EPS = 1e-6

LANES = 128
SUBLANES = 8
VMEM_LIMIT = 56 * 1024 * 1024
MASKED = -1e30

TOKEN_TILE = 512
EVEN_IN_TILE = 1024
ATTN_TILE = 256
MLSTM_CHUNK = 256
ODD_CHANNEL_BLOCK = 256


def _params(*sem):
    return pltpu.CompilerParams(dimension_semantics=sem, vmem_limit_bytes=VMEM_LIMIT)


def _resident(shape):
    nd = len(shape)
    return pl.BlockSpec(shape, lambda *_: (0,) * nd, pipeline_mode=pl.Buffered(1))


def _stacked(shape, *lead):
    nd = len(shape)
    return pl.BlockSpec((None,) * len(lead) + tuple(shape), lambda *_: tuple(lead) + (0,) * nd,
                        pipeline_mode=pl.Buffered(1))


def _rms(x, gain):
    return x * lax.rsqrt(jnp.mean(x * x, axis=-1, keepdims=True) + RMS_EPS) * gain


def _silu(x):
    return x / (1.0 + jnp.exp(-x))


def _dot(a, b):
    return jnp.dot(a, b, preferred_element_type=F32)


FFN_STAGE_ROWS = 256


def _stage_ffn_weights(sources, stage_ref, sem_ref):
    chunks = []
    for w_hbm, w16_ref in sources:
        rows, cols = w16_ref.shape
        for r0 in range(0, rows, FFN_STAGE_ROWS):
            chunks.append((w_hbm, w16_ref, r0, cols))

    def copy(i):
        w_hbm, _, r0, cols = chunks[i]
        return pltpu.make_async_copy(w_hbm.at[pl.ds(r0, FFN_STAGE_ROWS), :],
                                     stage_ref.at[i % 2, :, :cols], sem_ref.at[i % 2])

    copy(0).start()
    for i, (_, w16_ref, r0, cols) in enumerate(chunks):
        if i + 1 < len(chunks):
            copy(i + 1).start()
        copy(i).wait()
        w16_ref[r0:r0 + FFN_STAGE_ROWS, :] = stage_ref[i % 2, :, :cols].astype(BF16)


def _ffn_kernel(*refs, n_mix, layer, half):
    x_ref, mix_refs, rest = refs[0], refs[1:1 + n_mix], refs[1 + n_mix:]
    if n_mix:
        wo_ref, rest = rest[0], rest[1:]
    g_ref, wg_hbm, wu_hbm, wd_hbm, o_ref, wg_ref, wu_ref, wd_ref, stage_ref, sem_ref = rest

    @pl.when(pl.program_id(0) == 0)
    def _():
        _stage_ffn_weights([(wg_hbm.at[layer, half], wg_ref), (wu_hbm.at[layer, half], wu_ref),
                            (wd_hbm.at[layer, half], wd_ref)], stage_ref, sem_ref)

    tm = x_ref.shape[0]
    parts = [slice(p * tm // 2, (p + 1) * tm // 2) for p in range(2)]
    xs = []
    for rs in parts:
        x = x_ref[rs, :]
        row = 0
        for m_ref in mix_refs:
            width = m_ref.shape[1]
            x = x + _dot(m_ref[rs, :], wo_ref[row:row + width, :])
            row += width
        xs.append(x)
    xn = [_rms(x, g_ref[...]).astype(BF16) for x in xs]
    h = [(_silu(_dot(a, wg_ref[...])) * _dot(a, wu_ref[...])).astype(BF16) for a in xn]
    for rs, x, hh in zip(parts, xs, h):
        o_ref[rs, :] = x + FFN_RESIDUAL * _dot(hh, wd_ref[...])


def _ffn(x, gain, w_gate, w_up, w_down, layer, half, mix=(), w_out=None, w_out_index=None):
    t = x.shape[0]
    tm = TOKEN_TILE
    row = lambda n: pl.BlockSpec((tm, n), lambda i: (i, 0))
    hbm = pl.BlockSpec(memory_space=pl.ANY)
    mix_specs = [row(m.shape[1]) for m in mix]
    mix_args = list(mix)
    if mix:
        mix_specs.append(_stacked(w_out.shape[1:], w_out_index))
        mix_args.append(w_out)
    return pl.pallas_call(
        functools.partial(_ffn_kernel, n_mix=len(mix), layer=layer, half=half),
        grid=(t // tm,),
        in_specs=[row(D_MODEL), *mix_specs, _resident((1, D_MODEL)), hbm, hbm, hbm],
        out_specs=row(D_MODEL),
        out_shape=jax.ShapeDtypeStruct((t, D_MODEL), F32),
        scratch_shapes=[pltpu.VMEM((D_MODEL, D_FF), BF16), pltpu.VMEM((D_MODEL, D_FF), BF16),
                        pltpu.VMEM((D_FF, D_MODEL), BF16),
                        pltpu.VMEM((2, FFN_STAGE_ROWS, D_FF), F32),
                        pltpu.SemaphoreType.DMA((2,))],
        compiler_params=_params("arbitrary"),
        name="ffn",
    )(x, *mix_args, gain.reshape(1, D_MODEL), w_gate, w_up, w_down)


NORM_SLAB = 2 * LANES


def _head_sum_matrix():
    r = lax.broadcasted_iota(jnp.int32, (NORM_SLAB, NORM_SLAB), 0) // HEAD_DIM_A
    c = lax.broadcasted_iota(jnp.int32, (NORM_SLAB, NORM_SLAB), 1) // HEAD_DIM_A
    return jnp.where(r == c, 1.0, 0.0).astype(BF16)


def _head_norm(t, same_head, gain, scale):
    ss = _dot((t * t).astype(BF16), same_head)
    return t * lax.rsqrt(ss * (1.0 / HEAD_DIM_A) + RMS_EPS) * (gain * scale)


POOL_HALO = 16


def _pool_diffs(u, tail_ref, first_pos):
    ts = u.shape[0]
    pos = first_pos + lax.broadcasted_iota(jnp.int32, (ts, POOL_GROUP_DIM), 0)
    diffs = []
    for gi, window in enumerate(POOL_WINDOWS):
        gs = slice(gi * POOL_GROUP_DIM, (gi + 1) * POOL_GROUP_DIM)
        acc = jnp.concatenate([tail_ref[:, gs], u[:, gs]], axis=0)
        span = 1
        while span < window:
            acc = acc + pltpu.roll(acc, span, 0)
            span *= 2
        count = jnp.minimum(pos, window).astype(F32)
        diffs.append(acc[POOL_HALO:] / count - u[:, gs])
    tail_ref[...] = u[ts - POOL_HALO:, :]
    return jnp.concatenate(diffs, axis=1)


def _proj_even_kernel(x_ref, g_ref, w_ref, qg_ref, kg_ref, wp_ref, ps_ref,
                      q_ref, k_ref, v_ref, p_ref, tail_ref):
    si = pl.program_id(1)
    ts = x_ref.shape[1]

    @pl.when(si == 0)
    def _():
        tail_ref[...] = jnp.zeros_like(tail_ref)

    xn = _rms(x_ref[0], g_ref[...]).astype(BF16)
    u = _dot(xn, w_ref[:, 3 * WIDTH_A:])
    diff = _pool_diffs(u, tail_ref, si * ts + 1).astype(BF16)
    same_head = _head_sum_matrix()
    q = _dot(xn, w_ref[:, :WIDTH_A])
    k = _dot(xn, w_ref[:, WIDTH_A:2 * WIDTH_A])
    v_ref[0] = _dot(xn, w_ref[:, 2 * WIDTH_A:3 * WIDTH_A]).astype(BF16)
    for j in range(WIDTH_A // NORM_SLAB):
        sl = slice(j * NORM_SLAB, (j + 1) * NORM_SLAB)
        q_ref[0, :, sl] = _head_norm(q[:, sl], same_head, qg_ref[...],
                                     HEAD_DIM_A ** -0.5).astype(BF16)
        k_ref[0, :, sl] = _head_norm(k[:, sl], same_head, kg_ref[...], 1.0).astype(BF16)
    for j in range(WIDTH_B // NORM_SLAB):
        sl = slice(j * NORM_SLAB, (j + 1) * NORM_SLAB)
        p_ref[0, :, sl] = (_dot(diff[:, sl], wp_ref[j]) * ps_ref[:, sl]).astype(BF16)


def _proj_even(x, gain, w_in, i, q_gain, k_gain, pool_w, pool_scale):
    b, s, _ = x.shape
    ts = EVEN_IN_TILE
    assert POOL_WINDOWS == (2, 4, 8, 16) and max(POOL_WINDOWS) <= POOL_HALO
    rows = lambda n: pl.BlockSpec((1, ts, n), lambda bi, si: (bi, si, 0))
    pair = lambda g: jnp.tile(g, NORM_SLAB // HEAD_DIM_A).reshape(1, NORM_SLAB)
    zero = jnp.zeros_like(pool_w[0])
    wp = jnp.stack([jnp.block([[pool_w[2 * j], zero], [zero, pool_w[2 * j + 1]]])
                    for j in range(N_POOL_GROUPS // 2)])
    act = jax.ShapeDtypeStruct((b, s, WIDTH_A), BF16)
    return pl.pallas_call(
        _proj_even_kernel,
        grid=(b, s // ts),
        in_specs=[rows(D_MODEL), _resident((1, D_MODEL)), _stacked((D_MODEL, IN_WIDTH_EVEN), i),
                  _resident((1, NORM_SLAB)), _resident((1, NORM_SLAB)),
                  _resident((N_POOL_GROUPS // 2, NORM_SLAB, NORM_SLAB)), _resident((1, WIDTH_B))],
        out_specs=[rows(WIDTH_A), rows(WIDTH_A), rows(WIDTH_A), rows(WIDTH_B)],
        out_shape=[act, act, act, jax.ShapeDtypeStruct((b, s, WIDTH_B), BF16)],
        scratch_shapes=[pltpu.VMEM((POOL_HALO, WIDTH_B), F32)],
        compiler_params=_params("parallel", "arbitrary"),
        name="proj_even",
    )(x, gain.reshape(1, D_MODEL), w_in, pair(q_gain), pair(k_gain), wp.astype(BF16),
      pool_scale.reshape(1, WIDTH_B))


def _t5_bucket(distance):
    max_exact = REL_BUCKETS // 2
    d = jnp.maximum(distance.astype(F32), 1.0)
    large = max_exact + (jnp.log(d / max_exact) / math.log(REL_MAX_DISTANCE / max_exact)
                         * (REL_BUCKETS - max_exact)).astype(jnp.int32)
    large = jnp.minimum(large, REL_BUCKETS - 1)
    return jnp.where(distance < max_exact, distance, large)


def _attn_bias_rows(rel_bias, seq, tile):
    dist = np.arange(seq)
    mult = np.zeros(seq, np.int64)
    for window, dilation in DILATED_PATTERNS:
        mult += (dist % dilation == 0) & (dist <= window)
    log_mult = np.where(mult > 0, np.log(np.maximum(mult, 1)), MASKED).astype(np.float32)
    per_dist = rel_bias[_t5_bucket(jnp.asarray(dist, jnp.int32))].astype(F32)
    per_dist = jnp.where(jnp.asarray(mult > 0)[:, None], per_dist + log_mult[:, None], MASKED)
    per_dist = per_dist.T
    base = seq - tile
    heads = per_dist.shape[0]
    masked = jnp.full((heads, tile), MASKED, F32)
    v = jnp.concatenate([per_dist[:, base:], masked, per_dist[:, :base]], axis=1)
    w = jnp.roll(v[:, ::-1], 1, axis=1)
    return w.reshape(heads // 2, 2, 1, seq + tile)


def _attn_kernel(q_ref, k_ref, v_ref, w_ref, o_ref, tab_ref, vext_ref, *, tile):
    seq = q_ref.shape[1]
    nq = seq // tile

    vext_ref[:, :LANES] = v_ref[0]
    vext_ref[:, LANES:] = jnp.ones((seq, LANES), BF16)

    @pl.when(pl.program_id(1) == 0)
    def _():
        for hh in range(2):
            rows = jnp.broadcast_to(w_ref[0, hh], (tile, seq + tile))
            tab_ref[hh] = pltpu.roll(rows, 0, 1, stride=1, stride_axis=0)[:, :seq]

    first = lax.broadcasted_iota(jnp.int32, (tile, LANES), 1) < HEAD_DIM_A
    for qi in reversed(range(nq)):
        n = (qi + 1) * tile
        off = (nq - 1 - qi) * tile
        q = q_ref[0, qi * tile:(qi + 1) * tile, :]
        zero = jnp.zeros_like(q)
        q2 = jnp.concatenate([jnp.where(first, q, zero), jnp.where(first, zero, q)], axis=0)
        s = lax.dot_general(q2, k_ref[0, :n, :], (((1,), (1,)), ((), ())),
                            preferred_element_type=F32)
        s = s + jnp.concatenate([tab_ref[0, :, off:off + n], tab_ref[1, :, off:off + n]], axis=0)
        p = jnp.exp(s - jnp.max(s, axis=-1, keepdims=True))
        o = _dot(p.astype(BF16), vext_ref[:n, :])
        o = o[:, :LANES] / o[:, LANES:]
        o_ref[0, qi * tile:(qi + 1) * tile, :] = jnp.where(first, o[:tile], o[tile:]).astype(BF16)


def _attention(q, k, v, bias_rows):
    b, s, _ = q.shape
    tile = ATTN_TILE
    seq_spec = pl.BlockSpec((1, s, LANES), lambda h, bi: (bi, 0, h))
    return pl.pallas_call(
        functools.partial(_attn_kernel, tile=tile),
        grid=(WIDTH_A // LANES, b),
        in_specs=[seq_spec, seq_spec, seq_spec,
                  pl.BlockSpec((1, 2, 1, s + tile), lambda h, bi: (h, 0, 0, 0))],
        out_specs=seq_spec,
        out_shape=jax.ShapeDtypeStruct((b, s, WIDTH_A), BF16),
        scratch_shapes=[pltpu.VMEM((2, tile, s), F32), pltpu.VMEM((s, 2 * LANES), BF16)],
        compiler_params=_params("arbitrary", "arbitrary"),
        name="dilated_attention",
    )(q, k, v, bias_rows)


def _shift_rows_halo(a, tail, back, row8):
    halo = tail.shape[0]
    rolled = pltpu.roll(a, back, 0)
    head = jnp.where(row8 < back, pltpu.roll(tail, back, 0), rolled[:halo])
    return jnp.concatenate([head, rolled[halo:]], axis=0)


def _odd_in_kernel(x_ref, g_ref, w_ref, cw_ref, cb_ref, wmix_ref, bg_ref,
                   xc_ref, q_ref, k_ref, v_ref, z_ref, gates_ref, gates_t_ref, tail_ref):
    ts = x_ref.shape[1]
    halo = tail_ref.shape[0]
    cbw = ODD_CHANNEL_BLOCK
    row8 = lax.broadcasted_iota(jnp.int32, (halo, cbw), 0)

    @pl.when(pl.program_id(1) == 0)
    def _():
        tail_ref[...] = jnp.zeros_like(tail_ref)

    xn = _rms(x_ref[0], g_ref[...]).astype(BF16)
    gates = jnp.broadcast_to(bg_ref[...], (ts, LANES))
    n_blocks = INNER_C // cbw
    xm_next = _dot(xn, w_ref[:, :cbw])
    for cb in range(n_blocks):
        cs = slice(cb * cbw, (cb + 1) * cbw)
        xm = xm_next
        if cb + 1 < n_blocks:
            xm_next = _dot(xn, w_ref[:, (cb + 1) * cbw:(cb + 2) * cbw])
        tail = tail_ref[:, cs]
        tail_ref[:, cs] = xm[ts - halo:, :]
        w0, w1, w2, w3 = (cw_ref[tap:tap + 1, cs] for tap in range(CONV_WIDTH))
        xm_1 = _shift_rows_halo(xm, tail, 1, row8)
        far = xm * w1 + xm_1 * w0
        far_tail = tail * w1 + pltpu.roll(tail, 1, 0) * w0
        conv = xm * w3 + xm_1 * w2 + _shift_rows_halo(far, far_tail, 2, row8)
        xc16 = _silu(conv + cb_ref[:, cs]).astype(BF16)
        xm16 = xm.astype(BF16)
        xc_ref[0, :, cs] = xc16
        for gl in range(cbw // LANES):
            ls = slice(gl * LANES, (gl + 1) * LANES)
            gi = cb * (cbw // LANES) + gl
            sl = slice(gi * LANES, (gi + 1) * LANES)
            out = _dot(jnp.concatenate([xc16[:, ls], xm16[:, ls]], axis=1), wmix_ref[gi])
            q_ref[0, :, sl] = out[:, :LANES].astype(BF16)
            k_ref[0, :, sl] = out[:, LANES:2 * LANES].astype(BF16)
            v_ref[0, :, sl] = out[:, 2 * LANES:3 * LANES].astype(BF16)
            gates = gates + out[:, 3 * LANES:]
    gates_ref[0] = gates
    gates_t_ref[0] = gates.T
    z_ref[0] = _dot(xn, w_ref[:, INNER_C:]).astype(BF16)


def _block_diag_tiles(w):
    per_tile = LANES // QKV_BLOCK
    w = w.reshape(-1, per_tile, QKV_BLOCK, QKV_BLOCK)
    eye = jnp.eye(per_tile, dtype=w.dtype)
    dense = jnp.einsum("tgio,gh->tgiho", w, eye)
    return dense.reshape(-1, LANES, LANES)


def _odd_in(x, gain, w_in, i, conv_w, conv_b, wq, wk, wv, w_gates, b_gates):
    b, s, _ = x.shape
    c = INNER_C
    ts = TOKEN_TILE
    n_gates = 2 * N_HEADS_C
    wq_t, wk_t, wv_t = _block_diag_tiles(wq), _block_diag_tiles(wk), _block_diag_tiles(wv)
    wg = w_gates.reshape(3, c // LANES, LANES, n_gates)
    compose = lambda w_t, g: jnp.einsum("tij,tjn->tin", w_t, g, precision=lax.Precision.HIGHEST)
    pad_gates = lambda g: jnp.pad(g, ((0, 0), (0, 0), (0, LANES - n_gates)))
    zeros = jnp.zeros_like(wq_t)
    wmix = jnp.concatenate([
        jnp.concatenate([wq_t, wk_t * HEAD_DIM_C ** -0.5, zeros,
                         pad_gates(compose(wq_t, wg[0]) + compose(wk_t, wg[1]))], axis=2),
        jnp.concatenate([zeros, zeros, wv_t, pad_gates(compose(wv_t, wg[2]))], axis=2)], axis=1)
    bg = jnp.pad(b_gates.reshape(1, n_gates), ((0, 0), (0, LANES - n_gates)))
    rows = lambda n: pl.BlockSpec((1, ts, n), lambda bi, si: (bi, si, 0))
    act = jax.ShapeDtypeStruct((b, s, c), BF16)
    return pl.pallas_call(
        _odd_in_kernel,
        grid=(b, s // ts),
        in_specs=[rows(D_MODEL), _resident((1, D_MODEL)), _stacked((D_MODEL, 2 * c), i),
                  _resident((CONV_WIDTH, c)), _resident((1, c)),
                  _resident((c // LANES, 2 * LANES, 4 * LANES)), _resident((1, LANES))],
        out_specs=[rows(c), rows(c), rows(c), rows(c), rows(c), rows(LANES),
                   pl.BlockSpec((1, LANES, ts), lambda bi, si: (bi, 0, si))],
        out_shape=[act, act, act, act, act, jax.ShapeDtypeStruct((b, s, LANES), F32),
                   jax.ShapeDtypeStruct((b, LANES, s), F32)],
        scratch_shapes=[pltpu.VMEM((SUBLANES, c), F32)],
        compiler_params=_params("parallel", "arbitrary"),
        name="odd_in",
    )(x, gain.reshape(1, D_MODEL), w_in, conv_w, conv_b.reshape(1, c), wmix.astype(BF16), bg)


def _log_sigmoid(x):
    return jnp.minimum(x, 0.0) - jnp.log(1.0 + jnp.exp(-jnp.abs(x)))


def _mlstm_head(q, k, v, ig_c, fg_c, ig_r, fg_r, c_ref, m_ref):
    chunk, e = q.shape
    v_ext = jnp.concatenate([v, jnp.ones((chunk, LANES), BF16)], axis=1)
    lf_c = _log_sigmoid(fg_c)
    lf_r = _log_sigmoid(fg_r)
    t_idx = lax.broadcasted_iota(jnp.int32, (chunk, chunk), 0)
    s_idx = lax.broadcasted_iota(jnp.int32, (chunk, chunk), 1)
    causal = s_idx <= t_idx
    bcum_c = jnp.sum(jnp.where(causal, lf_r, 0.0), axis=1, keepdims=True)
    bcum_r = jnp.sum(jnp.where(t_idx <= s_idx, lf_c, 0.0), axis=0, keepdims=True)
    b_last = jnp.sum(lf_r, axis=1, keepdims=True)
    d = jnp.where(causal, bcum_c - bcum_r + ig_r, MASKED)
    m_prev = m_ref[...]
    inter = bcum_c + m_prev
    m_t = jnp.maximum(inter, jnp.max(d, axis=1, keepdims=True))
    w_inter = jnp.exp(inter - m_t)
    s_qk = lax.dot_general(q, k, (((1,), (1,)), ((), ())), preferred_element_type=F32)
    s_qk = (s_qk * jnp.exp(d - m_t)).astype(BF16)
    state = c_ref[...]
    both = w_inter * _dot(q, state.astype(BF16)) + _dot(s_qk, v_ext)
    inv = 1.0 / jnp.maximum(jnp.abs(both[:, e:]), jnp.exp(-m_t))

    a_r = b_last - bcum_r + ig_r
    a_c = b_last - bcum_c + ig_c
    m_new = jnp.maximum(b_last + m_prev, jnp.max(a_r, axis=1, keepdims=True))
    decay = jnp.exp(b_last + m_prev - m_new)
    kw = k * jnp.exp(a_c - m_new).astype(BF16)
    c_ref[...] = decay * state + lax.dot_general(
        kw, v_ext, (((0,), (0,)), ((), ())), preferred_element_type=F32)
    m_ref[...] = m_new
    return both[:, :e], inv


def _mlstm_kernel(q_ref, k_ref, v_ref, gc_ref, gr_ref, og_ref, xc_ref, z_ref, skip_ref,
                  o_ref, c_ref, m_ref):
    @pl.when(pl.program_id(1) == 0)
    def _():
        c_ref[...] = jnp.zeros_like(c_ref)
        m_ref[...] = jnp.zeros_like(m_ref)

    e = HEAD_DIM_C
    nh = N_HEADS_C
    gcol = gc_ref[0]
    grow = gr_ref[0]
    for h in range(nh):
        sl = slice(h * e, (h + 1) * e)
        num, inv = _mlstm_head(q_ref[0, :, sl], k_ref[0, :, sl], v_ref[0, :, sl],
                               gcol[:, h:h + 1], gcol[:, nh + h:nh + h + 1],
                               grow[h:h + 1, :], grow[nh + h:nh + h + 1, :],
                               c_ref.at[h], m_ref.at[h])
        msq = jnp.mean(num * num, axis=1, keepdims=True)
        f = inv * lax.rsqrt(inv * inv * msq + RMS_EPS)
        hn = (num * jnp.concatenate([f] * (e // LANES), axis=1) * og_ref[:, sl]).astype(BF16)
        y = hn + skip_ref[:, sl].astype(BF16) * xc_ref[0, :, sl]
        o_ref[0, :, sl] = y * _silu(z_ref[0, :, sl])


def _mlstm(q, k, v, gates, gates_t, out_gain, xc, z, skip):
    b, s, c = q.shape
    chunk = MLSTM_CHUNK
    e = HEAD_DIM_C
    rows = pl.BlockSpec((1, chunk, c), lambda bi, ci: (bi, ci, 0))
    return pl.pallas_call(
        _mlstm_kernel,
        grid=(b, s // chunk),
        in_specs=[rows, rows, rows,
                  pl.BlockSpec((1, chunk, LANES), lambda bi, ci: (bi, ci, 0)),
                  pl.BlockSpec((1, 2 * N_HEADS_C, chunk), lambda bi, ci: (bi, 0, ci)),
                  _resident((1, c)), rows, rows, _resident((1, c))],
        out_specs=rows,
        out_shape=jax.ShapeDtypeStruct((b, s, c), BF16),
        scratch_shapes=[pltpu.VMEM((N_HEADS_C, e, e + LANES), F32),
                        pltpu.VMEM((N_HEADS_C, 1, 1), F32)],
        compiler_params=_params("parallel", "arbitrary"),
        name="mlstm_chunkwise",
    )(q, k, v, gates, gates_t, out_gain.reshape(1, c), xc, z, skip.reshape(1, c))


def kernel(x, rel_bias, norm_gains, ffn_w_gate, ffn_w_up, ffn_w_down, ev_w_in, ev_q_gain, ev_k_gain, ev_pool_w, ev_pool_scale, ev_w_out, od_w_in, od_conv_w, od_conv_b, od_wq, od_wk, od_wv, od_w_gates, od_b_gates, od_skip, od_out_gain, od_w_out):
    b, s, d = x.shape
    t = b * s
    x = x.reshape(t, d)
    bias_rows = _attn_bias_rows(rel_bias, s, ATTN_TILE)
    ffn_w = (ffn_w_gate, ffn_w_up, ffn_w_down)
    ev_w_in, ev_w_out = ev_w_in.astype(BF16), ev_w_out.astype(BF16)
    od_w_in, od_w_out = od_w_in.astype(BF16), od_w_out.astype(BF16)
    for layer in range(DEPTH):
        g = norm_gains[layer]
        i = layer // 2
        x = _ffn(x, g[0], *ffn_w, layer, 0)
        if layer % 2 == 0:
            q, k, v, pooled = _proj_even(x.reshape(b, s, d), g[1], ev_w_in, i, ev_q_gain[i],
                                         ev_k_gain[i], ev_pool_w[i], ev_pool_scale[i])
            attn = _attention(q, k, v, bias_rows)
            mix, w_out = (attn.reshape(t, WIDTH_A), pooled.reshape(t, WIDTH_B)), ev_w_out
        else:
            xc, q, k, v, z, gates, gates_t = _odd_in(
                x.reshape(b, s, d), g[1], od_w_in, i, od_conv_w[i], od_conv_b[i], od_wq[i],
                od_wk[i], od_wv[i], od_w_gates[i], od_b_gates[i])
            y = _mlstm(q, k, v, gates, gates_t, od_out_gain[i], xc, z, od_skip[i])
            mix, w_out = (y.reshape(t, INNER_C),), od_w_out
        x = _ffn(x, g[2], *ffn_w, layer, 1, mix, w_out, i)
    return x.reshape(b, s, d)
```

```python
import functools
import math

import numpy as np
import jax
import jax.numpy as jnp
from jax import lax
from jax.experimental import pallas as pl
from jax.experimental.pallas import tpu as pltpu

F32 = jnp.float32
BF16 = jnp.bfloat16

D_MODEL = 1024
DEPTH = 4
N_HEADS_A = 8
HEAD_DIM_A = 64
WIDTH_A = N_HEADS_A * HEAD_DIM_A
DILATED_PATTERNS = ((128, 1), (512, 4), (2048, 16))
N_POOL_GROUPS = 4
POOL_WINDOWS = (2, 4, 8, 16)
WIDTH_B = D_MODEL // 2
POOL_GROUP_DIM = WIDTH_B // N_POOL_GROUPS
IN_WIDTH_EVEN = 3 * WIDTH_A + WIDTH_B
REL_BUCKETS = 32
REL_MAX_DISTANCE = 2048
N_HEADS_C = 4
INNER_C = 2 * D_MODEL
HEAD_DIM_C = INNER_C // N_HEADS_C
CONV_WIDTH = 4
QKV_BLOCK = 4
D_FF = 256 * ((8 * D_MODEL // 3 + 255) // 256)
FFN_RESIDUAL = 0.5
RMS_EPS = 1e-6

LANES = 128
SUBLANES = 8
VMEM_LIMIT = 56 * 1024 * 1024
MASKED = -1e30

TOKEN_TILE = 512
EVEN_IN_TILE = 1024
ATTN_TILE = 256
MLSTM_CHUNK = 256
ODD_CHANNEL_BLOCK = 256


def _params(*sem):
    return pltpu.CompilerParams(dimension_semantics=sem, vmem_limit_bytes=VMEM_LIMIT)


def _resident(shape):
    nd = len(shape)
    return pl.BlockSpec(shape, lambda *_: (0,) * nd, pipeline_mode=pl.Buffered(1))


def _stacked(shape, *lead):
    nd = len(shape)
    return pl.BlockSpec((None,) * len(lead) + tuple(shape), lambda *_: tuple(lead) + (0,) * nd,
                        pipeline_mode=pl.Buffered(1))


def _rms(x, gain):
    return x * lax.rsqrt(jnp.mean(x * x, axis=-1, keepdims=True) + RMS_EPS) * gain


def _silu(x):
    return x / (1.0 + jnp.exp(-x))


def _dot(a, b):
    return jnp.dot(a, b, preferred_element_type=F32)


FFN_STAGE_ROWS = 128
FFN_STAGE_SLOTS = 4


def _stage_ffn_weights(sources, stage_ref, sem_ref):
    slots = stage_ref.shape[0]
    chunks = []
    for w_hbm, w16_ref in sources:
        rows, cols = w16_ref.shape
        for r0 in range(0, rows, FFN_STAGE_ROWS):
            chunks.append((w_hbm, w16_ref, r0, cols))

    def copy(i):
        w_hbm, _, r0, cols = chunks[i]
        return pltpu.make_async_copy(w_hbm.at[pl.ds(r0, FFN_STAGE_ROWS), :],
                                     stage_ref.at[i % slots, :, :cols], sem_ref.at[i % slots])

    for i in range(slots - 1):
        copy(i).start()
    for i, (_, w16_ref, r0, cols) in enumerate(chunks):
        if i + slots - 1 < len(chunks):
            copy(i + slots - 1).start()
        copy(i).wait()
        w16_ref[r0:r0 + FFN_STAGE_ROWS, :] = stage_ref[i % slots, :, :cols].astype(BF16)


def _ffn_kernel(*refs, n_mix, layer, half):
    x_ref, mix_refs, rest = refs[0], refs[1:1 + n_mix], refs[1 + n_mix:]
    if n_mix:
        wo_ref, rest = rest[0], rest[1:]
    g_ref, wg_hbm, wu_hbm, wd_hbm, o_ref, wg_ref, wu_ref, wd_ref, stage_ref, sem_ref = rest

    @pl.when(pl.program_id(0) == 0)
    def _():
        _stage_ffn_weights([(wg_hbm.at[layer, half], wg_ref), (wu_hbm.at[layer, half], wu_ref),
                            (wd_hbm.at[layer, half], wd_ref)], stage_ref, sem_ref)

    tm = x_ref.shape[0]
    parts = [slice(p * tm // 2, (p + 1) * tm // 2) for p in range(2)]
    xs = []
    for rs in parts:
        x = x_ref[rs, :]
        row = 0
        for m_ref in mix_refs:
            width = m_ref.shape[1]
            x = x + _dot(m_ref[rs, :], wo_ref[row:row + width, :])
            row += width
        xs.append(x)
    xn = [_rms(x, g_ref[...]).astype(BF16) for x in xs]
    h = [(_silu(_dot(a, wg_ref[...])) * _dot(a, wu_ref[...])).astype(BF16) for a in xn]
    for rs, x, hh in zip(parts, xs, h):
        o_ref[rs, :] = x + FFN_RESIDUAL * _dot(hh, wd_ref[...])


def _ffn(x, gain, w_gate, w_up, w_down, layer, half, mix=(), w_out=None, w_out_index=None):
    t = x.shape[0]
    tm = TOKEN_TILE
    row = lambda n: pl.BlockSpec((tm, n), lambda i: (i, 0))
    hbm = pl.BlockSpec(memory_space=pl.ANY)
    mix_specs = [row(m.shape[1]) for m in mix]
    mix_args = list(mix)
    if mix:
        mix_specs.append(_stacked(w_out.shape[1:], w_out_index))
        mix_args.append(w_out)
    return pl.pallas_call(
        functools.partial(_ffn_kernel, n_mix=len(mix), layer=layer, half=half),
        grid=(t // tm,),
        in_specs=[row(D_MODEL), *mix_specs, _resident((1, D_MODEL)), hbm, hbm, hbm],
        out_specs=row(D_MODEL),
        out_shape=jax.ShapeDtypeStruct((t, D_MODEL), F32),
        scratch_shapes=[pltpu.VMEM((D_MODEL, D_FF), BF16), pltpu.VMEM((D_MODEL, D_FF), BF16),
                        pltpu.VMEM((D_FF, D_MODEL), BF16),
                        pltpu.VMEM((FFN_STAGE_SLOTS, FFN_STAGE_ROWS, D_FF), F32),
                        pltpu.SemaphoreType.DMA((FFN_STAGE_SLOTS,))],
        compiler_params=_params("arbitrary"),
        name="ffn",
    )(x, *mix_args, gain.reshape(1, D_MODEL), w_gate, w_up, w_down)


NORM_SLAB = 2 * LANES


def _head_sum_matrix():
    r = lax.broadcasted_iota(jnp.int32, (NORM_SLAB, NORM_SLAB), 0) // HEAD_DIM_A
    c = lax.broadcasted_iota(jnp.int32, (NORM_SLAB, NORM_SLAB), 1) // HEAD_DIM_A
    return jnp.where(r == c, 1.0, 0.0).astype(BF16)


def _head_norm(t, same_head, gain, scale):
    ss = _dot((t * t).astype(BF16), same_head)
    return t * lax.rsqrt(ss * (1.0 / HEAD_DIM_A) + RMS_EPS) * (gain * scale)


POOL_HALO = 16


def _pool_diffs(u, tail_ref, first_pos):
    ts = u.shape[0]
    pos = first_pos + lax.broadcasted_iota(jnp.int32, (ts, POOL_GROUP_DIM), 0)
    diffs = []
    for gi, window in enumerate(POOL_WINDOWS):
        gs = slice(gi * POOL_GROUP_DIM, (gi + 1) * POOL_GROUP_DIM)
        acc = jnp.concatenate([tail_ref[:, gs], u[:, gs]], axis=0)
        span = 1
        while span < window:
            acc = acc + pltpu.roll(acc, span, 0)
            span *= 2
        count = jnp.minimum(pos, window).astype(F32)
        diffs.append(acc[POOL_HALO:] / count - u[:, gs])
    tail_ref[...] = u[ts - POOL_HALO:, :]
    return jnp.concatenate(diffs, axis=1)


def _proj_even_kernel(x_ref, g_ref, w_ref, qg_ref, kg_ref, wp_ref, ps_ref,
                      q_ref, k_ref, v_ref, p_ref, tail_ref):
    si = pl.program_id(1)
    ts = x_ref.shape[1]

    @pl.when(si == 0)
    def _():
        tail_ref[...] = jnp.zeros_like(tail_ref)

    xn = _rms(x_ref[0], g_ref[...]).astype(BF16)
    u = _dot(xn, w_ref[:, 3 * WIDTH_A:])
    diff = _pool_diffs(u, tail_ref, si * ts + 1).astype(BF16)
    same_head = _head_sum_matrix()
    q = _dot(xn, w_ref[:, :WIDTH_A])
    k = _dot(xn, w_ref[:, WIDTH_A:2 * WIDTH_A])
    v_ref[0] = _dot(xn, w_ref[:, 2 * WIDTH_A:3 * WIDTH_A]).astype(BF16)
    for j in range(WIDTH_A // NORM_SLAB):
        sl = slice(j * NORM_SLAB, (j + 1) * NORM_SLAB)
        q_ref[0, :, sl] = _head_norm(q[:, sl], same_head, qg_ref[...],
                                     HEAD_DIM_A ** -0.5).astype(BF16)
        k_ref[0, :, sl] = _head_norm(k[:, sl], same_head, kg_ref[...], 1.0).astype(BF16)
    for j in range(WIDTH_B // NORM_SLAB):
        sl = slice(j * NORM_SLAB, (j + 1) * NORM_SLAB)
        p_ref[0, :, sl] = (_dot(diff[:, sl], wp_ref[j]) * ps_ref[:, sl]).astype(BF16)


def _proj_even(x, gain, w_in, i, q_gain, k_gain, pool_w, pool_scale):
    b, s, _ = x.shape
    ts = EVEN_IN_TILE
    assert POOL_WINDOWS == (2, 4, 8, 16) and max(POOL_WINDOWS) <= POOL_HALO
    rows = lambda n: pl.BlockSpec((1, ts, n), lambda bi, si: (bi, si, 0))
    pair = lambda g: jnp.tile(g, NORM_SLAB // HEAD_DIM_A).reshape(1, NORM_SLAB)
    zero = jnp.zeros_like(pool_w[0])
    wp = jnp.stack([jnp.block([[pool_w[2 * j], zero], [zero, pool_w[2 * j + 1]]])
                    for j in range(N_POOL_GROUPS // 2)])
    act = jax.ShapeDtypeStruct((b, s, WIDTH_A), BF16)
    return pl.pallas_call(
        _proj_even_kernel,
        grid=(b, s // ts),
        in_specs=[rows(D_MODEL), _resident((1, D_MODEL)), _stacked((D_MODEL, IN_WIDTH_EVEN), i),
                  _resident((1, NORM_SLAB)), _resident((1, NORM_SLAB)),
                  _resident((N_POOL_GROUPS // 2, NORM_SLAB, NORM_SLAB)), _resident((1, WIDTH_B))],
        out_specs=[rows(WIDTH_A), rows(WIDTH_A), rows(WIDTH_A), rows(WIDTH_B)],
        out_shape=[act, act, act, jax.ShapeDtypeStruct((b, s, WIDTH_B), BF16)],
        scratch_shapes=[pltpu.VMEM((POOL_HALO, WIDTH_B), F32)],
        compiler_params=_params("parallel", "arbitrary"),
        name="proj_even",
    )(x, gain.reshape(1, D_MODEL), w_in, pair(q_gain), pair(k_gain), wp.astype(BF16),
      pool_scale.reshape(1, WIDTH_B))


def _t5_bucket(distance):
    max_exact = REL_BUCKETS // 2
    d = jnp.maximum(distance.astype(F32), 1.0)
    large = max_exact + (jnp.log(d / max_exact) / math.log(REL_MAX_DISTANCE / max_exact)
                         * (REL_BUCKETS - max_exact)).astype(jnp.int32)
    large = jnp.minimum(large, REL_BUCKETS - 1)
    return jnp.where(distance < max_exact, distance, large)


def _attn_bias_rows(rel_bias, seq, tile):
    dist = np.arange(seq)
    mult = np.zeros(seq, np.int64)
    for window, dilation in DILATED_PATTERNS:
        mult += (dist % dilation == 0) & (dist <= window)
    log_mult = np.where(mult > 0, np.log(np.maximum(mult, 1)), MASKED).astype(np.float32)
    per_dist = rel_bias[_t5_bucket(jnp.asarray(dist, jnp.int32))].astype(F32)
    per_dist = jnp.where(jnp.asarray(mult > 0)[:, None], per_dist + log_mult[:, None], MASKED)
    per_dist = per_dist.T
    base = seq - tile
    heads = per_dist.shape[0]
    masked = jnp.full((heads, tile), MASKED, F32)
    v = jnp.concatenate([per_dist[:, base:], masked, per_dist[:, :base]], axis=1)
    w = jnp.roll(v[:, ::-1], 1, axis=1)
    return w.reshape(heads // 2, 2, 1, seq + tile)


def _attn_kernel(q_ref, k_ref, v_ref, w_ref, o_ref, tab_ref, vext_ref, *, tile):
    seq = q_ref.shape[1]
    nq = seq // tile

    vext_ref[:, :LANES] = v_ref[0]
    vext_ref[:, LANES:] = jnp.ones((seq, LANES), BF16)

    @pl.when(pl.program_id(1) == 0)
    def _():
        for hh in range(2):
            rows = jnp.broadcast_to(w_ref[0, hh], (tile, seq + tile))
            tab_ref[hh] = pltpu.roll(rows, 0, 1, stride=1, stride_axis=0)[:, :seq]

    first = lax.broadcasted_iota(jnp.int32, (tile, LANES), 1) < HEAD_DIM_A
    for qi in reversed(range(nq)):
        n = (qi + 1) * tile
        off = (nq - 1 - qi) * tile
        q = q_ref[0, qi * tile:(qi + 1) * tile, :]
        zero = jnp.zeros_like(q)
        q2 = jnp.concatenate([jnp.where(first, q, zero), jnp.where(first, zero, q)], axis=0)
        s = lax.dot_general(q2, k_ref[0, :n, :], (((1,), (1,)), ((), ())),
                            preferred_element_type=F32)
        s = s + jnp.concatenate([tab_ref[0, :, off:off + n], tab_ref[1, :, off:off + n]], axis=0)
        p = jnp.exp(s - jnp.max(s, axis=-1, keepdims=True))
        o = _dot(p.astype(BF16), vext_ref[:n, :])
        o = o[:, :LANES] / o[:, LANES:]
        o_ref[0, qi * tile:(qi + 1) * tile, :] = jnp.where(first, o[:tile], o[tile:]).astype(BF16)


def _attention(q, k, v, bias_rows):
    b, s, _ = q.shape
    tile = ATTN_TILE
    seq_spec = pl.BlockSpec((1, s, LANES), lambda h, bi: (bi, 0, h))
    return pl.pallas_call(
        functools.partial(_attn_kernel, tile=tile),
        grid=(WIDTH_A // LANES, b),
        in_specs=[seq_spec, seq_spec, seq_spec,
                  pl.BlockSpec((1, 2, 1, s + tile), lambda h, bi: (h, 0, 0, 0))],
        out_specs=seq_spec,
        out_shape=jax.ShapeDtypeStruct((b, s, WIDTH_A), BF16),
        scratch_shapes=[pltpu.VMEM((2, tile, s), F32), pltpu.VMEM((s, 2 * LANES), BF16)],
        compiler_params=_params("arbitrary", "arbitrary"),
        name="dilated_attention",
    )(q, k, v, bias_rows)


def _shift_rows_halo(a, tail, back, row8):
    halo = tail.shape[0]
    rolled = pltpu.roll(a, back, 0)
    head = jnp.where(row8 < back, pltpu.roll(tail, back, 0), rolled[:halo])
    return jnp.concatenate([head, rolled[halo:]], axis=0)


def _odd_in_kernel(x_ref, g_ref, w_ref, cw_ref, cb_ref, wmix_ref, bg_ref,
                   xc_ref, q_ref, k_ref, v_ref, z_ref, gates_ref, gates_t_ref, tail_ref):
    ts = x_ref.shape[1]
    halo = tail_ref.shape[0]
    cbw = ODD_CHANNEL_BLOCK
    row8 = lax.broadcasted_iota(jnp.int32, (halo, cbw), 0)

    @pl.when(pl.program_id(1) == 0)
    def _():
        tail_ref[...] = jnp.zeros_like(tail_ref)

    xn = _rms(x_ref[0], g_ref[...]).astype(BF16)
    gates = jnp.broadcast_to(bg_ref[...], (ts, LANES))
    n_blocks = INNER_C // cbw
    xm_next = _dot(xn, w_ref[:, :cbw])
    for cb in range(n_blocks):
        cs = slice(cb * cbw, (cb + 1) * cbw)
        xm = xm_next
        if cb + 1 < n_blocks:
            xm_next = _dot(xn, w_ref[:, (cb + 1) * cbw:(cb + 2) * cbw])
        tail = tail_ref[:, cs]
        tail_ref[:, cs] = xm[ts - halo:, :]
        w0, w1, w2, w3 = (cw_ref[tap:tap + 1, cs] for tap in range(CONV_WIDTH))
        xm_1 = _shift_rows_halo(xm, tail, 1, row8)
        far = xm * w1 + xm_1 * w0
        far_tail = tail * w1 + pltpu.roll(tail, 1, 0) * w0
        conv = xm * w3 + xm_1 * w2 + _shift_rows_halo(far, far_tail, 2, row8)
        xc16 = _silu(conv + cb_ref[:, cs]).astype(BF16)
        xm16 = xm.astype(BF16)
        xc_ref[0, :, cs] = xc16
        for gl in range(cbw // LANES):
            ls = slice(gl * LANES, (gl + 1) * LANES)
            gi = cb * (cbw // LANES) + gl
            sl = slice(gi * LANES, (gi + 1) * LANES)
            out = _dot(jnp.concatenate([xc16[:, ls], xm16[:, ls]], axis=1), wmix_ref[gi])
            q_ref[0, :, sl] = out[:, :LANES].astype(BF16)
            k_ref[0, :, sl] = out[:, LANES:2 * LANES].astype(BF16)
            v_ref[0, :, sl] = out[:, 2 * LANES:3 * LANES].astype(BF16)
            gates = gates + out[:, 3 * LANES:]
    gates_ref[0] = gates
    gates_t_ref[0] = gates.T
    z_ref[0] = _dot(xn, w_ref[:, INNER_C:]).astype(BF16)


def _block_diag_tiles(w):
    per_tile = LANES // QKV_BLOCK
    w = w.reshape(-1, per_tile, QKV_BLOCK, QKV_BLOCK)
    eye = jnp.eye(per_tile, dtype=w.dtype)
    dense = jnp.einsum("tgio,gh->tgiho", w, eye)
    return dense.reshape(-1, LANES, LANES)


def _odd_in(x, gain, w_in, i, conv_w, conv_b, wq, wk, wv, w_gates, b_gates):
    b, s, _ = x.shape
    c = INNER_C
    ts = TOKEN_TILE
    n_gates = 2 * N_HEADS_C
    wq_t, wk_t, wv_t = _block_diag_tiles(wq), _block_diag_tiles(wk), _block_diag_tiles(wv)
    wg = w_gates.reshape(3, c // LANES, LANES, n_gates)
    compose = lambda w_t, g: jnp.einsum("tij,tjn->tin", w_t, g, precision=lax.Precision.HIGHEST)
    pad_gates = lambda g: jnp.pad(g, ((0, 0), (0, 0), (0, LANES - n_gates)))
    zeros = jnp.zeros_like(wq_t)
    wmix = jnp.concatenate([
        jnp.concatenate([wq_t, wk_t * HEAD_DIM_C ** -0.5, zeros,
                         pad_gates(compose(wq_t, wg[0]) + compose(wk_t, wg[1]))], axis=2),
        jnp.concatenate([zeros, zeros, wv_t, pad_gates(compose(wv_t, wg[2]))], axis=2)], axis=1)
    bg = jnp.pad(b_gates.reshape(1, n_gates), ((0, 0), (0, LANES - n_gates)))
    rows = lambda n: pl.BlockSpec((1, ts, n), lambda bi, si: (bi, si, 0))
    act = jax.ShapeDtypeStruct((b, s, c), BF16)
    return pl.pallas_call(
        _odd_in_kernel,
        grid=(b, s // ts),
        in_specs=[rows(D_MODEL), _resident((1, D_MODEL)), _stacked((D_MODEL, 2 * c), i),
                  _resident((CONV_WIDTH, c)), _resident((1, c)),
                  _resident((c // LANES, 2 * LANES, 4 * LANES)), _resident((1, LANES))],
        out_specs=[rows(c), rows(c), rows(c), rows(c), rows(c), rows(LANES),
                   pl.BlockSpec((1, LANES, ts), lambda bi, si: (bi, 0, si))],
        out_shape=[act, act, act, act, act, jax.ShapeDtypeStruct((b, s, LANES), F32),
                   jax.ShapeDtypeStruct((b, LANES, s), F32)],
        scratch_shapes=[pltpu.VMEM((SUBLANES, c), F32)],
        compiler_params=_params("parallel", "arbitrary"),
        name="odd_in",
    )(x, gain.reshape(1, D_MODEL), w_in, conv_w, conv_b.reshape(1, c), wmix.astype(BF16), bg)


def _log_sigmoid(x):
    return jnp.minimum(x, 0.0) - jnp.log(1.0 + jnp.exp(-jnp.abs(x)))


def _mlstm_head(q, k, v, ig_c, fg_c, ig_r, fg_r, c_ref, m_ref):
    chunk, e = q.shape
    v_ext = jnp.concatenate([v, jnp.ones((chunk, LANES), BF16)], axis=1)
    lf_c = _log_sigmoid(fg_c)
    lf_r = _log_sigmoid(fg_r)
    t_idx = lax.broadcasted_iota(jnp.int32, (chunk, chunk), 0)
    s_idx = lax.broadcasted_iota(jnp.int32, (chunk, chunk), 1)
    causal = s_idx <= t_idx
    bcum_c = jnp.sum(jnp.where(causal, lf_r, 0.0), axis=1, keepdims=True)
    bcum_r = jnp.sum(jnp.where(t_idx <= s_idx, lf_c, 0.0), axis=0, keepdims=True)
    b_last = jnp.sum(lf_r, axis=1, keepdims=True)
    d = jnp.where(causal, bcum_c - bcum_r + ig_r, MASKED)
    m_prev = m_ref[...]
    inter = bcum_c + m_prev
    m_t = jnp.maximum(inter, jnp.max(d, axis=1, keepdims=True))
    w_inter = jnp.exp(inter - m_t)
    s_qk = lax.dot_general(q, k, (((1,), (1,)), ((), ())), preferred_element_type=F32)
    s_qk = (s_qk * jnp.exp(d - m_t)).astype(BF16)
    state = c_ref[...]
    both = w_inter * _dot(q, state.astype(BF16)) + _dot(s_qk, v_ext)
    inv = 1.0 / jnp.maximum(jnp.abs(both[:, e:]), jnp.exp(-m_t))

    a_r = b_last - bcum_r + ig_r
    a_c = b_last - bcum_c + ig_c
    m_new = jnp.maximum(b_last + m_prev, jnp.max(a_r, axis=1, keepdims=True))
    decay = jnp.exp(b_last + m_prev - m_new)
    kw = k * jnp.exp(a_c - m_new).astype(BF16)
    c_ref[...] = decay * state + lax.dot_general(
        kw, v_ext, (((0,), (0,)), ((), ())), preferred_element_type=F32)
    m_ref[...] = m_new
    return both[:, :e], inv


def _mlstm_kernel(q_ref, k_ref, v_ref, gc_ref, gr_ref, og_ref, xc_ref, z_ref, skip_ref,
                  o_ref, c_ref, m_ref):
    @pl.when(pl.program_id(1) == 0)
    def _():
        c_ref[...] = jnp.zeros_like(c_ref)
        m_ref[...] = jnp.zeros_like(m_ref)

    e = HEAD_DIM_C
    nh = N_HEADS_C
    gcol = gc_ref[0]
    grow = gr_ref[0]
    for h in range(nh):
        sl = slice(h * e, (h + 1) * e)
        num, inv = _mlstm_head(q_ref[0, :, sl], k_ref[0, :, sl], v_ref[0, :, sl],
                               gcol[:, h:h + 1], gcol[:, nh + h:nh + h + 1],
                               grow[h:h + 1, :], grow[nh + h:nh + h + 1, :],
                               c_ref.at[h], m_ref.at[h])
        msq = jnp.mean(num * num, axis=1, keepdims=True)
        f = inv * lax.rsqrt(inv * inv * msq + RMS_EPS)
        hn = (num * jnp.concatenate([f] * (e // LANES), axis=1) * og_ref[:, sl]).astype(BF16)
        y = hn + skip_ref[:, sl].astype(BF16) * xc_ref[0, :, sl]
        o_ref[0, :, sl] = y * _silu(z_ref[0, :, sl])


def _mlstm(q, k, v, gates, gates_t, out_gain, xc, z, skip):
    b, s, c = q.shape
    chunk = MLSTM_CHUNK
    e = HEAD_DIM_C
    rows = pl.BlockSpec((1, chunk, c), lambda bi, ci: (bi, ci, 0))
    return pl.pallas_call(
        _mlstm_kernel,
        grid=(b, s // chunk),
        in_specs=[rows, rows, rows,
                  pl.BlockSpec((1, chunk, LANES), lambda bi, ci: (bi, ci, 0)),
                  pl.BlockSpec((1, 2 * N_HEADS_C, chunk), lambda bi, ci: (bi, 0, ci)),
                  _resident((1, c)), rows, rows, _resident((1, c))],
        out_specs=rows,
        out_shape=jax.ShapeDtypeStruct((b, s, c), BF16),
        scratch_shapes=[pltpu.VMEM((N_HEADS_C, e, e + LANES), F32),
                        pltpu.VMEM((N_HEADS_C, 1, 1), F32)],
        compiler_params=_params("parallel", "arbitrary"),
        name="mlstm_chunkwise",
    )(q, k, v, gates, gates_t, out_gain.reshape(1, c), xc, z, skip.reshape(1, c))


def kernel(x, rel_bias, norm_gains, ffn_w_gate, ffn_w_up, ffn_w_down, ev_w_in, ev_q_gain, ev_k_gain, ev_pool_w, ev_pool_scale, ev_w_out, od_w_in, od_conv_w, od_conv_b, od_wq, od_wk, od_wv, od_w_gates, od_b_gates, od_skip, od_out_gain, od_w_out):
    b, s, d = x.shape
    t = b * s
    x = x.reshape(t, d)
    bias_rows = _attn_bias_rows(rel_bias, s, ATTN_TILE)
    ffn_w = (ffn_w_gate, ffn_w_up, ffn_w_down)
    ev_w_in, ev_w_out = ev_w_in.astype(BF16), ev_w_out.astype(BF16)
    od_w_in, od_w_out = od_w_in.astype(BF16), od_w_out.astype(BF16)
    for layer in range(DEPTH):
        g = norm_gains[layer]
        i = layer // 2
        x = _ffn(x, g[0], *ffn_w, layer, 0)
        if layer % 2 == 0:
            q, k, v, pooled = _proj_even(x.reshape(b, s, d), g[1], ev_w_in, i, ev_q_gain[i],
                                         ev_k_gain[i], ev_pool_w[i], ev_pool_scale[i])
            attn = _attention(q, k, v, bias_rows)
            mix, w_out = (attn.reshape(t, WIDTH_A), pooled.reshape(t, WIDTH_B)), ev_w_out
        else:
            xc, q, k, v, z, gates, gates_t = _odd_in(
                x.reshape(b, s, d), g[1], od_w_in, i, od_conv_w[i], od_conv_b[i], od_wq[i],
                od_wk[i], od_wv[i], od_w_gates[i], od_b_gates[i])
            y = _mlstm(q, k, v, gates, gates_t, od_out_gain[i], xc, z, od_skip[i])
            mix, w_out = (y.reshape(t, INNER_C),), od_w_out
        x = _ffn(x, g[2], *ffn_w, layer, 1, mix, w_out, i)
    return x.reshape(b, s, d)
```

```python
import functools
import math

import numpy as np
import jax
import jax.numpy as jnp
from jax import lax
from jax.experimental import pallas as pl
from jax.experimental.pallas import tpu as pltpu

F32 = jnp.float32
BF16 = jnp.bfloat16

D_MODEL = 1024
DEPTH = 4
N_HEADS_A = 8
HEAD_DIM_A = 64
WIDTH_A = N_HEADS_A * HEAD_DIM_A
DILATED_PATTERNS = ((128, 1), (512, 4), (2048, 16))
N_POOL_GROUPS = 4
POOL_WINDOWS = (2, 4, 8, 16)
WIDTH_B = D_MODEL // 2
POOL_GROUP_DIM = WIDTH_B // N_POOL_GROUPS
IN_WIDTH_EVEN = 3 * WIDTH_A + WIDTH_B
REL_BUCKETS = 32
REL_MAX_DISTANCE = 2048
N_HEADS_C = 4
INNER_C = 2 * D_MODEL
HEAD_DIM_C = INNER_C // N_HEADS_C
CONV_WIDTH = 4
QKV_BLOCK = 4
D_FF = 256 * ((8 * D_MODEL // 3 + 255) // 256)
FFN_RESIDUAL = 0.5
RMS_EPS = 1e-6

LANES = 128
SUBLANES = 8
VMEM_LIMIT = 56 * 1024 * 1024
MASKED = -1e30

TOKEN_TILE = 512
EVEN_IN_TILE = 1024
ATTN_TILE = 256
MLSTM_CHUNK = 256
ODD_CHANNEL_BLOCK = 256


def _params(*sem):
    return pltpu.CompilerParams(dimension_semantics=sem, vmem_limit_bytes=VMEM_LIMIT)


def _resident(shape):
    nd = len(shape)
    return pl.BlockSpec(shape, lambda *_: (0,) * nd, pipeline_mode=pl.Buffered(1))


def _stacked(shape, *lead):
    nd = len(shape)
    return pl.BlockSpec((None,) * len(lead) + tuple(shape), lambda *_: tuple(lead) + (0,) * nd,
                        pipeline_mode=pl.Buffered(1))


def _rms(x, gain):
    return x * lax.rsqrt(jnp.mean(x * x, axis=-1, keepdims=True) + RMS_EPS) * gain


def _silu(x):
    return x / (1.0 + jnp.exp(-x))


def _dot(a, b):
    return jnp.dot(a, b, preferred_element_type=F32)


FFN_STAGE_ROWS = 128
FFN_STAGE_SLOTS = 6


def _stage_ffn_weights(sources, stage_ref, sem_ref):
    slots = stage_ref.shape[0]
    chunks = []
    for w_hbm, w16_ref in sources:
        rows, cols = w16_ref.shape
        for r0 in range(0, rows, FFN_STAGE_ROWS):
            chunks.append((w_hbm, w16_ref, r0, cols))

    def copy(i):
        w_hbm, _, r0, cols = chunks[i]
        return pltpu.make_async_copy(w_hbm.at[pl.ds(r0, FFN_STAGE_ROWS), :],
                                     stage_ref.at[i % slots, :, :cols], sem_ref.at[i % slots])

    for i in range(slots - 1):
        copy(i).start()
    for i, (_, w16_ref, r0, cols) in enumerate(chunks):
        if i + slots - 1 < len(chunks):
            copy(i + slots - 1).start()
        copy(i).wait()
        w16_ref[r0:r0 + FFN_STAGE_ROWS, :] = stage_ref[i % slots, :, :cols].astype(BF16)


def _ffn_kernel(*refs, n_mix, layer, half):
    x_ref, mix_refs, rest = refs[0], refs[1:1 + n_mix], refs[1 + n_mix:]
    if n_mix:
        wo_ref, rest = rest[0], rest[1:]
    g_ref, wg_hbm, wu_hbm, wd_hbm, o_ref, wg_ref, wu_ref, wd_ref, stage_ref, sem_ref = rest

    @pl.when(pl.program_id(0) == 0)
    def _():
        _stage_ffn_weights([(wg_hbm.at[layer, half], wg_ref), (wu_hbm.at[layer, half], wu_ref),
                            (wd_hbm.at[layer, half], wd_ref)], stage_ref, sem_ref)

    tm = x_ref.shape[0]
    parts = [slice(p * tm // 2, (p + 1) * tm // 2) for p in range(2)]
    xs = []
    for rs in parts:
        x = x_ref[rs, :]
        row = 0
        for m_ref in mix_refs:
            width = m_ref.shape[1]
            x = x + _dot(m_ref[rs, :], wo_ref[row:row + width, :])
            row += width
        xs.append(x)
    xn = [_rms(x, g_ref[...]).astype(BF16) for x in xs]
    h = [(_silu(_dot(a, wg_ref[...])) * _dot(a, wu_ref[...])).astype(BF16) for a in xn]
    for rs, x, hh in zip(parts, xs, h):
        o_ref[rs, :] = x + FFN_RESIDUAL * _dot(hh, wd_ref[...])


def _ffn(x, gain, w_gate, w_up, w_down, layer, half, mix=(), w_out=None, w_out_index=None):
    t = x.shape[0]
    tm = TOKEN_TILE
    row = lambda n: pl.BlockSpec((tm, n), lambda i: (i, 0))
    hbm = pl.BlockSpec(memory_space=pl.ANY)
    mix_specs = [row(m.shape[1]) for m in mix]
    mix_args = list(mix)
    if mix:
        mix_specs.append(_stacked(w_out.shape[1:], w_out_index))
        mix_args.append(w_out)
    return pl.pallas_call(
        functools.partial(_ffn_kernel, n_mix=len(mix), layer=layer, half=half),
        grid=(t // tm,),
        in_specs=[row(D_MODEL), *mix_specs, _resident((1, D_MODEL)), hbm, hbm, hbm],
        out_specs=row(D_MODEL),
        out_shape=jax.ShapeDtypeStruct((t, D_MODEL), F32),
        scratch_shapes=[pltpu.VMEM((D_MODEL, D_FF), BF16), pltpu.VMEM((D_MODEL, D_FF), BF16),
                        pltpu.VMEM((D_FF, D_MODEL), BF16),
                        pltpu.VMEM((FFN_STAGE_SLOTS, FFN_STAGE_ROWS, D_FF), F32),
                        pltpu.SemaphoreType.DMA((FFN_STAGE_SLOTS,))],
        compiler_params=_params("arbitrary"),
        name="ffn",
    )(x, *mix_args, gain.reshape(1, D_MODEL), w_gate, w_up, w_down)


NORM_SLAB = 2 * LANES


def _head_sum_matrix():
    r = lax.broadcasted_iota(jnp.int32, (NORM_SLAB, NORM_SLAB), 0) // HEAD_DIM_A
    c = lax.broadcasted_iota(jnp.int32, (NORM_SLAB, NORM_SLAB), 1) // HEAD_DIM_A
    return jnp.where(r == c, 1.0, 0.0).astype(BF16)


def _head_norm(t, same_head, gain, scale):
    ss = _dot((t * t).astype(BF16), same_head)
    return t * lax.rsqrt(ss * (1.0 / HEAD_DIM_A) + RMS_EPS) * (gain * scale)


POOL_HALO = 16


def _pool_diffs(u, tail_ref, first_pos):
    ts = u.shape[0]
    pos = first_pos + lax.broadcasted_iota(jnp.int32, (ts, POOL_GROUP_DIM), 0)
    diffs = []
    for gi, window in enumerate(POOL_WINDOWS):
        gs = slice(gi * POOL_GROUP_DIM, (gi + 1) * POOL_GROUP_DIM)
        acc = jnp.concatenate([tail_ref[:, gs], u[:, gs]], axis=0)
        span = 1
        while span < window:
            acc = acc + pltpu.roll(acc, span, 0)
            span *= 2
        count = jnp.minimum(pos, window).astype(F32)
        diffs.append(acc[POOL_HALO:] / count - u[:, gs])
    tail_ref[...] = u[ts - POOL_HALO:, :]
    return jnp.concatenate(diffs, axis=1)


def _proj_even_kernel(x_ref, g_ref, w_ref, qg_ref, kg_ref, wp_ref, ps_ref,
                      q_ref, k_ref, v_ref, p_ref, tail_ref):
    si = pl.program_id(1)
    ts = x_ref.shape[1]

    @pl.when(si == 0)
    def _():
        tail_ref[...] = jnp.zeros_like(tail_ref)

    xn = _rms(x_ref[0], g_ref[...]).astype(BF16)
    u = _dot(xn, w_ref[:, 3 * WIDTH_A:])
    diff = _pool_diffs(u, tail_ref, si * ts + 1).astype(BF16)
    same_head = _head_sum_matrix()
    q = _dot(xn, w_ref[:, :WIDTH_A])
    k = _dot(xn, w_ref[:, WIDTH_A:2 * WIDTH_A])
    v_ref[0] = _dot(xn, w_ref[:, 2 * WIDTH_A:3 * WIDTH_A]).astype(BF16)
    for j in range(WIDTH_A // NORM_SLAB):
        sl = slice(j * NORM_SLAB, (j + 1) * NORM_SLAB)
        q_ref[0, :, sl] = _head_norm(q[:, sl], same_head, qg_ref[...],
                                     HEAD_DIM_A ** -0.5).astype(BF16)
        k_ref[0, :, sl] = _head_norm(k[:, sl], same_head, kg_ref[...], 1.0).astype(BF16)
    for j in range(WIDTH_B // NORM_SLAB):
        sl = slice(j * NORM_SLAB, (j + 1) * NORM_SLAB)
        p_ref[0, :, sl] = (_dot(diff[:, sl], wp_ref[j]) * ps_ref[:, sl]).astype(BF16)


def _proj_even(x, gain, w_in, i, q_gain, k_gain, pool_w, pool_scale):
    b, s, _ = x.shape
    ts = EVEN_IN_TILE
    assert POOL_WINDOWS == (2, 4, 8, 16) and max(POOL_WINDOWS) <= POOL_HALO
    rows = lambda n: pl.BlockSpec((1, ts, n), lambda bi, si: (bi, si, 0))
    pair = lambda g: jnp.tile(g, NORM_SLAB // HEAD_DIM_A).reshape(1, NORM_SLAB)
    zero = jnp.zeros_like(pool_w[0])
    wp = jnp.stack([jnp.block([[pool_w[2 * j], zero], [zero, pool_w[2 * j + 1]]])
                    for j in range(N_POOL_GROUPS // 2)])
    act = jax.ShapeDtypeStruct((b, s, WIDTH_A), BF16)
    return pl.pallas_call(
        _proj_even_kernel,
        grid=(b, s // ts),
        in_specs=[rows(D_MODEL), _resident((1, D_MODEL)), _stacked((D_MODEL, IN_WIDTH_EVEN), i),
                  _resident((1, NORM_SLAB)), _resident((1, NORM_SLAB)),
                  _resident((N_POOL_GROUPS // 2, NORM_SLAB, NORM_SLAB)), _resident((1, WIDTH_B))],
        out_specs=[rows(WIDTH_A), rows(WIDTH_A), rows(WIDTH_A), rows(WIDTH_B)],
        out_shape=[act, act, act, jax.ShapeDtypeStruct((b, s, WIDTH_B), BF16)],
        scratch_shapes=[pltpu.VMEM((POOL_HALO, WIDTH_B), F32)],
        compiler_params=_params("parallel", "arbitrary"),
        name="proj_even",
    )(x, gain.reshape(1, D_MODEL), w_in, pair(q_gain), pair(k_gain), wp.astype(BF16),
      pool_scale.reshape(1, WIDTH_B))


def _t5_bucket(distance):
    max_exact = REL_BUCKETS // 2
    d = jnp.maximum(distance.astype(F32), 1.0)
    large = max_exact + (jnp.log(d / max_exact) / math.log(REL_MAX_DISTANCE / max_exact)
                         * (REL_BUCKETS - max_exact)).astype(jnp.int32)
    large = jnp.minimum(large, REL_BUCKETS - 1)
    return jnp.where(distance < max_exact, distance, large)


def _attn_bias_rows(rel_bias, seq, tile):
    dist = np.arange(seq)
    mult = np.zeros(seq, np.int64)
    for window, dilation in DILATED_PATTERNS:
        mult += (dist % dilation == 0) & (dist <= window)
    log_mult = np.where(mult > 0, np.log(np.maximum(mult, 1)), MASKED).astype(np.float32)
    per_dist = rel_bias[_t5_bucket(jnp.asarray(dist, jnp.int32))].astype(F32)
    per_dist = jnp.where(jnp.asarray(mult > 0)[:, None], per_dist + log_mult[:, None], MASKED)
    per_dist = per_dist.T
    base = seq - tile
    heads = per_dist.shape[0]
    masked = jnp.full((heads, tile), MASKED, F32)
    v = jnp.concatenate([per_dist[:, base:], masked, per_dist[:, :base]], axis=1)
    w = jnp.roll(v[:, ::-1], 1, axis=1)
    return w.reshape(heads // 2, 2, 1, seq + tile)


def _attn_kernel(q_ref, k_ref, v_ref, w_ref, o_ref, tab_ref, vext_ref, *, tile):
    seq = q_ref.shape[1]
    nq = seq // tile

    vext_ref[:, :LANES] = v_ref[0]
    vext_ref[:, LANES:] = jnp.ones((seq, LANES), BF16)

    @pl.when(pl.program_id(1) == 0)
    def _():
        for hh in range(2):
            rows = jnp.broadcast_to(w_ref[0, hh], (tile, seq + tile))
            tab_ref[hh] = pltpu.roll(rows, 0, 1, stride=1, stride_axis=0)[:, :seq]

    first = lax.broadcasted_iota(jnp.int32, (tile, LANES), 1) < HEAD_DIM_A
    for qi in reversed(range(nq)):
        n = (qi + 1) * tile
        off = (nq - 1 - qi) * tile
        q = q_ref[0, qi * tile:(qi + 1) * tile, :]
        zero = jnp.zeros_like(q)
        q2 = jnp.concatenate([jnp.where(first, q, zero), jnp.where(first, zero, q)], axis=0)
        s = lax.dot_general(q2, k_ref[0, :n, :], (((1,), (1,)), ((), ())),
                            preferred_element_type=F32)
        s = s + jnp.concatenate([tab_ref[0, :, off:off + n], tab_ref[1, :, off:off + n]], axis=0)
        p = jnp.exp(s - jnp.max(s, axis=-1, keepdims=True))
        o = _dot(p.astype(BF16), vext_ref[:n, :])
        o = o[:, :LANES] / o[:, LANES:]
        o_ref[0, qi * tile:(qi + 1) * tile, :] = jnp.where(first, o[:tile], o[tile:]).astype(BF16)


def _attention(q, k, v, bias_rows):
    b, s, _ = q.shape
    tile = ATTN_TILE
    seq_spec = pl.BlockSpec((1, s, LANES), lambda h, bi: (bi, 0, h))
    return pl.pallas_call(
        functools.partial(_attn_kernel, tile=tile),
        grid=(WIDTH_A // LANES, b),
        in_specs=[seq_spec, seq_spec, seq_spec,
                  pl.BlockSpec((1, 2, 1, s + tile), lambda h, bi: (h, 0, 0, 0))],
        out_specs=seq_spec,
        out_shape=jax.ShapeDtypeStruct((b, s, WIDTH_A), BF16),
        scratch_shapes=[pltpu.VMEM((2, tile, s), F32), pltpu.VMEM((s, 2 * LANES), BF16)],
        compiler_params=_params("arbitrary", "arbitrary"),
        name="dilated_attention",
    )(q, k, v, bias_rows)


def _shift_rows_halo(a, tail, back, row8):
    halo = tail.shape[0]
    rolled = pltpu.roll(a, back, 0)
    head = jnp.where(row8 < back, pltpu.roll(tail, back, 0), rolled[:halo])
    return jnp.concatenate([head, rolled[halo:]], axis=0)


def _odd_in_kernel(x_ref, g_ref, w_ref, cw_ref, cb_ref, wmix_ref, bg_ref,
                   xc_ref, q_ref, k_ref, v_ref, z_ref, gates_ref, gates_t_ref, tail_ref):
    ts = x_ref.shape[1]
    halo = tail_ref.shape[0]
    cbw = ODD_CHANNEL_BLOCK
    row8 = lax.broadcasted_iota(jnp.int32, (halo, cbw), 0)

    @pl.when(pl.program_id(1) == 0)
    def _():
        tail_ref[...] = jnp.zeros_like(tail_ref)

    xn = _rms(x_ref[0], g_ref[...]).astype(BF16)
    gates = jnp.broadcast_to(bg_ref[...], (ts, LANES))
    n_blocks = INNER_C // cbw
    xm_next = _dot(xn, w_ref[:, :cbw])
    for cb in range(n_blocks):
        cs = slice(cb * cbw, (cb + 1) * cbw)
        xm = xm_next
        if cb + 1 < n_blocks:
            xm_next = _dot(xn, w_ref[:, (cb + 1) * cbw:(cb + 2) * cbw])
        tail = tail_ref[:, cs]
        tail_ref[:, cs] = xm[ts - halo:, :]
        w0, w1, w2, w3 = (cw_ref[tap:tap + 1, cs] for tap in range(CONV_WIDTH))
        xm_1 = _shift_rows_halo(xm, tail, 1, row8)
        far = xm * w1 + xm_1 * w0
        far_tail = tail * w1 + pltpu.roll(tail, 1, 0) * w0
        conv = xm * w3 + xm_1 * w2 + _shift_rows_halo(far, far_tail, 2, row8)
        xc16 = _silu(conv + cb_ref[:, cs]).astype(BF16)
        xm16 = xm.astype(BF16)
        xc_ref[0, :, cs] = xc16
        for gl in range(cbw // LANES):
            ls = slice(gl * LANES, (gl + 1) * LANES)
            gi = cb * (cbw // LANES) + gl
            sl = slice(gi * LANES, (gi + 1) * LANES)
            out = _dot(jnp.concatenate([xc16[:, ls], xm16[:, ls]], axis=1), wmix_ref[gi])
            q_ref[0, :, sl] = out[:, :LANES].astype(BF16)
            k_ref[0, :, sl] = out[:, LANES:2 * LANES].astype(BF16)
            v_ref[0, :, sl] = out[:, 2 * LANES:3 * LANES].astype(BF16)
            gates = gates + out[:, 3 * LANES:]
    gates_ref[0] = gates
    gates_t_ref[0] = gates.T
    z_ref[0] = _dot(xn, w_ref[:, INNER_C:]).astype(BF16)


def _block_diag_tiles(w):
    per_tile = LANES // QKV_BLOCK
    rows = w.reshape(-1, LANES, 1, QKV_BLOCK)
    tiled = jnp.broadcast_to(rows, (rows.shape[0], LANES, per_tile, QKV_BLOCK))
    group = np.arange(LANES) // QKV_BLOCK
    on_diagonal = jnp.asarray(group[:, None] == group[None, :], w.dtype)
    return tiled.reshape(-1, LANES, LANES) * on_diagonal


def _odd_in(x, gain, w_in, i, conv_w, conv_b, wq, wk, wv, w_gates, b_gates):
    b, s, _ = x.shape
    c = INNER_C
    ts = TOKEN_TILE
    n_gates = 2 * N_HEADS_C
    wq_t, wk_t, wv_t = _block_diag_tiles(wq), _block_diag_tiles(wk), _block_diag_tiles(wv)
    wg = w_gates.reshape(3, c // LANES, LANES, n_gates)
    compose = lambda w_t, g: jnp.einsum("tij,tjn->tin", w_t, g, precision=lax.Precision.HIGHEST)
    pad_gates = lambda g: jnp.pad(g, ((0, 0), (0, 0), (0, LANES - n_gates)))
    zeros = jnp.zeros_like(wq_t)
    wmix = jnp.concatenate([
        jnp.concatenate([wq_t, wk_t * HEAD_DIM_C ** -0.5, zeros,
                         pad_gates(compose(wq_t, wg[0]) + compose(wk_t, wg[1]))], axis=2),
        jnp.concatenate([zeros, zeros, wv_t, pad_gates(compose(wv_t, wg[2]))], axis=2)], axis=1)
    bg = jnp.pad(b_gates.reshape(1, n_gates), ((0, 0), (0, LANES - n_gates)))
    rows = lambda n: pl.BlockSpec((1, ts, n), lambda bi, si: (bi, si, 0))
    act = jax.ShapeDtypeStruct((b, s, c), BF16)
    return pl.pallas_call(
        _odd_in_kernel,
        grid=(b, s // ts),
        in_specs=[rows(D_MODEL), _resident((1, D_MODEL)), _stacked((D_MODEL, 2 * c), i),
                  _resident((CONV_WIDTH, c)), _resident((1, c)),
                  _resident((c // LANES, 2 * LANES, 4 * LANES)), _resident((1, LANES))],
        out_specs=[rows(c), rows(c), rows(c), rows(c), rows(c), rows(LANES),
                   pl.BlockSpec((1, LANES, ts), lambda bi, si: (bi, 0, si))],
        out_shape=[act, act, act, act, act, jax.ShapeDtypeStruct((b, s, LANES), F32),
                   jax.ShapeDtypeStruct((b, LANES, s), F32)],
        scratch_shapes=[pltpu.VMEM((SUBLANES, c), F32)],
        compiler_params=_params("parallel", "arbitrary"),
        name="odd_in",
    )(x, gain.reshape(1, D_MODEL), w_in, conv_w, conv_b.reshape(1, c), wmix.astype(BF16), bg)


def _log_sigmoid(x):
    return jnp.minimum(x, 0.0) - jnp.log(1.0 + jnp.exp(-jnp.abs(x)))


def _mlstm_head(q, k, v, ig_c, fg_c, ig_r, fg_r, c_ref, m_ref):
    chunk, e = q.shape
    v_ext = jnp.concatenate([v, jnp.ones((chunk, LANES), BF16)], axis=1)
    lf_c = _log_sigmoid(fg_c)
    lf_r = _log_sigmoid(fg_r)
    t_idx = lax.broadcasted_iota(jnp.int32, (chunk, chunk), 0)
    s_idx = lax.broadcasted_iota(jnp.int32, (chunk, chunk), 1)
    causal = s_idx <= t_idx
    bcum_c = jnp.sum(jnp.where(causal, lf_r, 0.0), axis=1, keepdims=True)
    bcum_r = jnp.sum(jnp.where(t_idx <= s_idx, lf_c, 0.0), axis=0, keepdims=True)
    b_last = jnp.sum(lf_r, axis=1, keepdims=True)
    d = jnp.where(causal, bcum_c - bcum_r + ig_r, MASKED)
    m_prev = m_ref[...]
    inter = bcum_c + m_prev
    m_t = jnp.maximum(inter, jnp.max(d, axis=1, keepdims=True))
    w_inter = jnp.exp(inter - m_t)
    s_qk = lax.dot_general(q, k, (((1,), (1,)), ((), ())), preferred_element_type=F32)
    s_qk = (s_qk * jnp.exp(d - m_t)).astype(BF16)
    state = c_ref[...]
    both = w_inter * _dot(q, state.astype(BF16)) + _dot(s_qk, v_ext)
    inv = 1.0 / jnp.maximum(jnp.abs(both[:, e:]), jnp.exp(-m_t))

    a_r = b_last - bcum_r + ig_r
    a_c = b_last - bcum_c + ig_c
    m_new = jnp.maximum(b_last + m_prev, jnp.max(a_r, axis=1, keepdims=True))
    decay = jnp.exp(b_last + m_prev - m_new)
    kw = k * jnp.exp(a_c - m_new).astype(BF16)
    c_ref[...] = decay * state + lax.dot_general(
        kw, v_ext, (((0,), (0,)), ((), ())), preferred_element_type=F32)
    m_ref[...] = m_new
    return both[:, :e], inv


def _mlstm_kernel(q_ref, k_ref, v_ref, gc_ref, gr_ref, og_ref, xc_ref, z_ref, skip_ref,
                  o_ref, c_ref, m_ref):
    @pl.when(pl.program_id(1) == 0)
    def _():
        c_ref[...] = jnp.zeros_like(c_ref)
        m_ref[...] = jnp.zeros_like(m_ref)

    e = HEAD_DIM_C
    nh = N_HEADS_C
    gcol = gc_ref[0]
    grow = gr_ref[0]
    for h in range(nh):
        sl = slice(h * e, (h + 1) * e)
        num, inv = _mlstm_head(q_ref[0, :, sl], k_ref[0, :, sl], v_ref[0, :, sl],
                               gcol[:, h:h + 1], gcol[:, nh + h:nh + h + 1],
                               grow[h:h + 1, :], grow[nh + h:nh + h + 1, :],
                               c_ref.at[h], m_ref.at[h])
        msq = jnp.mean(num * num, axis=1, keepdims=True)
        f = inv * lax.rsqrt(inv * inv * msq + RMS_EPS)
        hn = (num * jnp.concatenate([f] * (e // LANES), axis=1) * og_ref[:, sl]).astype(BF16)
        y = hn + skip_ref[:, sl].astype(BF16) * xc_ref[0, :, sl]
        o_ref[0, :, sl] = y * _silu(z_ref[0, :, sl])


def _mlstm(q, k, v, gates, gates_t, out_gain, xc, z, skip):
    b, s, c = q.shape
    chunk = MLSTM_CHUNK
    e = HEAD_DIM_C
    rows = pl.BlockSpec((1, chunk, c), lambda bi, ci: (bi, ci, 0))
    return pl.pallas_call(
        _mlstm_kernel,
        grid=(b, s // chunk),
        in_specs=[rows, rows, rows,
                  pl.BlockSpec((1, chunk, LANES), lambda bi, ci: (bi, ci, 0)),
                  pl.BlockSpec((1, 2 * N_HEADS_C, chunk), lambda bi, ci: (bi, 0, ci)),
                  _resident((1, c)), rows, rows, _resident((1, c))],
        out_specs=rows,
        out_shape=jax.ShapeDtypeStruct((b, s, c), BF16),
        scratch_shapes=[pltpu.VMEM((N_HEADS_C, e, e + LANES), F32),
                        pltpu.VMEM((N_HEADS_C, 1, 1), F32)],
        compiler_params=_params("parallel", "arbitrary"),
        name="mlstm_chunkwise",
    )(q, k, v, gates, gates_t, out_gain.reshape(1, c), xc, z, skip.reshape(1, c))


def kernel(x, rel_bias, norm_gains, ffn_w_gate, ffn_w_up, ffn_w_down, ev_w_in, ev_q_gain, ev_k_gain, ev_pool_w, ev_pool_scale, ev_w_out, od_w_in, od_conv_w, od_conv_b, od_wq, od_wk, od_wv, od_w_gates, od_b_gates, od_skip, od_out_gain, od_w_out):
    b, s, d = x.shape
    t = b * s
    x = x.reshape(t, d)
    bias_rows = _attn_bias_rows(rel_bias, s, ATTN_TILE)
    ffn_w = (ffn_w_gate, ffn_w_up, ffn_w_down)
    ev_w_in, ev_w_out = ev_w_in.astype(BF16), ev_w_out.astype(BF16)
    od_w_in, od_w_out = od_w_in.astype(BF16), od_w_out.astype(BF16)
    for layer in range(DEPTH):
        g = norm_gains[layer]
        i = layer // 2
        x = _ffn(x, g[0], *ffn_w, layer, 0)
        if layer % 2 == 0:
            q, k, v, pooled = _proj_even(x.reshape(b, s, d), g[1], ev_w_in, i, ev_q_gain[i],
                                         ev_k_gain[i], ev_pool_w[i], ev_pool_scale[i])
            attn = _attention(q, k, v, bias_rows)
            mix, w_out = (attn.reshape(t, WIDTH_A), pooled.reshape(t, WIDTH_B)), ev_w_out
        else:
            xc, q, k, v, z, gates, gates_t = _odd_in(
                x.reshape(b, s, d), g[1], od_w_in, i, od_conv_w[i], od_conv_b[i], od_wq[i],
                od_wk[i], od_wv[i], od_w_gates[i], od_b_gates[i])
            y = _mlstm(q, k, v, gates, gates_t, od_out_gain[i], xc, z, od_skip[i])
            mix, w_out = (y.reshape(t, INNER_C),), od_w_out
        x = _ffn(x, g[2], *ffn_w, layer, 1, mix, w_out, i)
    return x.reshape(b, s, d)
```

```python
import functools
import math

import numpy as np
import jax
import jax.numpy as jnp
from jax import lax
from jax.experimental import pallas as pl
from jax.experimental.pallas import tpu as pltpu

F32 = jnp.float32
BF16 = jnp.bfloat16

D_MODEL = 1024
DEPTH = 4
N_HEADS_A = 8
HEAD_DIM_A = 64
WIDTH_A = N_HEADS_A * HEAD_DIM_A
DILATED_PATTERNS = ((128, 1), (512, 4), (2048, 16))
N_POOL_GROUPS = 4
POOL_WINDOWS = (2, 4, 8, 16)
WIDTH_B = D_MODEL // 2
POOL_GROUP_DIM = WIDTH_B // N_POOL_GROUPS
IN_WIDTH_EVEN = 3 * WIDTH_A + WIDTH_B
REL_BUCKETS = 32
REL_MAX_DISTANCE = 2048
N_HEADS_C = 4
INNER_C = 2 * D_MODEL
HEAD_DIM_C = INNER_C // N_HEADS_C
CONV_WIDTH = 4
QKV_BLOCK = 4
D_FF = 256 * ((8 * D_MODEL // 3 + 255) // 256)
FFN_RESIDUAL = 0.5
RMS_EPS = 1e-6

LANES = 128
SUBLANES = 8
VMEM_LIMIT = 56 * 1024 * 1024
MASKED = -1e30

TOKEN_TILE = 512
EVEN_IN_TILE = 1024
ATTN_TILE = 256
MLSTM_CHUNK = 256
ODD_CHANNEL_BLOCK = 256


def _params(*sem):
    return pltpu.CompilerParams(dimension_semantics=sem, vmem_limit_bytes=VMEM_LIMIT)


def _resident(shape):
    nd = len(shape)
    return pl.BlockSpec(shape, lambda *_: (0,) * nd, pipeline_mode=pl.Buffered(1))


def _stacked(shape, *lead):
    nd = len(shape)
    return pl.BlockSpec((None,) * len(lead) + tuple(shape), lambda *_: tuple(lead) + (0,) * nd,
                        pipeline_mode=pl.Buffered(1))


def _rms(x, gain):
    return x * lax.rsqrt(jnp.mean(x * x, axis=-1, keepdims=True) + RMS_EPS) * gain


def _silu(x):
    return x / (1.0 + jnp.exp(-x))


def _dot(a, b):
    return jnp.dot(a, b, preferred_element_type=F32)


FFN_STAGE_ROWS = 128
FFN_STAGE_SLOTS = 6
IN_STAGE_ROWS = 64
IN_STAGE_SLOTS = 4


def _stage_weights(sources, stage_ref, sem_ref):
    slots, step, _ = stage_ref.shape
    chunks = []
    for w_hbm, w16_ref in sources:
        rows, cols = w16_ref.shape
        for r0 in range(0, rows, step):
            chunks.append((w_hbm, w16_ref, r0, cols))

    def copy(i):
        w_hbm, _, r0, cols = chunks[i]
        return pltpu.make_async_copy(w_hbm.at[pl.ds(r0, step), :],
                                     stage_ref.at[i % slots, :, :cols], sem_ref.at[i % slots])

    for i in range(slots - 1):
        copy(i).start()
    for i, (_, w16_ref, r0, cols) in enumerate(chunks):
        if i + slots - 1 < len(chunks):
            copy(i + slots - 1).start()
        copy(i).wait()
        w16_ref[r0:r0 + step, :] = stage_ref[i % slots, :, :cols].astype(BF16)


def _ffn_kernel(*refs, n_mix, layer, half):
    x_ref, mix_refs, rest = refs[0], refs[1:1 + n_mix], refs[1 + n_mix:]
    if n_mix:
        wo_ref, rest = rest[0], rest[1:]
    g_ref, wg_hbm, wu_hbm, wd_hbm, o_ref, wg_ref, wu_ref, wd_ref, stage_ref, sem_ref = rest

    @pl.when(pl.program_id(0) == 0)
    def _():
        _stage_weights([(wg_hbm.at[layer, half], wg_ref), (wu_hbm.at[layer, half], wu_ref),
                            (wd_hbm.at[layer, half], wd_ref)], stage_ref, sem_ref)

    tm = x_ref.shape[0]
    parts = [slice(p * tm // 2, (p + 1) * tm // 2) for p in range(2)]
    xs = []
    for rs in parts:
        x = x_ref[rs, :]
        row = 0
        for m_ref in mix_refs:
            width = m_ref.shape[1]
            x = x + _dot(m_ref[rs, :], wo_ref[row:row + width, :])
            row += width
        xs.append(x)
    xn = [_rms(x, g_ref[...]).astype(BF16) for x in xs]
    h = [(_silu(_dot(a, wg_ref[...])) * _dot(a, wu_ref[...])).astype(BF16) for a in xn]
    for rs, x, hh in zip(parts, xs, h):
        o_ref[rs, :] = x + FFN_RESIDUAL * _dot(hh, wd_ref[...])


def _ffn(x, gain, w_gate, w_up, w_down, layer, half, mix=(), w_out=None, w_out_index=None):
    t = x.shape[0]
    tm = TOKEN_TILE
    row = lambda n: pl.BlockSpec((tm, n), lambda i: (i, 0))
    hbm = pl.BlockSpec(memory_space=pl.ANY)
    mix_specs = [row(m.shape[1]) for m in mix]
    mix_args = list(mix)
    if mix:
        mix_specs.append(_stacked(w_out.shape[1:], w_out_index))
        mix_args.append(w_out)
    return pl.pallas_call(
        functools.partial(_ffn_kernel, n_mix=len(mix), layer=layer, half=half),
        grid=(t // tm,),
        in_specs=[row(D_MODEL), *mix_specs, _resident((1, D_MODEL)), hbm, hbm, hbm],
        out_specs=row(D_MODEL),
        out_shape=jax.ShapeDtypeStruct((t, D_MODEL), F32),
        scratch_shapes=[pltpu.VMEM((D_MODEL, D_FF), BF16), pltpu.VMEM((D_MODEL, D_FF), BF16),
                        pltpu.VMEM((D_FF, D_MODEL), BF16),
                        pltpu.VMEM((FFN_STAGE_SLOTS, FFN_STAGE_ROWS, D_FF), F32),
                        pltpu.SemaphoreType.DMA((FFN_STAGE_SLOTS,))],
        compiler_params=_params("arbitrary"),
        name="ffn",
    )(x, *mix_args, gain.reshape(1, D_MODEL), w_gate, w_up, w_down)


NORM_SLAB = 2 * LANES


def _head_sum_matrix():
    r = lax.broadcasted_iota(jnp.int32, (NORM_SLAB, NORM_SLAB), 0) // HEAD_DIM_A
    c = lax.broadcasted_iota(jnp.int32, (NORM_SLAB, NORM_SLAB), 1) // HEAD_DIM_A
    return jnp.where(r == c, 1.0, 0.0).astype(BF16)


def _head_norm(t, same_head, gain, scale):
    ss = _dot((t * t).astype(BF16), same_head)
    return t * lax.rsqrt(ss * (1.0 / HEAD_DIM_A) + RMS_EPS) * (gain * scale)


POOL_HALO = 16


def _pool_diffs(u, tail_ref, first_pos):
    ts = u.shape[0]
    pos = first_pos + lax.broadcasted_iota(jnp.int32, (ts, POOL_GROUP_DIM), 0)
    diffs = []
    for gi, window in enumerate(POOL_WINDOWS):
        gs = slice(gi * POOL_GROUP_DIM, (gi + 1) * POOL_GROUP_DIM)
        acc = jnp.concatenate([tail_ref[:, gs], u[:, gs]], axis=0)
        span = 1
        while span < window:
            acc = acc + pltpu.roll(acc, span, 0)
            span *= 2
        count = jnp.minimum(pos, window).astype(F32)
        diffs.append(acc[POOL_HALO:] / count - u[:, gs])
    tail_ref[...] = u[ts - POOL_HALO:, :]
    return jnp.concatenate(diffs, axis=1)


def _proj_even_kernel(x_ref, g_ref, w_hbm, qg_ref, kg_ref, wp_ref, ps_ref,
                      q_ref, k_ref, v_ref, p_ref, tail_ref, w_ref, stage_ref, sem_ref, *, layer):
    si = pl.program_id(1)
    ts = x_ref.shape[1]

    @pl.when((pl.program_id(0) == 0) & (si == 0))
    def _():
        _stage_weights([(w_hbm.at[layer], w_ref)], stage_ref, sem_ref)

    @pl.when(si == 0)
    def _():
        tail_ref[...] = jnp.zeros_like(tail_ref)

    xn = _rms(x_ref[0], g_ref[...]).astype(BF16)
    u = _dot(xn, w_ref[:, 3 * WIDTH_A:])
    diff = _pool_diffs(u, tail_ref, si * ts + 1).astype(BF16)
    same_head = _head_sum_matrix()
    q = _dot(xn, w_ref[:, :WIDTH_A])
    k = _dot(xn, w_ref[:, WIDTH_A:2 * WIDTH_A])
    v_ref[0] = _dot(xn, w_ref[:, 2 * WIDTH_A:3 * WIDTH_A]).astype(BF16)
    for j in range(WIDTH_A // NORM_SLAB):
        sl = slice(j * NORM_SLAB, (j + 1) * NORM_SLAB)
        q_ref[0, :, sl] = _head_norm(q[:, sl], same_head, qg_ref[...],
                                     HEAD_DIM_A ** -0.5).astype(BF16)
        k_ref[0, :, sl] = _head_norm(k[:, sl], same_head, kg_ref[...], 1.0).astype(BF16)
    for j in range(WIDTH_B // NORM_SLAB):
        sl = slice(j * NORM_SLAB, (j + 1) * NORM_SLAB)
        p_ref[0, :, sl] = (_dot(diff[:, sl], wp_ref[j]) * ps_ref[:, sl]).astype(BF16)


def _proj_even(x, gain, w_in, i, q_gain, k_gain, pool_w, pool_scale):
    b, s, _ = x.shape
    ts = EVEN_IN_TILE
    assert POOL_WINDOWS == (2, 4, 8, 16) and max(POOL_WINDOWS) <= POOL_HALO
    rows = lambda n: pl.BlockSpec((1, ts, n), lambda bi, si: (bi, si, 0))
    pair = lambda g: jnp.tile(g, NORM_SLAB // HEAD_DIM_A).reshape(1, NORM_SLAB)
    zero = jnp.zeros_like(pool_w[0])
    wp = jnp.stack([jnp.block([[pool_w[2 * j], zero], [zero, pool_w[2 * j + 1]]])
                    for j in range(N_POOL_GROUPS // 2)])
    act = jax.ShapeDtypeStruct((b, s, WIDTH_A), BF16)
    return pl.pallas_call(
        functools.partial(_proj_even_kernel, layer=i),
        grid=(b, s // ts),
        in_specs=[rows(D_MODEL), _resident((1, D_MODEL)), pl.BlockSpec(memory_space=pl.ANY),
                  _resident((1, NORM_SLAB)), _resident((1, NORM_SLAB)),
                  _resident((N_POOL_GROUPS // 2, NORM_SLAB, NORM_SLAB)), _resident((1, WIDTH_B))],
        out_specs=[rows(WIDTH_A), rows(WIDTH_A), rows(WIDTH_A), rows(WIDTH_B)],
        out_shape=[act, act, act, jax.ShapeDtypeStruct((b, s, WIDTH_B), BF16)],
        scratch_shapes=[pltpu.VMEM((POOL_HALO, WIDTH_B), F32),
                        pltpu.VMEM((D_MODEL, IN_WIDTH_EVEN), BF16),
                        pltpu.VMEM((IN_STAGE_SLOTS, IN_STAGE_ROWS, IN_WIDTH_EVEN), F32),
                        pltpu.SemaphoreType.DMA((IN_STAGE_SLOTS,))],
        compiler_params=_params("arbitrary", "arbitrary"),
        name="proj_even",
    )(x, gain.reshape(1, D_MODEL), w_in, pair(q_gain), pair(k_gain), wp.astype(BF16),
      pool_scale.reshape(1, WIDTH_B))


def _t5_bucket(distance):
    max_exact = REL_BUCKETS // 2
    d = jnp.maximum(distance.astype(F32), 1.0)
    large = max_exact + (jnp.log(d / max_exact) / math.log(REL_MAX_DISTANCE / max_exact)
                         * (REL_BUCKETS - max_exact)).astype(jnp.int32)
    large = jnp.minimum(large, REL_BUCKETS - 1)
    return jnp.where(distance < max_exact, distance, large)


def _attn_bias_rows(rel_bias, seq, tile):
    dist = np.arange(seq)
    mult = np.zeros(seq, np.int64)
    for window, dilation in DILATED_PATTERNS:
        mult += (dist % dilation == 0) & (dist <= window)
    log_mult = np.where(mult > 0, np.log(np.maximum(mult, 1)), MASKED).astype(np.float32)
    per_dist = rel_bias[_t5_bucket(jnp.asarray(dist, jnp.int32))].astype(F32)
    per_dist = jnp.where(jnp.asarray(mult > 0)[:, None], per_dist + log_mult[:, None], MASKED)
    per_dist = per_dist.T
    base = seq - tile
    heads = per_dist.shape[0]
    masked = jnp.full((heads, tile), MASKED, F32)
    v = jnp.concatenate([per_dist[:, base:], masked, per_dist[:, :base]], axis=1)
    w = jnp.roll(v[:, ::-1], 1, axis=1)
    return w.reshape(heads // 2, 2, 1, seq + tile)


def _attn_kernel(q_ref, k_ref, v_ref, w_ref, o_ref, tab_ref, vext_ref, *, tile):
    seq = q_ref.shape[1]
    nq = seq // tile

    vext_ref[:, :LANES] = v_ref[0]
    vext_ref[:, LANES:] = jnp.ones((seq, LANES), BF16)

    @pl.when(pl.program_id(1) == 0)
    def _():
        for hh in range(2):
            rows = jnp.broadcast_to(w_ref[0, hh], (tile, seq + tile))
            tab_ref[hh] = pltpu.roll(rows, 0, 1, stride=1, stride_axis=0)[:, :seq]

    first = lax.broadcasted_iota(jnp.int32, (tile, LANES), 1) < HEAD_DIM_A
    for qi in reversed(range(nq)):
        n = (qi + 1) * tile
        off = (nq - 1 - qi) * tile
        q = q_ref[0, qi * tile:(qi + 1) * tile, :]
        zero = jnp.zeros_like(q)
        q2 = jnp.concatenate([jnp.where(first, q, zero), jnp.where(first, zero, q)], axis=0)
        s = lax.dot_general(q2, k_ref[0, :n, :], (((1,), (1,)), ((), ())),
                            preferred_element_type=F32)
        s = s + jnp.concatenate([tab_ref[0, :, off:off + n], tab_ref[1, :, off:off + n]], axis=0)
        p = jnp.exp(s - jnp.max(s, axis=-1, keepdims=True))
        o = _dot(p.astype(BF16), vext_ref[:n, :])
        o = o[:, :LANES] / o[:, LANES:]
        o_ref[0, qi * tile:(qi + 1) * tile, :] = jnp.where(first, o[:tile], o[tile:]).astype(BF16)


def _attention(q, k, v, bias_rows):
    b, s, _ = q.shape
    tile = ATTN_TILE
    seq_spec = pl.BlockSpec((1, s, LANES), lambda h, bi: (bi, 0, h))
    return pl.pallas_call(
        functools.partial(_attn_kernel, tile=tile),
        grid=(WIDTH_A // LANES, b),
        in_specs=[seq_spec, seq_spec, seq_spec,
                  pl.BlockSpec((1, 2, 1, s + tile), lambda h, bi: (h, 0, 0, 0))],
        out_specs=seq_spec,
        out_shape=jax.ShapeDtypeStruct((b, s, WIDTH_A), BF16),
        scratch_shapes=[pltpu.VMEM((2, tile, s), F32), pltpu.VMEM((s, 2 * LANES), BF16)],
        compiler_params=_params("arbitrary", "arbitrary"),
        name="dilated_attention",
    )(q, k, v, bias_rows)


def _shift_rows_halo(a, tail, back, row8):
    halo = tail.shape[0]
    rolled = pltpu.roll(a, back, 0)
    head = jnp.where(row8 < back, pltpu.roll(tail, back, 0), rolled[:halo])
    return jnp.concatenate([head, rolled[halo:]], axis=0)


def _odd_in_kernel(x_ref, g_ref, w_hbm, cw_ref, cb_ref, wmix_ref, bg_ref,
                   xc_ref, q_ref, k_ref, v_ref, z_ref, gates_ref, gates_t_ref, tail_ref,
                   w_ref, stage_ref, sem_ref, *, layer):
    ts = x_ref.shape[1]
    halo = tail_ref.shape[0]
    cbw = ODD_CHANNEL_BLOCK
    row8 = lax.broadcasted_iota(jnp.int32, (halo, cbw), 0)

    @pl.when(pl.program_id(1) == 0)
    def _():
        tail_ref[...] = jnp.zeros_like(tail_ref)

    @pl.when((pl.program_id(0) == 0) & (pl.program_id(1) == 0))
    def _():
        _stage_weights([(w_hbm.at[layer], w_ref)], stage_ref, sem_ref)

    xn = _rms(x_ref[0], g_ref[...]).astype(BF16)
    gates = jnp.broadcast_to(bg_ref[...], (ts, LANES))
    n_blocks = INNER_C // cbw
    xm_next = _dot(xn, w_ref[:, :cbw])
    for cb in range(n_blocks):
        cs = slice(cb * cbw, (cb + 1) * cbw)
        xm = xm_next
        if cb + 1 < n_blocks:
            xm_next = _dot(xn, w_ref[:, (cb + 1) * cbw:(cb + 2) * cbw])
        tail = tail_ref[:, cs]
        tail_ref[:, cs] = xm[ts - halo:, :]
        w0, w1, w2, w3 = (cw_ref[tap:tap + 1, cs] for tap in range(CONV_WIDTH))
        xm_1 = _shift_rows_halo(xm, tail, 1, row8)
        far = xm * w1 + xm_1 * w0
        far_tail = tail * w1 + pltpu.roll(tail, 1, 0) * w0
        conv = xm * w3 + xm_1 * w2 + _shift_rows_halo(far, far_tail, 2, row8)
        xc16 = _silu(conv + cb_ref[:, cs]).astype(BF16)
        xm16 = xm.astype(BF16)
        xc_ref[0, :, cs] = xc16
        for gl in range(cbw // LANES):
            ls = slice(gl * LANES, (gl + 1) * LANES)
            gi = cb * (cbw // LANES) + gl
            sl = slice(gi * LANES, (gi + 1) * LANES)
            out = _dot(jnp.concatenate([xc16[:, ls], xm16[:, ls]], axis=1), wmix_ref[gi])
            q_ref[0, :, sl] = out[:, :LANES].astype(BF16)
            k_ref[0, :, sl] = out[:, LANES:2 * LANES].astype(BF16)
            v_ref[0, :, sl] = out[:, 2 * LANES:3 * LANES].astype(BF16)
            gates = gates + out[:, 3 * LANES:]
    gates_ref[0] = gates
    gates_t_ref[0] = gates.T
    z_ref[0] = _dot(xn, w_ref[:, INNER_C:]).astype(BF16)


def _block_diag_tiles(w):
    per_tile = LANES // QKV_BLOCK
    rows = w.reshape(-1, LANES, 1, QKV_BLOCK)
    tiled = jnp.broadcast_to(rows, (rows.shape[0], LANES, per_tile, QKV_BLOCK))
    group = np.arange(LANES) // QKV_BLOCK
    on_diagonal = jnp.asarray(group[:, None] == group[None, :], w.dtype)
    return tiled.reshape(-1, LANES, LANES) * on_diagonal


def _odd_in(x, gain, w_in, i, conv_w, conv_b, wq, wk, wv, w_gates, b_gates):
    b, s, _ = x.shape
    c = INNER_C
    ts = TOKEN_TILE
    n_gates = 2 * N_HEADS_C
    wq_t, wk_t, wv_t = _block_diag_tiles(wq), _block_diag_tiles(wk), _block_diag_tiles(wv)
    wg = w_gates.reshape(3, c // LANES, LANES, n_gates)
    compose = lambda w_t, g: jnp.einsum("tij,tjn->tin", w_t, g, precision=lax.Precision.HIGHEST)
    pad_gates = lambda g: jnp.pad(g, ((0, 0), (0, 0), (0, LANES - n_gates)))
    zeros = jnp.zeros_like(wq_t)
    wmix = jnp.concatenate([
        jnp.concatenate([wq_t, wk_t * HEAD_DIM_C ** -0.5, zeros,
                         pad_gates(compose(wq_t, wg[0]) + compose(wk_t, wg[1]))], axis=2),
        jnp.concatenate([zeros, zeros, wv_t, pad_gates(compose(wv_t, wg[2]))], axis=2)], axis=1)
    bg = jnp.pad(b_gates.reshape(1, n_gates), ((0, 0), (0, LANES - n_gates)))
    rows = lambda n: pl.BlockSpec((1, ts, n), lambda bi, si: (bi, si, 0))
    act = jax.ShapeDtypeStruct((b, s, c), BF16)
    return pl.pallas_call(
        functools.partial(_odd_in_kernel, layer=i),
        grid=(b, s // ts),
        in_specs=[rows(D_MODEL), _resident((1, D_MODEL)), pl.BlockSpec(memory_space=pl.ANY),
                  _resident((CONV_WIDTH, c)), _resident((1, c)),
                  _resident((c // LANES, 2 * LANES, 4 * LANES)), _resident((1, LANES))],
        out_specs=[rows(c), rows(c), rows(c), rows(c), rows(c), rows(LANES),
                   pl.BlockSpec((1, LANES, ts), lambda bi, si: (bi, 0, si))],
        out_shape=[act, act, act, act, act, jax.ShapeDtypeStruct((b, s, LANES), F32),
                   jax.ShapeDtypeStruct((b, LANES, s), F32)],
        scratch_shapes=[pltpu.VMEM((SUBLANES, c), F32), pltpu.VMEM((D_MODEL, 2 * c), BF16),
                        pltpu.VMEM((IN_STAGE_SLOTS, IN_STAGE_ROWS, 2 * c), F32),
                        pltpu.SemaphoreType.DMA((IN_STAGE_SLOTS,))],
        compiler_params=_params("arbitrary", "arbitrary"),
        name="odd_in",
    )(x, gain.reshape(1, D_MODEL), w_in, conv_w, conv_b.reshape(1, c), wmix.astype(BF16), bg)


def _log_sigmoid(x):
    return jnp.minimum(x, 0.0) - jnp.log(1.0 + jnp.exp(-jnp.abs(x)))


def _mlstm_head(q, k, v, ig_c, fg_c, ig_r, fg_r, c_ref, m_ref):
    chunk, e = q.shape
    v_ext = jnp.concatenate([v, jnp.ones((chunk, LANES), BF16)], axis=1)
    lf_c = _log_sigmoid(fg_c)
    lf_r = _log_sigmoid(fg_r)
    t_idx = lax.broadcasted_iota(jnp.int32, (chunk, chunk), 0)
    s_idx = lax.broadcasted_iota(jnp.int32, (chunk, chunk), 1)
    causal = s_idx <= t_idx
    bcum_c = jnp.sum(jnp.where(causal, lf_r, 0.0), axis=1, keepdims=True)
    bcum_r = jnp.sum(jnp.where(t_idx <= s_idx, lf_c, 0.0), axis=0, keepdims=True)
    b_last = jnp.sum(lf_r, axis=1, keepdims=True)
    d = jnp.where(causal, bcum_c - bcum_r + ig_r, MASKED)
    m_prev = m_ref[...]
    inter = bcum_c + m_prev
    m_t = jnp.maximum(inter, jnp.max(d, axis=1, keepdims=True))
    w_inter = jnp.exp(inter - m_t)
    s_qk = lax.dot_general(q, k, (((1,), (1,)), ((), ())), preferred_element_type=F32)
    s_qk = (s_qk * jnp.exp(d - m_t)).astype(BF16)
    state = c_ref[...]
    both = w_inter * _dot(q, state.astype(BF16)) + _dot(s_qk, v_ext)
    inv = 1.0 / jnp.maximum(jnp.abs(both[:, e:]), jnp.exp(-m_t))

    a_r = b_last - bcum_r + ig_r
    a_c = b_last - bcum_c + ig_c
    m_new = jnp.maximum(b_last + m_prev, jnp.max(a_r, axis=1, keepdims=True))
    decay = jnp.exp(b_last + m_prev - m_new)
    kw = k * jnp.exp(a_c - m_new).astype(BF16)
    c_ref[...] = decay * state + lax.dot_general(
        kw, v_ext, (((0,), (0,)), ((), ())), preferred_element_type=F32)
    m_ref[...] = m_new
    return both[:, :e], inv


def _mlstm_kernel(q_ref, k_ref, v_ref, gc_ref, gr_ref, og_ref, xc_ref, z_ref, skip_ref,
                  o_ref, c_ref, m_ref):
    @pl.when(pl.program_id(1) == 0)
    def _():
        c_ref[...] = jnp.zeros_like(c_ref)
        m_ref[...] = jnp.zeros_like(m_ref)

    e = HEAD_DIM_C
    nh = N_HEADS_C
    gcol = gc_ref[0]
    grow = gr_ref[0]
    for h in range(nh):
        sl = slice(h * e, (h + 1) * e)
        num, inv = _mlstm_head(q_ref[0, :, sl], k_ref[0, :, sl], v_ref[0, :, sl],
                               gcol[:, h:h + 1], gcol[:, nh + h:nh + h + 1],
                               grow[h:h + 1, :], grow[nh + h:nh + h + 1, :],
                               c_ref.at[h], m_ref.at[h])
        msq = jnp.mean(num * num, axis=1, keepdims=True)
        f = inv * lax.rsqrt(inv * inv * msq + RMS_EPS)
        hn = (num * jnp.concatenate([f] * (e // LANES), axis=1) * og_ref[:, sl]).astype(BF16)
        y = hn + skip_ref[:, sl].astype(BF16) * xc_ref[0, :, sl]
        o_ref[0, :, sl] = y * _silu(z_ref[0, :, sl])


def _mlstm(q, k, v, gates, gates_t, out_gain, xc, z, skip):
    b, s, c = q.shape
    chunk = MLSTM_CHUNK
    e = HEAD_DIM_C
    rows = pl.BlockSpec((1, chunk, c), lambda bi, ci: (bi, ci, 0))
    return pl.pallas_call(
        _mlstm_kernel,
        grid=(b, s // chunk),
        in_specs=[rows, rows, rows,
                  pl.BlockSpec((1, chunk, LANES), lambda bi, ci: (bi, ci, 0)),
                  pl.BlockSpec((1, 2 * N_HEADS_C, chunk), lambda bi, ci: (bi, 0, ci)),
                  _resident((1, c)), rows, rows, _resident((1, c))],
        out_specs=rows,
        out_shape=jax.ShapeDtypeStruct((b, s, c), BF16),
        scratch_shapes=[pltpu.VMEM((N_HEADS_C, e, e + LANES), F32),
                        pltpu.VMEM((N_HEADS_C, 1, 1), F32)],
        compiler_params=_params("parallel", "arbitrary"),
        name="mlstm_chunkwise",
    )(q, k, v, gates, gates_t, out_gain.reshape(1, c), xc, z, skip.reshape(1, c))


def kernel(x, rel_bias, norm_gains, ffn_w_gate, ffn_w_up, ffn_w_down, ev_w_in, ev_q_gain, ev_k_gain, ev_pool_w, ev_pool_scale, ev_w_out, od_w_in, od_conv_w, od_conv_b, od_wq, od_wk, od_wv, od_w_gates, od_b_gates, od_skip, od_out_gain, od_w_out):
    b, s, d = x.shape
    t = b * s
    x = x.reshape(t, d)
    bias_rows = _attn_bias_rows(rel_bias, s, ATTN_TILE)
    ffn_w = (ffn_w_gate, ffn_w_up, ffn_w_down)
    ev_w_out, od_w_out = ev_w_out.astype(BF16), od_w_out.astype(BF16)
    for layer in range(DEPTH):
        g = norm_gains[layer]
        i = layer // 2
        x = _ffn(x, g[0], *ffn_w, layer, 0)
        if layer % 2 == 0:
            q, k, v, pooled = _proj_even(x.reshape(b, s, d), g[1], ev_w_in, i, ev_q_gain[i],
                                         ev_k_gain[i], ev_pool_w[i], ev_pool_scale[i])
            attn = _attention(q, k, v, bias_rows)
            mix, w_out = (attn.reshape(t, WIDTH_A), pooled.reshape(t, WIDTH_B)), ev_w_out
        else:
            xc, q, k, v, z, gates, gates_t = _odd_in(
                x.reshape(b, s, d), g[1], od_w_in, i, od_conv_w[i], od_conv_b[i], od_wq[i],
                od_wk[i], od_wv[i], od_w_gates[i], od_b_gates[i])
            y = _mlstm(q, k, v, gates, gates_t, od_out_gain[i], xc, z, od_skip[i])
            mix, w_out = (y.reshape(t, INNER_C),), od_w_out
        x = _ffn(x, g[2], *ffn_w, layer, 1, mix, w_out, i)
    return x.reshape(b, s, d)
```

```python
import functools
import math

import numpy as np
import jax
import jax.numpy as jnp
from jax import lax
from jax.experimental import pallas as pl
from jax.experimental.pallas import tpu as pltpu

F32 = jnp.float32
BF16 = jnp.bfloat16

D_MODEL = 1024
DEPTH = 4
N_HEADS_A = 8
HEAD_DIM_A = 64
WIDTH_A = N_HEADS_A * HEAD_DIM_A
DILATED_PATTERNS = ((128, 1), (512, 4), (2048, 16))
N_POOL_GROUPS = 4
POOL_WINDOWS = (2, 4, 8, 16)
WIDTH_B = D_MODEL // 2
POOL_GROUP_DIM = WIDTH_B // N_POOL_GROUPS
IN_WIDTH_EVEN = 3 * WIDTH_A + WIDTH_B
REL_BUCKETS = 32
REL_MAX_DISTANCE = 2048
N_HEADS_C = 4
INNER_C = 2 * D_MODEL
HEAD_DIM_C = INNER_C // N_HEADS_C
CONV_WIDTH = 4
QKV_BLOCK = 4
D_FF = 256 * ((8 * D_MODEL // 3 + 255) // 256)
FFN_RESIDUAL = 0.5
RMS_EPS = 1e-6

LANES = 128
SUBLANES = 8
VMEM_LIMIT = 56 * 1024 * 1024
MASKED = -1e30

TOKEN_TILE = 512
EVEN_IN_TILE = 1024
ATTN_TILE = 256
MLSTM_CHUNK = 256
ODD_CHANNEL_BLOCK = 256


def _params(*sem):
    return pltpu.CompilerParams(dimension_semantics=sem, vmem_limit_bytes=VMEM_LIMIT)


def _resident(shape):
    nd = len(shape)
    return pl.BlockSpec(shape, lambda *_: (0,) * nd, pipeline_mode=pl.Buffered(1))


def _rms(x, gain):
    return x * lax.rsqrt(jnp.mean(x * x, axis=-1, keepdims=True) + RMS_EPS) * gain


def _silu(x):
    return x / (1.0 + jnp.exp(-x))


def _dot(a, b):
    return jnp.dot(a, b, preferred_element_type=F32)


FFN_STAGE_ROWS = 128
FFN_STAGE_SLOTS = 6
IN_STAGE_ROWS = 64
IN_STAGE_SLOTS = 4


def _stage_weights(sources, stage_ref, sem_ref):
    slots, step, _ = stage_ref.shape
    chunks = []
    for w_hbm, w16_ref in sources:
        rows, cols = w16_ref.shape
        for r0 in range(0, rows, step):
            chunks.append((w_hbm, w16_ref, r0, cols))

    def copy(i):
        w_hbm, _, r0, cols = chunks[i]
        return pltpu.make_async_copy(w_hbm.at[pl.ds(r0, step), :],
                                     stage_ref.at[i % slots, :, :cols], sem_ref.at[i % slots])

    for i in range(slots - 1):
        copy(i).start()
    for i, (_, w16_ref, r0, cols) in enumerate(chunks):
        if i + slots - 1 < len(chunks):
            copy(i + slots - 1).start()
        copy(i).wait()
        w16_ref[r0:r0 + step, :] = stage_ref[i % slots, :, :cols].astype(BF16)


def _ffn_kernel(*refs, n_mix, layer, half, mixer):
    x_ref, mix_refs, rest = refs[0], refs[1:1 + n_mix], refs[1 + n_mix:]
    if n_mix:
        wo_hbm, rest = rest[0], rest[1:]
    g_ref, wg_hbm, wu_hbm, wd_hbm, o_ref, wg_ref, wu_ref, wd_ref, stage_ref, sem_ref = rest[:10]
    sources = [(wg_hbm.at[layer, half], wg_ref), (wu_hbm.at[layer, half], wu_ref),
               (wd_hbm.at[layer, half], wd_ref)]
    if n_mix:
        wo_ref = rest[10]
        sources.append((wo_hbm.at[mixer], wo_ref))

    @pl.when(pl.program_id(0) == 0)
    def _():
        _stage_weights(sources, stage_ref, sem_ref)

    tm = x_ref.shape[0]
    parts = [slice(p * tm // 2, (p + 1) * tm // 2) for p in range(2)]
    xs = []
    for rs in parts:
        x = x_ref[rs, :]
        row = 0
        for m_ref in mix_refs:
            width = m_ref.shape[1]
            x = x + _dot(m_ref[rs, :], wo_ref[row:row + width, :])
            row += width
        xs.append(x)
    xn = [_rms(x, g_ref[...]).astype(BF16) for x in xs]
    h = [(_silu(_dot(a, wg_ref[...])) * _dot(a, wu_ref[...])).astype(BF16) for a in xn]
    for rs, x, hh in zip(parts, xs, h):
        o_ref[rs, :] = x + FFN_RESIDUAL * _dot(hh, wd_ref[...])


def _ffn(x, gain, w_gate, w_up, w_down, layer, half, mix=(), w_out=None, w_out_index=None):
    t = x.shape[0]
    tm = TOKEN_TILE
    row = lambda n: pl.BlockSpec((tm, n), lambda i: (i, 0))
    hbm = pl.BlockSpec(memory_space=pl.ANY)
    mix_specs = [row(m.shape[1]) for m in mix]
    mix_args = list(mix)
    mix_scratch = []
    if mix:
        mix_specs.append(hbm)
        mix_args.append(w_out)
        mix_scratch.append(pltpu.VMEM(w_out.shape[1:], BF16))
    return pl.pallas_call(
        functools.partial(_ffn_kernel, n_mix=len(mix), layer=layer, half=half,
                          mixer=w_out_index),
        grid=(t // tm,),
        in_specs=[row(D_MODEL), *mix_specs, _resident((1, D_MODEL)), hbm, hbm, hbm],
        out_specs=row(D_MODEL),
        out_shape=jax.ShapeDtypeStruct((t, D_MODEL), F32),
        scratch_shapes=[pltpu.VMEM((D_MODEL, D_FF), BF16), pltpu.VMEM((D_MODEL, D_FF), BF16),
                        pltpu.VMEM((D_FF, D_MODEL), BF16),
                        pltpu.VMEM((FFN_STAGE_SLOTS, FFN_STAGE_ROWS, D_FF), F32),
                        pltpu.SemaphoreType.DMA((FFN_STAGE_SLOTS,)), *mix_scratch],
        compiler_params=_params("arbitrary"),
        name="ffn",
    )(x, *mix_args, gain.reshape(1, D_MODEL), w_gate, w_up, w_down)


NORM_SLAB = 2 * LANES


def _head_sum_matrix():
    r = lax.broadcasted_iota(jnp.int32, (NORM_SLAB, NORM_SLAB), 0) // HEAD_DIM_A
    c = lax.broadcasted_iota(jnp.int32, (NORM_SLAB, NORM_SLAB), 1) // HEAD_DIM_A
    return jnp.where(r == c, 1.0, 0.0).astype(BF16)


def _head_norm(t, same_head, gain, scale):
    ss = _dot((t * t).astype(BF16), same_head)
    return t * lax.rsqrt(ss * (1.0 / HEAD_DIM_A) + RMS_EPS) * (gain * scale)


POOL_HALO = 16


def _pool_diffs(u, tail_ref, first_pos):
    ts = u.shape[0]
    pos = first_pos + lax.broadcasted_iota(jnp.int32, (ts, POOL_GROUP_DIM), 0)
    diffs = []
    for gi, window in enumerate(POOL_WINDOWS):
        gs = slice(gi * POOL_GROUP_DIM, (gi + 1) * POOL_GROUP_DIM)
        acc = jnp.concatenate([tail_ref[:, gs], u[:, gs]], axis=0)
        span = 1
        while span < window:
            acc = acc + pltpu.roll(acc, span, 0)
            span *= 2
        count = jnp.minimum(pos, window).astype(F32)
        diffs.append(acc[POOL_HALO:] / count - u[:, gs])
    tail_ref[...] = u[ts - POOL_HALO:, :]
    return jnp.concatenate(diffs, axis=1)


def _proj_even_kernel(x_ref, g_ref, w_hbm, qg_ref, kg_ref, wp_ref, ps_ref,
                      q_ref, k_ref, v_ref, p_ref, tail_ref, w_ref, stage_ref, sem_ref, *, layer):
    si = pl.program_id(1)
    ts = x_ref.shape[1]

    @pl.when((pl.program_id(0) == 0) & (si == 0))
    def _():
        _stage_weights([(w_hbm.at[layer], w_ref)], stage_ref, sem_ref)

    @pl.when(si == 0)
    def _():
        tail_ref[...] = jnp.zeros_like(tail_ref)

    xn = _rms(x_ref[0], g_ref[...]).astype(BF16)
    u = _dot(xn, w_ref[:, 3 * WIDTH_A:])
    diff = _pool_diffs(u, tail_ref, si * ts + 1).astype(BF16)
    same_head = _head_sum_matrix()
    q = _dot(xn, w_ref[:, :WIDTH_A])
    k = _dot(xn, w_ref[:, WIDTH_A:2 * WIDTH_A])
    v_ref[0] = _dot(xn, w_ref[:, 2 * WIDTH_A:3 * WIDTH_A]).astype(BF16)
    for j in range(WIDTH_A // NORM_SLAB):
        sl = slice(j * NORM_SLAB, (j + 1) * NORM_SLAB)
        q_ref[0, :, sl] = _head_norm(q[:, sl], same_head, qg_ref[...],
                                     HEAD_DIM_A ** -0.5).astype(BF16)
        k_ref[0, :, sl] = _head_norm(k[:, sl], same_head, kg_ref[...], 1.0).astype(BF16)
    for j in range(WIDTH_B // NORM_SLAB):
        sl = slice(j * NORM_SLAB, (j + 1) * NORM_SLAB)
        p_ref[0, :, sl] = (_dot(diff[:, sl], wp_ref[j]) * ps_ref[:, sl]).astype(BF16)


def _proj_even(x, gain, w_in, i, q_gain, k_gain, pool_w, pool_scale):
    b, s, _ = x.shape
    ts = EVEN_IN_TILE
    assert POOL_WINDOWS == (2, 4, 8, 16) and max(POOL_WINDOWS) <= POOL_HALO
    rows = lambda n: pl.BlockSpec((1, ts, n), lambda bi, si: (bi, si, 0))
    pair = lambda g: jnp.tile(g, NORM_SLAB // HEAD_DIM_A).reshape(1, NORM_SLAB)
    zero = jnp.zeros_like(pool_w[0])
    wp = jnp.stack([jnp.block([[pool_w[2 * j], zero], [zero, pool_w[2 * j + 1]]])
                    for j in range(N_POOL_GROUPS // 2)])
    act = jax.ShapeDtypeStruct((b, s, WIDTH_A), BF16)
    return pl.pallas_call(
        functools.partial(_proj_even_kernel, layer=i),
        grid=(b, s // ts),
        in_specs=[rows(D_MODEL), _resident((1, D_MODEL)), pl.BlockSpec(memory_space=pl.ANY),
                  _resident((1, NORM_SLAB)), _resident((1, NORM_SLAB)),
                  _resident((N_POOL_GROUPS // 2, NORM_SLAB, NORM_SLAB)), _resident((1, WIDTH_B))],
        out_specs=[rows(WIDTH_A), rows(WIDTH_A), rows(WIDTH_A), rows(WIDTH_B)],
        out_shape=[act, act, act, jax.ShapeDtypeStruct((b, s, WIDTH_B), BF16)],
        scratch_shapes=[pltpu.VMEM((POOL_HALO, WIDTH_B), F32),
                        pltpu.VMEM((D_MODEL, IN_WIDTH_EVEN), BF16),
                        pltpu.VMEM((IN_STAGE_SLOTS, IN_STAGE_ROWS, IN_WIDTH_EVEN), F32),
                        pltpu.SemaphoreType.DMA((IN_STAGE_SLOTS,))],
        compiler_params=_params("arbitrary", "arbitrary"),
        name="proj_even",
    )(x, gain.reshape(1, D_MODEL), w_in, pair(q_gain), pair(k_gain), wp.astype(BF16),
      pool_scale.reshape(1, WIDTH_B))


def _t5_bucket(distance):
    max_exact = REL_BUCKETS // 2
    d = jnp.maximum(distance.astype(F32), 1.0)
    large = max_exact + (jnp.log(d / max_exact) / math.log(REL_MAX_DISTANCE / max_exact)
                         * (REL_BUCKETS - max_exact)).astype(jnp.int32)
    large = jnp.minimum(large, REL_BUCKETS - 1)
    return jnp.where(distance < max_exact, distance, large)


def _attn_bias_rows(rel_bias, seq, tile):
    dist = np.arange(seq)
    mult = np.zeros(seq, np.int64)
    for window, dilation in DILATED_PATTERNS:
        mult += (dist % dilation == 0) & (dist <= window)
    log_mult = np.where(mult > 0, np.log(np.maximum(mult, 1)), MASKED).astype(np.float32)
    per_dist = rel_bias[_t5_bucket(jnp.asarray(dist, jnp.int32))].astype(F32)
    per_dist = jnp.where(jnp.asarray(mult > 0)[:, None], per_dist + log_mult[:, None], MASKED)
    per_dist = per_dist.T
    base = seq - tile
    heads = per_dist.shape[0]
    masked = jnp.full((heads, tile), MASKED, F32)
    v = jnp.concatenate([per_dist[:, base:], masked, per_dist[:, :base]], axis=1)
    w = jnp.roll(v[:, ::-1], 1, axis=1)
    return w.reshape(heads // 2, 2, 1, seq + tile)


def _attn_kernel(q_ref, k_ref, v_ref, w_ref, o_ref, tab_ref, vext_ref, *, tile):
    seq = q_ref.shape[1]
    nq = seq // tile

    vext_ref[:, :LANES] = v_ref[0]
    vext_ref[:, LANES:] = jnp.ones((seq, LANES), BF16)

    @pl.when(pl.program_id(1) == 0)
    def _():
        for hh in range(2):
            rows = jnp.broadcast_to(w_ref[0, hh], (tile, seq + tile))
            tab_ref[hh] = pltpu.roll(rows, 0, 1, stride=1, stride_axis=0)[:, :seq]

    first = lax.broadcasted_iota(jnp.int32, (tile, LANES), 1) < HEAD_DIM_A
    def logits(qi):
        n = (qi + 1) * tile
        off = (nq - 1 - qi) * tile
        q = q_ref[0, qi * tile:(qi + 1) * tile, :]
        zero = jnp.zeros_like(q)
        q2 = jnp.concatenate([jnp.where(first, q, zero), jnp.where(first, zero, q)], axis=0)
        s = lax.dot_general(q2, k_ref[0, :n, :], (((1,), (1,)), ((), ())),
                            preferred_element_type=F32)
        return s + jnp.concatenate([tab_ref[0, :, off:off + n], tab_ref[1, :, off:off + n]],
                                   axis=0)

    order = list(reversed(range(nq)))
    s_next = logits(order[0])
    for pos, qi in enumerate(order):
        n = (qi + 1) * tile
        s = s_next
        if pos + 1 < nq:
            s_next = logits(order[pos + 1])
        p = jnp.exp(s - jnp.max(s, axis=-1, keepdims=True))
        o = _dot(p.astype(BF16), vext_ref[:n, :])
        o = o[:, :LANES] / o[:, LANES:]
        o_ref[0, qi * tile:(qi + 1) * tile, :] = jnp.where(first, o[:tile], o[tile:]).astype(BF16)


def _attention(q, k, v, bias_rows):
    b, s, _ = q.shape
    tile = ATTN_TILE
    seq_spec = pl.BlockSpec((1, s, LANES), lambda h, bi: (bi, 0, h))
    return pl.pallas_call(
        functools.partial(_attn_kernel, tile=tile),
        grid=(WIDTH_A // LANES, b),
        in_specs=[seq_spec, seq_spec, seq_spec,
                  pl.BlockSpec((1, 2, 1, s + tile), lambda h, bi: (h, 0, 0, 0))],
        out_specs=seq_spec,
        out_shape=jax.ShapeDtypeStruct((b, s, WIDTH_A), BF16),
        scratch_shapes=[pltpu.VMEM((2, tile, s), F32), pltpu.VMEM((s, 2 * LANES), BF16)],
        compiler_params=_params("arbitrary", "arbitrary"),
        name="dilated_attention",
    )(q, k, v, bias_rows)


def _shift_rows_halo(a, tail, back, row8):
    halo = tail.shape[0]
    rolled = pltpu.roll(a, back, 0)
    head = jnp.where(row8 < back, pltpu.roll(tail, back, 0), rolled[:halo])
    return jnp.concatenate([head, rolled[halo:]], axis=0)


def _odd_in_kernel(x_ref, g_ref, w_hbm, cw_ref, cb_ref, wmix_ref, bg_ref,
                   xc_ref, q_ref, k_ref, v_ref, z_ref, gates_ref, gates_t_ref, tail_ref,
                   w_ref, stage_ref, sem_ref, *, layer):
    ts = x_ref.shape[1]
    halo = tail_ref.shape[0]
    cbw = ODD_CHANNEL_BLOCK
    row8 = lax.broadcasted_iota(jnp.int32, (halo, cbw), 0)

    @pl.when(pl.program_id(1) == 0)
    def _():
        tail_ref[...] = jnp.zeros_like(tail_ref)

    @pl.when((pl.program_id(0) == 0) & (pl.program_id(1) == 0))
    def _():
        _stage_weights([(w_hbm.at[layer], w_ref)], stage_ref, sem_ref)

    xn = _rms(x_ref[0], g_ref[...]).astype(BF16)
    gates = jnp.broadcast_to(bg_ref[...], (ts, LANES))
    n_blocks = INNER_C // cbw
    xm_next = _dot(xn, w_ref[:, :cbw])
    for cb in range(n_blocks):
        cs = slice(cb * cbw, (cb + 1) * cbw)
        xm = xm_next
        if cb + 1 < n_blocks:
            xm_next = _dot(xn, w_ref[:, (cb + 1) * cbw:(cb + 2) * cbw])
        tail = tail_ref[:, cs]
        tail_ref[:, cs] = xm[ts - halo:, :]
        w0, w1, w2, w3 = (cw_ref[tap:tap + 1, cs] for tap in range(CONV_WIDTH))
        xm_1 = _shift_rows_halo(xm, tail, 1, row8)
        far = xm * w1 + xm_1 * w0
        far_tail = tail * w1 + pltpu.roll(tail, 1, 0) * w0
        conv = xm * w3 + xm_1 * w2 + _shift_rows_halo(far, far_tail, 2, row8)
        xc16 = _silu(conv + cb_ref[:, cs]).astype(BF16)
        xm16 = xm.astype(BF16)
        xc_ref[0, :, cs] = xc16
        for gl in range(cbw // LANES):
            ls = slice(gl * LANES, (gl + 1) * LANES)
            gi = cb * (cbw // LANES) + gl
            sl = slice(gi * LANES, (gi + 1) * LANES)
            out = _dot(jnp.concatenate([xc16[:, ls], xm16[:, ls]], axis=1), wmix_ref[gi])
            q_ref[0, :, sl] = out[:, :LANES].astype(BF16)
            k_ref[0, :, sl] = out[:, LANES:2 * LANES].astype(BF16)
            v_ref[0, :, sl] = out[:, 2 * LANES:3 * LANES].astype(BF16)
            gates = gates + out[:, 3 * LANES:]
    gates_ref[0] = gates
    gates_t_ref[0] = gates.T
    z_ref[0] = _dot(xn, w_ref[:, INNER_C:]).astype(BF16)


def _block_diag_tiles(w):
    per_tile = LANES // QKV_BLOCK
    rows = w.reshape(-1, LANES, 1, QKV_BLOCK)
    tiled = jnp.broadcast_to(rows, (rows.shape[0], LANES, per_tile, QKV_BLOCK))
    group = np.arange(LANES) // QKV_BLOCK
    on_diagonal = jnp.asarray(group[:, None] == group[None, :], w.dtype)
    return tiled.reshape(-1, LANES, LANES) * on_diagonal


def _odd_in(x, gain, w_in, i, conv_w, conv_b, wq, wk, wv, w_gates, b_gates):
    b, s, _ = x.shape
    c = INNER_C
    ts = TOKEN_TILE
    n_gates = 2 * N_HEADS_C
    wq_t, wk_t, wv_t = _block_diag_tiles(wq), _block_diag_tiles(wk), _block_diag_tiles(wv)
    wg = w_gates.reshape(3, c // LANES, LANES, n_gates)
    compose = lambda w_t, g: jnp.einsum("tij,tjn->tin", w_t, g, precision=lax.Precision.HIGHEST)
    pad_gates = lambda g: jnp.pad(g, ((0, 0), (0, 0), (0, LANES - n_gates)))
    zeros = jnp.zeros_like(wq_t)
    wmix = jnp.concatenate([
        jnp.concatenate([wq_t, wk_t * HEAD_DIM_C ** -0.5, zeros,
                         pad_gates(compose(wq_t, wg[0]) + compose(wk_t, wg[1]))], axis=2),
        jnp.concatenate([zeros, zeros, wv_t, pad_gates(compose(wv_t, wg[2]))], axis=2)], axis=1)
    bg = jnp.pad(b_gates.reshape(1, n_gates), ((0, 0), (0, LANES - n_gates)))
    rows = lambda n: pl.BlockSpec((1, ts, n), lambda bi, si: (bi, si, 0))
    act = jax.ShapeDtypeStruct((b, s, c), BF16)
    return pl.pallas_call(
        functools.partial(_odd_in_kernel, layer=i),
        grid=(b, s // ts),
        in_specs=[rows(D_MODEL), _resident((1, D_MODEL)), pl.BlockSpec(memory_space=pl.ANY),
                  _resident((CONV_WIDTH, c)), _resident((1, c)),
                  _resident((c // LANES, 2 * LANES, 4 * LANES)), _resident((1, LANES))],
        out_specs=[rows(c), rows(c), rows(c), rows(c), rows(c), rows(LANES),
                   pl.BlockSpec((1, LANES, ts), lambda bi, si: (bi, 0, si))],
        out_shape=[act, act, act, act, act, jax.ShapeDtypeStruct((b, s, LANES), F32),
                   jax.ShapeDtypeStruct((b, LANES, s), F32)],
        scratch_shapes=[pltpu.VMEM((SUBLANES, c), F32), pltpu.VMEM((D_MODEL, 2 * c), BF16),
                        pltpu.VMEM((IN_STAGE_SLOTS, IN_STAGE_ROWS, 2 * c), F32),
                        pltpu.SemaphoreType.DMA((IN_STAGE_SLOTS,))],
        compiler_params=_params("arbitrary", "arbitrary"),
        name="odd_in",
    )(x, gain.reshape(1, D_MODEL), w_in, conv_w, conv_b.reshape(1, c), wmix.astype(BF16), bg)


def _log_sigmoid(x):
    return jnp.minimum(x, 0.0) - jnp.log(1.0 + jnp.exp(-jnp.abs(x)))


def _mlstm_head(q, k, v, ig_c, fg_c, ig_r, fg_r, c_ref, m_ref):
    chunk, e = q.shape
    v_ext = jnp.concatenate([v, jnp.ones((chunk, LANES), BF16)], axis=1)
    lf_c = _log_sigmoid(fg_c)
    lf_r = _log_sigmoid(fg_r)
    t_idx = lax.broadcasted_iota(jnp.int32, (chunk, chunk), 0)
    s_idx = lax.broadcasted_iota(jnp.int32, (chunk, chunk), 1)
    causal = s_idx <= t_idx
    bcum_c = jnp.sum(jnp.where(causal, lf_r, 0.0), axis=1, keepdims=True)
    bcum_r = jnp.sum(jnp.where(t_idx <= s_idx, lf_c, 0.0), axis=0, keepdims=True)
    b_last = jnp.sum(lf_r, axis=1, keepdims=True)
    d = jnp.where(causal, bcum_c - bcum_r + ig_r, MASKED)
    m_prev = m_ref[...]
    inter = bcum_c + m_prev
    m_t = jnp.maximum(inter, jnp.max(d, axis=1, keepdims=True))
    w_inter = jnp.exp(inter - m_t)
    s_qk = lax.dot_general(q, k, (((1,), (1,)), ((), ())), preferred_element_type=F32)
    s_qk = (s_qk * jnp.exp(d - m_t)).astype(BF16)
    state = c_ref[...]
    both = w_inter * _dot(q, state.astype(BF16)) + _dot(s_qk, v_ext)
    inv = 1.0 / jnp.maximum(jnp.abs(both[:, e:]), jnp.exp(-m_t))

    a_r = b_last - bcum_r + ig_r
    a_c = b_last - bcum_c + ig_c
    m_new = jnp.maximum(b_last + m_prev, jnp.max(a_r, axis=1, keepdims=True))
    decay = jnp.exp(b_last + m_prev - m_new)
    kw = k * jnp.exp(a_c - m_new).astype(BF16)
    c_ref[...] = decay * state + lax.dot_general(
        kw, v_ext, (((0,), (0,)), ((), ())), preferred_element_type=F32)
    m_ref[...] = m_new
    return both[:, :e], inv


def _mlstm_kernel(q_ref, k_ref, v_ref, gc_ref, gr_ref, og_ref, xc_ref, z_ref, skip_ref,
                  o_ref, c_ref, m_ref):
    @pl.when(pl.program_id(1) == 0)
    def _():
        c_ref[...] = jnp.zeros_like(c_ref)
        m_ref[...] = jnp.zeros_like(m_ref)

    e = HEAD_DIM_C
    nh = N_HEADS_C
    gcol = gc_ref[0]
    grow = gr_ref[0]
    for h in range(nh):
        sl = slice(h * e, (h + 1) * e)
        num, inv = _mlstm_head(q_ref[0, :, sl], k_ref[0, :, sl], v_ref[0, :, sl],
                               gcol[:, h:h + 1], gcol[:, nh + h:nh + h + 1],
                               grow[h:h + 1, :], grow[nh + h:nh + h + 1, :],
                               c_ref.at[h], m_ref.at[h])
        msq = jnp.mean(num * num, axis=1, keepdims=True)
        f = inv * lax.rsqrt(inv * inv * msq + RMS_EPS)
        hn = (num * jnp.concatenate([f] * (e // LANES), axis=1) * og_ref[:, sl]).astype(BF16)
        y = hn + skip_ref[:, sl].astype(BF16) * xc_ref[0, :, sl]
        o_ref[0, :, sl] = y * _silu(z_ref[0, :, sl])


def _mlstm(q, k, v, gates, gates_t, out_gain, xc, z, skip):
    b, s, c = q.shape
    chunk = MLSTM_CHUNK
    e = HEAD_DIM_C
    rows = pl.BlockSpec((1, chunk, c), lambda bi, ci: (bi, ci, 0))
    return pl.pallas_call(
        _mlstm_kernel,
        grid=(b, s // chunk),
        in_specs=[rows, rows, rows,
                  pl.BlockSpec((1, chunk, LANES), lambda bi, ci: (bi, ci, 0)),
                  pl.BlockSpec((1, 2 * N_HEADS_C, chunk), lambda bi, ci: (bi, 0, ci)),
                  _resident((1, c)), rows, rows, _resident((1, c))],
        out_specs=rows,
        out_shape=jax.ShapeDtypeStruct((b, s, c), BF16),
        scratch_shapes=[pltpu.VMEM((N_HEADS_C, e, e + LANES), F32),
                        pltpu.VMEM((N_HEADS_C, 1, 1), F32)],
        compiler_params=_params("parallel", "arbitrary"),
        name="mlstm_chunkwise",
    )(q, k, v, gates, gates_t, out_gain.reshape(1, c), xc, z, skip.reshape(1, c))


def kernel(x, rel_bias, norm_gains, ffn_w_gate, ffn_w_up, ffn_w_down, ev_w_in, ev_q_gain, ev_k_gain, ev_pool_w, ev_pool_scale, ev_w_out, od_w_in, od_conv_w, od_conv_b, od_wq, od_wk, od_wv, od_w_gates, od_b_gates, od_skip, od_out_gain, od_w_out):
    b, s, d = x.shape
    t = b * s
    x = x.reshape(t, d)
    bias_rows = _attn_bias_rows(rel_bias, s, ATTN_TILE)
    ffn_w = (ffn_w_gate, ffn_w_up, ffn_w_down)
    for layer in range(DEPTH):
        g = norm_gains[layer]
        i = layer // 2
        x = _ffn(x, g[0], *ffn_w, layer, 0)
        if layer % 2 == 0:
            q, k, v, pooled = _proj_even(x.reshape(b, s, d), g[1], ev_w_in, i, ev_q_gain[i],
                                         ev_k_gain[i], ev_pool_w[i], ev_pool_scale[i])
            attn = _attention(q, k, v, bias_rows)
            mix, w_out = (attn.reshape(t, WIDTH_A), pooled.reshape(t, WIDTH_B)), ev_w_out
        else:
            xc, q, k, v, z, gates, gates_t = _odd_in(
                x.reshape(b, s, d), g[1], od_w_in, i, od_conv_w[i], od_conv_b[i], od_wq[i],
                od_wk[i], od_wv[i], od_w_gates[i], od_b_gates[i])
            y = _mlstm(q, k, v, gates, gates_t, od_out_gain[i], xc, z, od_skip[i])
            mix, w_out = (y.reshape(t, INNER_C),), od_w_out
        x = _ffn(x, g[2], *ffn_w, layer, 1, mix, w_out, i)
    return x.reshape(b, s, d)
```

```python
import functools
import math

import numpy as np
import jax
import jax.numpy as jnp
from jax import lax
from jax.experimental import pallas as pl
from jax.experimental.pallas import tpu as pltpu

F32 = jnp.float32
BF16 = jnp.bfloat16

D_MODEL = 1024
DEPTH = 4
N_HEADS_A = 8
HEAD_DIM_A = 64
WIDTH_A = N_HEADS_A * HEAD_DIM_A
DILATED_PATTERNS = ((128, 1), (512, 4), (2048, 16))
N_POOL_GROUPS = 4
POOL_WINDOWS = (2, 4, 8, 16)
WIDTH_B = D_MODEL // 2
POOL_GROUP_DIM = WIDTH_B // N_POOL_GROUPS
IN_WIDTH_EVEN = 3 * WIDTH_A + WIDTH_B
REL_BUCKETS = 32
REL_MAX_DISTANCE = 2048
N_HEADS_C = 4
INNER_C = 2 * D_MODEL
HEAD_DIM_C = INNER_C // N_HEADS_C
CONV_WIDTH = 4
QKV_BLOCK = 4
D_FF = 256 * ((8 * D_MODEL // 3 + 255) // 256)
FFN_RESIDUAL = 0.5
RMS_EPS = 1e-6

LANES = 128
SUBLANES = 8
VMEM_LIMIT = 56 * 1024 * 1024
MASKED = -1e30

TOKEN_TILE = 512
EVEN_IN_TILE = 1024
ATTN_TILE = 256
MLSTM_CHUNK = 256
ODD_CHANNEL_BLOCK = 256


def _params(*sem):
    return pltpu.CompilerParams(dimension_semantics=sem, vmem_limit_bytes=VMEM_LIMIT)


def _resident(shape):
    nd = len(shape)
    return pl.BlockSpec(shape, lambda *_: (0,) * nd, pipeline_mode=pl.Buffered(1))


def _rms(x, gain):
    return x * lax.rsqrt(jnp.mean(x * x, axis=-1, keepdims=True) + RMS_EPS) * gain


def _silu(x):
    return x / (1.0 + jnp.exp(-x))


def _dot(a, b):
    return jnp.dot(a, b, preferred_element_type=F32)


FFN_STAGE_ROWS = 128
FFN_STAGE_SLOTS = 6
IN_STAGE_ROWS = 64
IN_STAGE_SLOTS = 4


def _stage_weights(sources, stage_ref, sem_ref):
    slots, step, _ = stage_ref.shape
    chunks = []
    for w_hbm, w16_ref in sources:
        rows, cols = w16_ref.shape
        for r0 in range(0, rows, step):
            chunks.append((w_hbm, w16_ref, r0, cols))

    def copy(i):
        w_hbm, _, r0, cols = chunks[i]
        return pltpu.make_async_copy(w_hbm.at[pl.ds(r0, step), :],
                                     stage_ref.at[i % slots, :, :cols], sem_ref.at[i % slots])

    for i in range(slots - 1):
        copy(i).start()
    for i, (_, w16_ref, r0, cols) in enumerate(chunks):
        if i + slots - 1 < len(chunks):
            copy(i + slots - 1).start()
        copy(i).wait()
        w16_ref[r0:r0 + step, :] = stage_ref[i % slots, :, :cols].astype(BF16)


def _ffn_kernel(*refs, n_mix, layer, half, mixer):
    x_ref, mix_refs, rest = refs[0], refs[1:1 + n_mix], refs[1 + n_mix:]
    if n_mix:
        wo_hbm, rest = rest[0], rest[1:]
    g_ref, wg_hbm, wu_hbm, wd_hbm, o_ref, wg_ref, wu_ref, wd_ref, stage_ref, sem_ref = rest[:10]
    sources = [(wg_hbm.at[layer, half], wg_ref), (wu_hbm.at[layer, half], wu_ref),
               (wd_hbm.at[layer, half], wd_ref)]
    if n_mix:
        wo_ref = rest[10]
        sources.append((wo_hbm.at[mixer], wo_ref))

    @pl.when(pl.program_id(0) == 0)
    def _():
        _stage_weights(sources, stage_ref, sem_ref)

    tm = x_ref.shape[0]
    parts = [slice(p * tm // 2, (p + 1) * tm // 2) for p in range(2)]
    xs = []
    for rs in parts:
        x = x_ref[rs, :]
        row = 0
        for m_ref in mix_refs:
            width = m_ref.shape[1]
            x = x + _dot(m_ref[rs, :], wo_ref[row:row + width, :])
            row += width
        xs.append(x)
    xn = [_rms(x, g_ref[...]).astype(BF16) for x in xs]
    h = [(_silu(_dot(a, wg_ref[...])) * _dot(a, wu_ref[...])).astype(BF16) for a in xn]
    for rs, x, hh in zip(parts, xs, h):
        o_ref[rs, :] = x + FFN_RESIDUAL * _dot(hh, wd_ref[...])


def _ffn(x, gain, w_gate, w_up, w_down, layer, half, mix=(), w_out=None, w_out_index=None):
    t = x.shape[0]
    tm = TOKEN_TILE
    row = lambda n: pl.BlockSpec((tm, n), lambda i: (i, 0))
    hbm = pl.BlockSpec(memory_space=pl.ANY)
    mix_specs = [row(m.shape[1]) for m in mix]
    mix_args = list(mix)
    mix_scratch = []
    if mix:
        mix_specs.append(hbm)
        mix_args.append(w_out)
        mix_scratch.append(pltpu.VMEM(w_out.shape[1:], BF16))
    return pl.pallas_call(
        functools.partial(_ffn_kernel, n_mix=len(mix), layer=layer, half=half,
                          mixer=w_out_index),
        grid=(t // tm,),
        in_specs=[row(D_MODEL), *mix_specs, _resident((1, D_MODEL)), hbm, hbm, hbm],
        out_specs=row(D_MODEL),
        out_shape=jax.ShapeDtypeStruct((t, D_MODEL), F32),
        scratch_shapes=[pltpu.VMEM((D_MODEL, D_FF), BF16), pltpu.VMEM((D_MODEL, D_FF), BF16),
                        pltpu.VMEM((D_FF, D_MODEL), BF16),
                        pltpu.VMEM((FFN_STAGE_SLOTS, FFN_STAGE_ROWS, D_FF), F32),
                        pltpu.SemaphoreType.DMA((FFN_STAGE_SLOTS,)), *mix_scratch],
        compiler_params=_params("arbitrary"),
        name="ffn",
    )(x, *mix_args, gain.reshape(1, D_MODEL), w_gate, w_up, w_down)


NORM_SLAB = 2 * LANES


def _head_sum_matrix():
    r = lax.broadcasted_iota(jnp.int32, (NORM_SLAB, NORM_SLAB), 0) // HEAD_DIM_A
    c = lax.broadcasted_iota(jnp.int32, (NORM_SLAB, NORM_SLAB), 1) // HEAD_DIM_A
    return jnp.where(r == c, 1.0, 0.0).astype(BF16)


def _head_norm(t, same_head, gain, scale):
    ss = _dot((t * t).astype(BF16), same_head)
    return t * lax.rsqrt(ss * (1.0 / HEAD_DIM_A) + RMS_EPS) * (gain * scale)


POOL_HALO = 16


def _pool_diffs(u, tail_ref, first_pos):
    ts = u.shape[0]
    pos = first_pos + lax.broadcasted_iota(jnp.int32, (ts, POOL_GROUP_DIM), 0)
    diffs = []
    for gi, window in enumerate(POOL_WINDOWS):
        gs = slice(gi * POOL_GROUP_DIM, (gi + 1) * POOL_GROUP_DIM)
        acc = jnp.concatenate([tail_ref[:, gs], u[:, gs]], axis=0)
        span = 1
        while span < window:
            acc = acc + pltpu.roll(acc, span, 0)
            span *= 2
        count = jnp.minimum(pos, window).astype(F32)
        diffs.append(acc[POOL_HALO:] / count - u[:, gs])
    tail_ref[...] = u[ts - POOL_HALO:, :]
    return jnp.concatenate(diffs, axis=1)


def _proj_even_kernel(x_ref, g_ref, w_hbm, qg_ref, kg_ref, wp_ref, ps_ref,
                      q_ref, k_ref, v_ref, p_ref, tail_ref, w_ref, stage_ref, sem_ref, *, layer):
    si = pl.program_id(1)
    ts = x_ref.shape[1]

    @pl.when((pl.program_id(0) == 0) & (si == 0))
    def _():
        _stage_weights([(w_hbm.at[layer], w_ref)], stage_ref, sem_ref)

    @pl.when(si == 0)
    def _():
        tail_ref[...] = jnp.zeros_like(tail_ref)

    xn = _rms(x_ref[0], g_ref[...]).astype(BF16)
    u = _dot(xn, w_ref[:, 3 * WIDTH_A:])
    diff = _pool_diffs(u, tail_ref, si * ts + 1).astype(BF16)
    same_head = _head_sum_matrix()
    q = _dot(xn, w_ref[:, :WIDTH_A])
    k = _dot(xn, w_ref[:, WIDTH_A:2 * WIDTH_A])
    v_ref[0] = _dot(xn, w_ref[:, 2 * WIDTH_A:3 * WIDTH_A]).astype(BF16)
    for j in range(WIDTH_A // NORM_SLAB):
        sl = slice(j * NORM_SLAB, (j + 1) * NORM_SLAB)
        q_ref[0, :, sl] = _head_norm(q[:, sl], same_head, qg_ref[...],
                                     HEAD_DIM_A ** -0.5).astype(BF16)
        k_ref[0, :, sl] = _head_norm(k[:, sl], same_head, kg_ref[...], 1.0).astype(BF16)
    for j in range(WIDTH_B // NORM_SLAB):
        sl = slice(j * NORM_SLAB, (j + 1) * NORM_SLAB)
        p_ref[0, :, sl] = (_dot(diff[:, sl], wp_ref[j]) * ps_ref[:, sl]).astype(BF16)


def _proj_even(x, gain, w_in, i, q_gain, k_gain, pool_w, pool_scale):
    b, s, _ = x.shape
    ts = EVEN_IN_TILE
    assert POOL_WINDOWS == (2, 4, 8, 16) and max(POOL_WINDOWS) <= POOL_HALO
    rows = lambda n: pl.BlockSpec((1, ts, n), lambda bi, si: (bi, si, 0))
    pair = lambda g: jnp.tile(g, NORM_SLAB // HEAD_DIM_A).reshape(1, NORM_SLAB)
    zero = jnp.zeros_like(pool_w[0])
    wp = jnp.stack([jnp.block([[pool_w[2 * j], zero], [zero, pool_w[2 * j + 1]]])
                    for j in range(N_POOL_GROUPS // 2)])
    act = jax.ShapeDtypeStruct((b, s, WIDTH_A), BF16)
    return pl.pallas_call(
        functools.partial(_proj_even_kernel, layer=i),
        grid=(b, s // ts),
        in_specs=[rows(D_MODEL), _resident((1, D_MODEL)), pl.BlockSpec(memory_space=pl.ANY),
                  _resident((1, NORM_SLAB)), _resident((1, NORM_SLAB)),
                  _resident((N_POOL_GROUPS // 2, NORM_SLAB, NORM_SLAB)), _resident((1, WIDTH_B))],
        out_specs=[rows(WIDTH_A), rows(WIDTH_A), rows(WIDTH_A), rows(WIDTH_B)],
        out_shape=[act, act, act, jax.ShapeDtypeStruct((b, s, WIDTH_B), BF16)],
        scratch_shapes=[pltpu.VMEM((POOL_HALO, WIDTH_B), F32),
                        pltpu.VMEM((D_MODEL, IN_WIDTH_EVEN), BF16),
                        pltpu.VMEM((IN_STAGE_SLOTS, IN_STAGE_ROWS, IN_WIDTH_EVEN), F32),
                        pltpu.SemaphoreType.DMA((IN_STAGE_SLOTS,))],
        compiler_params=_params("arbitrary", "arbitrary"),
        name="proj_even",
    )(x, gain.reshape(1, D_MODEL), w_in, pair(q_gain), pair(k_gain), wp.astype(BF16),
      pool_scale.reshape(1, WIDTH_B))


def _t5_bucket(distance):
    max_exact = REL_BUCKETS // 2
    d = jnp.maximum(distance.astype(F32), 1.0)
    large = max_exact + (jnp.log(d / max_exact) / math.log(REL_MAX_DISTANCE / max_exact)
                         * (REL_BUCKETS - max_exact)).astype(jnp.int32)
    large = jnp.minimum(large, REL_BUCKETS - 1)
    return jnp.where(distance < max_exact, distance, large)


def _attn_bias_rows(rel_bias, seq, tile):
    dist = np.arange(seq)
    mult = np.zeros(seq, np.int64)
    for window, dilation in DILATED_PATTERNS:
        mult += (dist % dilation == 0) & (dist <= window)
    log_mult = np.where(mult > 0, np.log(np.maximum(mult, 1)), MASKED).astype(np.float32)
    per_dist = rel_bias[_t5_bucket(jnp.asarray(dist, jnp.int32))].astype(F32)
    per_dist = jnp.where(jnp.asarray(mult > 0)[:, None], per_dist + log_mult[:, None], MASKED)
    per_dist = per_dist.T
    base = seq - tile
    heads = per_dist.shape[0]
    masked = jnp.full((heads, tile), MASKED, F32)
    v = jnp.concatenate([per_dist[:, base:], masked, per_dist[:, :base]], axis=1)
    w = jnp.roll(v[:, ::-1], 1, axis=1)
    return w.reshape(heads // 2, 2, 1, seq + tile)


def _attn_kernel(q_ref, k_ref, v_ref, w_ref, o_ref, tab_ref, vext_ref, *, tile):
    seq = q_ref.shape[1]
    nq = seq // tile

    vext_ref[:, :LANES] = v_ref[0]
    vext_ref[:, LANES:] = jnp.ones((seq, LANES), BF16)

    @pl.when(pl.program_id(1) == 0)
    def _():
        for hh in range(2):
            rows = jnp.broadcast_to(w_ref[0, hh], (tile, seq + tile))
            tab_ref[hh] = pltpu.roll(rows, 0, 1, stride=1, stride_axis=0)[:, :seq]

    first = lax.broadcasted_iota(jnp.int32, (tile, LANES), 1) < HEAD_DIM_A
    def logits(qi):
        n = (qi + 1) * tile
        off = (nq - 1 - qi) * tile
        q = q_ref[0, qi * tile:(qi + 1) * tile, :]
        zero = jnp.zeros_like(q)
        q2 = jnp.concatenate([jnp.where(first, q, zero), jnp.where(first, zero, q)], axis=0)
        s = lax.dot_general(q2, k_ref[0, :n, :], (((1,), (1,)), ((), ())),
                            preferred_element_type=F32)
        return s + jnp.concatenate([tab_ref[0, :, off:off + n], tab_ref[1, :, off:off + n]],
                                   axis=0)

    order = list(reversed(range(nq)))
    s_next = logits(order[0])
    for pos, qi in enumerate(order):
        n = (qi + 1) * tile
        s = s_next
        if pos + 1 < nq:
            s_next = logits(order[pos + 1])
        p = jnp.exp(s - jnp.max(s, axis=-1, keepdims=True))
        o = _dot(p.astype(BF16), vext_ref[:n, :])
        o = o[:, :LANES] / o[:, LANES:]
        o_ref[0, qi * tile:(qi + 1) * tile, :] = jnp.where(first, o[:tile], o[tile:]).astype(BF16)


def _attention(q, k, v, bias_rows):
    b, s, _ = q.shape
    tile = ATTN_TILE
    seq_spec = pl.BlockSpec((1, s, LANES), lambda h, bi: (bi, 0, h))
    return pl.pallas_call(
        functools.partial(_attn_kernel, tile=tile),
        grid=(WIDTH_A // LANES, b),
        in_specs=[seq_spec, seq_spec, seq_spec,
                  pl.BlockSpec((1, 2, 1, s + tile), lambda h, bi: (h, 0, 0, 0))],
        out_specs=seq_spec,
        out_shape=jax.ShapeDtypeStruct((b, s, WIDTH_A), BF16),
        scratch_shapes=[pltpu.VMEM((2, tile, s), F32), pltpu.VMEM((s, 2 * LANES), BF16)],
        compiler_params=_params("arbitrary", "arbitrary"),
        name="dilated_attention",
    )(q, k, v, bias_rows)


def _shift_rows_halo(a, tail, back, row8):
    halo = tail.shape[0]
    rolled = pltpu.roll(a, back, 0)
    head = jnp.where(row8 < back, pltpu.roll(tail, back, 0), rolled[:halo])
    return jnp.concatenate([head, rolled[halo:]], axis=0)


def _odd_in_kernel(x_ref, g_ref, w_hbm, cw_ref, cb_ref, xc_ref, xm_ref, z_ref, tail_ref,
                   w_ref, stage_ref, sem_ref, *, layer):
    ts = x_ref.shape[1]
    halo = tail_ref.shape[0]
    cbw = ODD_CHANNEL_BLOCK
    row8 = lax.broadcasted_iota(jnp.int32, (halo, cbw), 0)

    @pl.when(pl.program_id(1) == 0)
    def _():
        tail_ref[...] = jnp.zeros_like(tail_ref)

    @pl.when((pl.program_id(0) == 0) & (pl.program_id(1) == 0))
    def _():
        _stage_weights([(w_hbm.at[layer], w_ref)], stage_ref, sem_ref)

    xn = _rms(x_ref[0], g_ref[...]).astype(BF16)
    n_blocks = INNER_C // cbw
    xm_next = _dot(xn, w_ref[:, :cbw])
    for cb in range(n_blocks):
        cs = slice(cb * cbw, (cb + 1) * cbw)
        xm = xm_next
        if cb + 1 < n_blocks:
            xm_next = _dot(xn, w_ref[:, (cb + 1) * cbw:(cb + 2) * cbw])
        tail = tail_ref[:, cs]
        tail_ref[:, cs] = xm[ts - halo:, :]
        w0, w1, w2, w3 = (cw_ref[tap:tap + 1, cs] for tap in range(CONV_WIDTH))
        xm_1 = _shift_rows_halo(xm, tail, 1, row8)
        far = xm * w1 + xm_1 * w0
        far_tail = tail * w1 + pltpu.roll(tail, 1, 0) * w0
        conv = xm * w3 + xm_1 * w2 + _shift_rows_halo(far, far_tail, 2, row8)
        xc_ref[0, :, cs] = _silu(conv + cb_ref[:, cs]).astype(BF16)
        xm_ref[0, :, cs] = xm.astype(BF16)
    z_ref[0] = _dot(xn, w_ref[:, INNER_C:]).astype(BF16)


def _block_diag_tiles(w):
    per_tile = LANES // QKV_BLOCK
    rows = w.reshape(-1, LANES, 1, QKV_BLOCK)
    tiled = jnp.broadcast_to(rows, (rows.shape[0], LANES, per_tile, QKV_BLOCK))
    group = np.arange(LANES) // QKV_BLOCK
    on_diagonal = jnp.asarray(group[:, None] == group[None, :], w.dtype)
    return tiled.reshape(-1, LANES, LANES) * on_diagonal


def _mix_weights(wq, wk, wv, w_gates, b_gates):
    c = INNER_C
    n_gates = 2 * N_HEADS_C
    wq_t, wk_t, wv_t = _block_diag_tiles(wq), _block_diag_tiles(wk), _block_diag_tiles(wv)
    wg = w_gates.reshape(3, c // LANES, LANES, n_gates)
    compose = lambda w_t, g: jnp.einsum("tij,tjn->tin", w_t, g, precision=lax.Precision.HIGHEST)
    pad_gates = lambda g: jnp.pad(g, ((0, 0), (0, 0), (0, LANES - n_gates)))
    zeros = jnp.zeros_like(wq_t)
    wmix = jnp.concatenate([
        jnp.concatenate([wq_t, wk_t * HEAD_DIM_C ** -0.5, zeros,
                         pad_gates(compose(wq_t, wg[0]) + compose(wk_t, wg[1]))], axis=2),
        jnp.concatenate([zeros, zeros, wv_t, pad_gates(compose(wv_t, wg[2]))], axis=2)], axis=1)
    bg = jnp.pad(b_gates.reshape(1, n_gates), ((0, 0), (0, LANES - n_gates)))
    return wmix.astype(BF16), bg


def _odd_in(x, gain, w_in, i, conv_w, conv_b):
    b, s, _ = x.shape
    c = INNER_C
    ts = TOKEN_TILE
    rows = lambda n: pl.BlockSpec((1, ts, n), lambda bi, si: (bi, si, 0))
    act = jax.ShapeDtypeStruct((b, s, c), BF16)
    return pl.pallas_call(
        functools.partial(_odd_in_kernel, layer=i),
        grid=(b, s // ts),
        in_specs=[rows(D_MODEL), _resident((1, D_MODEL)), pl.BlockSpec(memory_space=pl.ANY),
                  _resident((CONV_WIDTH, c)), _resident((1, c))],
        out_specs=[rows(c), rows(c), rows(c)],
        out_shape=[act, act, act],
        scratch_shapes=[pltpu.VMEM((SUBLANES, c), F32), pltpu.VMEM((D_MODEL, 2 * c), BF16),
                        pltpu.VMEM((IN_STAGE_SLOTS, IN_STAGE_ROWS, 2 * c), F32),
                        pltpu.SemaphoreType.DMA((IN_STAGE_SLOTS,))],
        compiler_params=_params("arbitrary", "arbitrary"),
        name="odd_in",
    )(x, gain.reshape(1, D_MODEL), w_in, conv_w, conv_b.reshape(1, c))


def _log_sigmoid(x):
    return jnp.minimum(x, 0.0) - jnp.log(1.0 + jnp.exp(-jnp.abs(x)))


def _mlstm_head(q, k, v, ig_c, fg_c, ig_r, fg_r, c_ref, m_ref):
    chunk, e = q.shape
    v_ext = jnp.concatenate([v, jnp.ones((chunk, LANES), BF16)], axis=1)
    lf_c = _log_sigmoid(fg_c)
    lf_r = _log_sigmoid(fg_r)
    t_idx = lax.broadcasted_iota(jnp.int32, (chunk, chunk), 0)
    s_idx = lax.broadcasted_iota(jnp.int32, (chunk, chunk), 1)
    causal = s_idx <= t_idx
    bcum_c = jnp.sum(jnp.where(causal, lf_r, 0.0), axis=1, keepdims=True)
    bcum_r = jnp.sum(jnp.where(t_idx <= s_idx, lf_c, 0.0), axis=0, keepdims=True)
    b_last = jnp.sum(lf_r, axis=1, keepdims=True)
    d = jnp.where(causal, bcum_c - bcum_r + ig_r, MASKED)
    m_prev = m_ref[...]
    inter = bcum_c + m_prev
    m_t = jnp.maximum(inter, jnp.max(d, axis=1, keepdims=True))
    w_inter = jnp.exp(inter - m_t)
    s_qk = lax.dot_general(q, k, (((1,), (1,)), ((), ())), preferred_element_type=F32)
    s_qk = (s_qk * jnp.exp(d - m_t)).astype(BF16)
    state = c_ref[...]
    both = w_inter * _dot(q, state.astype(BF16)) + _dot(s_qk, v_ext)
    inv = 1.0 / jnp.maximum(jnp.abs(both[:, e:]), jnp.exp(-m_t))

    a_r = b_last - bcum_r + ig_r
    a_c = b_last - bcum_c + ig_c
    m_new = jnp.maximum(b_last + m_prev, jnp.max(a_r, axis=1, keepdims=True))
    decay = jnp.exp(b_last + m_prev - m_new)
    kw = k * jnp.exp(a_c - m_new).astype(BF16)
    c_ref[...] = decay * state + lax.dot_general(
        kw, v_ext, (((0,), (0,)), ((), ())), preferred_element_type=F32)
    m_ref[...] = m_new
    return both[:, :e], inv


def _mlstm_kernel(xc_ref, xm_ref, z_ref, wmix_ref, bg_ref, og_ref, skip_ref,
                  o_ref, c_ref, m_ref, q_ref, k_ref, v_ref):
    @pl.when(pl.program_id(1) == 0)
    def _():
        c_ref[...] = jnp.zeros_like(c_ref)
        m_ref[...] = jnp.zeros_like(m_ref)

    e = HEAD_DIM_C
    nh = N_HEADS_C
    chunk = xc_ref.shape[1]
    gcol = jnp.broadcast_to(bg_ref[...], (chunk, LANES))
    for gi in range(INNER_C // LANES):
        sl = slice(gi * LANES, (gi + 1) * LANES)
        out = _dot(jnp.concatenate([xc_ref[0, :, sl], xm_ref[0, :, sl]], axis=1), wmix_ref[gi])
        q_ref[:, sl] = out[:, :LANES].astype(BF16)
        k_ref[:, sl] = out[:, LANES:2 * LANES].astype(BF16)
        v_ref[:, sl] = out[:, 2 * LANES:3 * LANES].astype(BF16)
        gcol = gcol + out[:, 3 * LANES:]
    grow = gcol.T
    for h in range(nh):
        sl = slice(h * e, (h + 1) * e)
        num, inv = _mlstm_head(q_ref[:, sl], k_ref[:, sl], v_ref[:, sl],
                               gcol[:, h:h + 1], gcol[:, nh + h:nh + h + 1],
                               grow[h:h + 1, :], grow[nh + h:nh + h + 1, :],
                               c_ref.at[h], m_ref.at[h])
        msq = jnp.mean(num * num, axis=1, keepdims=True)
        f = inv * lax.rsqrt(inv * inv * msq + RMS_EPS)
        hn = (num * jnp.concatenate([f] * (e // LANES), axis=1) * og_ref[:, sl]).astype(BF16)
        y = hn + skip_ref[:, sl].astype(BF16) * xc_ref[0, :, sl]
        o_ref[0, :, sl] = y * _silu(z_ref[0, :, sl])


def _mlstm(xc, xm, z, wmix, bg, out_gain, skip):
    b, s, c = xc.shape
    chunk = MLSTM_CHUNK
    e = HEAD_DIM_C
    rows = pl.BlockSpec((1, chunk, c), lambda bi, ci: (bi, ci, 0))
    return pl.pallas_call(
        _mlstm_kernel,
        grid=(b, s // chunk),
        in_specs=[rows, rows, rows, _resident(wmix.shape), _resident((1, LANES)),
                  _resident((1, c)), _resident((1, c))],
        out_specs=rows,
        out_shape=jax.ShapeDtypeStruct((b, s, c), BF16),
        scratch_shapes=[pltpu.VMEM((N_HEADS_C, e, e + LANES), F32),
                        pltpu.VMEM((N_HEADS_C, 1, 1), F32),
                        pltpu.VMEM((chunk, c), BF16), pltpu.VMEM((chunk, c), BF16),
                        pltpu.VMEM((chunk, c), BF16)],
        compiler_params=_params("parallel", "arbitrary"),
        name="mlstm_chunkwise",
    )(xc, xm, z, wmix, bg, out_gain.reshape(1, c), skip.reshape(1, c))


def kernel(x, rel_bias, norm_gains, ffn_w_gate, ffn_w_up, ffn_w_down, ev_w_in, ev_q_gain, ev_k_gain, ev_pool_w, ev_pool_scale, ev_w_out, od_w_in, od_conv_w, od_conv_b, od_wq, od_wk, od_wv, od_w_gates, od_b_gates, od_skip, od_out_gain, od_w_out):
    b, s, d = x.shape
    t = b * s
    x = x.reshape(t, d)
    bias_rows = _attn_bias_rows(rel_bias, s, ATTN_TILE)
    ffn_w = (ffn_w_gate, ffn_w_up, ffn_w_down)
    for layer in range(DEPTH):
        g = norm_gains[layer]
        i = layer // 2
        x = _ffn(x, g[0], *ffn_w, layer, 0)
        if layer % 2 == 0:
            q, k, v, pooled = _proj_even(x.reshape(b, s, d), g[1], ev_w_in, i, ev_q_gain[i],
                                         ev_k_gain[i], ev_pool_w[i], ev_pool_scale[i])
            attn = _attention(q, k, v, bias_rows)
            mix, w_out = (attn.reshape(t, WIDTH_A), pooled.reshape(t, WIDTH_B)), ev_w_out
        else:
            xc, xm, z = _odd_in(x.reshape(b, s, d), g[1], od_w_in, i, od_conv_w[i], od_conv_b[i])
            wmix, bg = _mix_weights(od_wq[i], od_wk[i], od_wv[i], od_w_gates[i], od_b_gates[i])
            y = _mlstm(xc, xm, z, wmix, bg, od_out_gain[i], od_skip[i])
            mix, w_out = (y.reshape(t, INNER_C),), od_w_out
        x = _ffn(x, g[2], *ffn_w, layer, 1, mix, w_out, i)
    return x.reshape(b, s, d)
```

```python
import functools
import math

import numpy as np
import jax
import jax.numpy as jnp
from jax import lax
from jax.experimental import pallas as pl
from jax.experimental.pallas import tpu as pltpu

F32 = jnp.float32
BF16 = jnp.bfloat16

D_MODEL = 1024
DEPTH = 4
N_HEADS_A = 8
HEAD_DIM_A = 64
WIDTH_A = N_HEADS_A * HEAD_DIM_A
DILATED_PATTERNS = ((128, 1), (512, 4), (2048, 16))
N_POOL_GROUPS = 4
POOL_WINDOWS = (2, 4, 8, 16)
WIDTH_B = D_MODEL // 2
POOL_GROUP_DIM = WIDTH_B // N_POOL_GROUPS
IN_WIDTH_EVEN = 3 * WIDTH_A + WIDTH_B
REL_BUCKETS = 32
REL_MAX_DISTANCE = 2048
N_HEADS_C = 4
INNER_C = 2 * D_MODEL
HEAD_DIM_C = INNER_C // N_HEADS_C
CONV_WIDTH = 4
QKV_BLOCK = 4
D_FF = 256 * ((8 * D_MODEL // 3 + 255) // 256)
FFN_RESIDUAL = 0.5
RMS_EPS = 1e-6

LANES = 128
SUBLANES = 8
VMEM_LIMIT = 56 * 1024 * 1024
MASKED = -1e30

TOKEN_TILE = 512
EVEN_IN_TILE = 1024
ATTN_TILE = 256
MLSTM_CHUNK = 256
ODD_CHANNEL_BLOCK = 256


def _params(*sem):
    return pltpu.CompilerParams(dimension_semantics=sem, vmem_limit_bytes=VMEM_LIMIT)


def _resident(shape):
    nd = len(shape)
    return pl.BlockSpec(shape, lambda *_: (0,) * nd, pipeline_mode=pl.Buffered(1))


def _rms(x, gain):
    return x * lax.rsqrt(jnp.mean(x * x, axis=-1, keepdims=True) + RMS_EPS) * gain


def _silu(x):
    return x / (1.0 + jnp.exp(-x))


def _dot(a, b):
    return jnp.dot(a, b, preferred_element_type=F32)


FFN_STAGE_ROWS = 128
FFN_STAGE_SLOTS = 6
IN_STAGE_ROWS = 64
IN_STAGE_SLOTS = 4


def _stage_weights(sources, stage_ref, sem_ref):
    slots, step, _ = stage_ref.shape
    chunks = []
    for w_hbm, w16_ref in sources:
        rows, cols = w16_ref.shape
        for r0 in range(0, rows, step):
            chunks.append((w_hbm, w16_ref, r0, cols))

    def copy(i):
        w_hbm, _, r0, cols = chunks[i]
        return pltpu.make_async_copy(w_hbm.at[pl.ds(r0, step), :],
                                     stage_ref.at[i % slots, :, :cols], sem_ref.at[i % slots])

    for i in range(slots - 1):
        copy(i).start()
    for i, (_, w16_ref, r0, cols) in enumerate(chunks):
        if i + slots - 1 < len(chunks):
            copy(i + slots - 1).start()
        copy(i).wait()
        w16_ref[r0:r0 + step, :] = stage_ref[i % slots, :, :cols].astype(BF16)


def _ffn_kernel(*refs, n_mix, layer, half, mixer):
    x_ref, mix_refs, rest = refs[0], refs[1:1 + n_mix], refs[1 + n_mix:]
    if n_mix:
        wo_hbm, rest = rest[0], rest[1:]
    g_ref, wg_hbm, wu_hbm, wd_hbm, o_ref, wg_ref, wu_ref, wd_ref, stage_ref, sem_ref = rest[:10]
    sources = [(wg_hbm.at[layer, half], wg_ref), (wu_hbm.at[layer, half], wu_ref),
               (wd_hbm.at[layer, half], wd_ref)]
    if n_mix:
        wo_ref = rest[10]
        sources.append((wo_hbm.at[mixer], wo_ref))

    @pl.when(pl.program_id(0) == 0)
    def _():
        _stage_weights(sources, stage_ref, sem_ref)

    tm = x_ref.shape[0]
    parts = [slice(p * tm // 2, (p + 1) * tm // 2) for p in range(2)]
    xs = []
    for rs in parts:
        x = x_ref[rs, :]
        row = 0
        for m_ref in mix_refs:
            width = m_ref.shape[1]
            x = x + _dot(m_ref[rs, :], wo_ref[row:row + width, :])
            row += width
        xs.append(x)
    xn = [_rms(x, g_ref[...]).astype(BF16) for x in xs]
    h = [(_silu(_dot(a, wg_ref[...])) * _dot(a, wu_ref[...])).astype(BF16) for a in xn]
    for rs, x, hh in zip(parts, xs, h):
        o_ref[rs, :] = x + FFN_RESIDUAL * _dot(hh, wd_ref[...])


def _ffn(x, gain, w_gate, w_up, w_down, layer, half, mix=(), w_out=None, w_out_index=None):
    t = x.shape[0]
    tm = TOKEN_TILE
    row = lambda n: pl.BlockSpec((tm, n), lambda i: (i, 0))
    hbm = pl.BlockSpec(memory_space=pl.ANY)
    mix_specs = [row(m.shape[1]) for m in mix]
    mix_args = list(mix)
    mix_scratch = []
    if mix:
        mix_specs.append(hbm)
        mix_args.append(w_out)
        mix_scratch.append(pltpu.VMEM(w_out.shape[1:], BF16))
    return pl.pallas_call(
        functools.partial(_ffn_kernel, n_mix=len(mix), layer=layer, half=half,
                          mixer=w_out_index),
        grid=(t // tm,),
        in_specs=[row(D_MODEL), *mix_specs, _resident((1, D_MODEL)), hbm, hbm, hbm],
        out_specs=row(D_MODEL),
        out_shape=jax.ShapeDtypeStruct((t, D_MODEL), F32),
        scratch_shapes=[pltpu.VMEM((D_MODEL, D_FF), BF16), pltpu.VMEM((D_MODEL, D_FF), BF16),
                        pltpu.VMEM((D_FF, D_MODEL), BF16),
                        pltpu.VMEM((FFN_STAGE_SLOTS, FFN_STAGE_ROWS, D_FF), F32),
                        pltpu.SemaphoreType.DMA((FFN_STAGE_SLOTS,)), *mix_scratch],
        compiler_params=_params("arbitrary"),
        name="ffn",
    )(x, *mix_args, gain.reshape(1, D_MODEL), w_gate, w_up, w_down)


NORM_SLAB = 2 * LANES


def _head_sum_matrix():
    r = lax.broadcasted_iota(jnp.int32, (NORM_SLAB, NORM_SLAB), 0) // HEAD_DIM_A
    c = lax.broadcasted_iota(jnp.int32, (NORM_SLAB, NORM_SLAB), 1) // HEAD_DIM_A
    return jnp.where(r == c, 1.0, 0.0).astype(BF16)


def _head_norm(t, same_head, gain, scale):
    ss = _dot((t * t).astype(BF16), same_head)
    return t * lax.rsqrt(ss * (1.0 / HEAD_DIM_A) + RMS_EPS) * (gain * scale)


POOL_HALO = 16


def _pool_diffs(u, tail_ref, first_pos):
    ts = u.shape[0]
    pos = first_pos + lax.broadcasted_iota(jnp.int32, (ts, POOL_GROUP_DIM), 0)
    diffs = []
    for gi, window in enumerate(POOL_WINDOWS):
        gs = slice(gi * POOL_GROUP_DIM, (gi + 1) * POOL_GROUP_DIM)
        acc = jnp.concatenate([tail_ref[:, gs], u[:, gs]], axis=0)
        span = 1
        while span < window:
            acc = acc + pltpu.roll(acc, span, 0)
            span *= 2
        count = jnp.minimum(pos, window).astype(F32)
        diffs.append(acc[POOL_HALO:] / count - u[:, gs])
    tail_ref[...] = u[ts - POOL_HALO:, :]
    return jnp.concatenate(diffs, axis=1)


def _proj_even_kernel(x_ref, g_ref, w_hbm, qg_ref, kg_ref, wp_ref, ps_ref,
                      q_ref, k_ref, v_ref, p_ref, tail_ref, w_ref, stage_ref, sem_ref, *, layer):
    si = pl.program_id(1)
    ts = x_ref.shape[1]

    @pl.when((pl.program_id(0) == 0) & (si == 0))
    def _():
        _stage_weights([(w_hbm.at[layer], w_ref)], stage_ref, sem_ref)

    @pl.when(si == 0)
    def _():
        tail_ref[...] = jnp.zeros_like(tail_ref)

    xn = _rms(x_ref[0], g_ref[...]).astype(BF16)
    u = _dot(xn, w_ref[:, 3 * WIDTH_A:])
    diff = _pool_diffs(u, tail_ref, si * ts + 1).astype(BF16)
    same_head = _head_sum_matrix()
    q = _dot(xn, w_ref[:, :WIDTH_A])
    k = _dot(xn, w_ref[:, WIDTH_A:2 * WIDTH_A])
    v_ref[0] = _dot(xn, w_ref[:, 2 * WIDTH_A:3 * WIDTH_A]).astype(BF16)
    for j in range(WIDTH_A // NORM_SLAB):
        sl = slice(j * NORM_SLAB, (j + 1) * NORM_SLAB)
        q_ref[0, :, sl] = _head_norm(q[:, sl], same_head, qg_ref[...],
                                     HEAD_DIM_A ** -0.5).astype(BF16)
        k_ref[0, :, sl] = _head_norm(k[:, sl], same_head, kg_ref[...], 1.0).astype(BF16)
    for j in range(WIDTH_B // NORM_SLAB):
        sl = slice(j * NORM_SLAB, (j + 1) * NORM_SLAB)
        p_ref[0, :, sl] = (_dot(diff[:, sl], wp_ref[j]) * ps_ref[:, sl]).astype(BF16)


def _proj_even(x, gain, w_in, i, q_gain, k_gain, pool_w, pool_scale):
    b, s, _ = x.shape
    ts = EVEN_IN_TILE
    assert POOL_WINDOWS == (2, 4, 8, 16) and max(POOL_WINDOWS) <= POOL_HALO
    rows = lambda n: pl.BlockSpec((1, ts, n), lambda bi, si: (bi, si, 0))
    pair = lambda g: jnp.tile(g, NORM_SLAB // HEAD_DIM_A).reshape(1, NORM_SLAB)
    zero = jnp.zeros_like(pool_w[0])
    wp = jnp.stack([jnp.block([[pool_w[2 * j], zero], [zero, pool_w[2 * j + 1]]])
                    for j in range(N_POOL_GROUPS // 2)])
    act = jax.ShapeDtypeStruct((b, s, WIDTH_A), BF16)
    return pl.pallas_call(
        functools.partial(_proj_even_kernel, layer=i),
        grid=(b, s // ts),
        in_specs=[rows(D_MODEL), _resident((1, D_MODEL)), pl.BlockSpec(memory_space=pl.ANY),
                  _resident((1, NORM_SLAB)), _resident((1, NORM_SLAB)),
                  _resident((N_POOL_GROUPS // 2, NORM_SLAB, NORM_SLAB)), _resident((1, WIDTH_B))],
        out_specs=[rows(WIDTH_A), rows(WIDTH_A), rows(WIDTH_A), rows(WIDTH_B)],
        out_shape=[act, act, act, jax.ShapeDtypeStruct((b, s, WIDTH_B), BF16)],
        scratch_shapes=[pltpu.VMEM((POOL_HALO, WIDTH_B), F32),
                        pltpu.VMEM((D_MODEL, IN_WIDTH_EVEN), BF16),
                        pltpu.VMEM((IN_STAGE_SLOTS, IN_STAGE_ROWS, IN_WIDTH_EVEN), F32),
                        pltpu.SemaphoreType.DMA((IN_STAGE_SLOTS,))],
        compiler_params=_params("arbitrary", "arbitrary"),
        name="proj_even",
    )(x, gain.reshape(1, D_MODEL), w_in, pair(q_gain), pair(k_gain), wp.astype(BF16),
      pool_scale.reshape(1, WIDTH_B))


def _t5_bucket(distance):
    max_exact = REL_BUCKETS // 2
    d = jnp.maximum(distance.astype(F32), 1.0)
    large = max_exact + (jnp.log(d / max_exact) / math.log(REL_MAX_DISTANCE / max_exact)
                         * (REL_BUCKETS - max_exact)).astype(jnp.int32)
    large = jnp.minimum(large, REL_BUCKETS - 1)
    return jnp.where(distance < max_exact, distance, large)


def _attn_bias_rows(rel_bias, seq, tile):
    dist = np.arange(seq)
    mult = np.zeros(seq, np.int64)
    for window, dilation in DILATED_PATTERNS:
        mult += (dist % dilation == 0) & (dist <= window)
    log_mult = np.where(mult > 0, np.log(np.maximum(mult, 1)), MASKED).astype(np.float32)
    per_dist = rel_bias[_t5_bucket(jnp.asarray(dist, jnp.int32))].astype(F32)
    per_dist = jnp.where(jnp.asarray(mult > 0)[:, None], per_dist + log_mult[:, None], MASKED)
    per_dist = per_dist.T
    base = seq - tile
    heads = per_dist.shape[0]
    masked = jnp.full((heads, tile), MASKED, F32)
    v = jnp.concatenate([per_dist[:, base:], masked, per_dist[:, :base]], axis=1)
    w = jnp.roll(v[:, ::-1], 1, axis=1)
    return w.reshape(heads // 2, 2, 1, seq + tile)


def _attn_kernel(q_ref, k_ref, v_ref, w_ref, o_ref, tab_ref, vext_ref, *, tile):
    seq = q_ref.shape[1]
    nq = seq // tile

    vext_ref[:, :LANES] = v_ref[0]
    vext_ref[:, LANES:] = jnp.ones((seq, LANES), BF16)

    @pl.when(pl.program_id(1) == 0)
    def _():
        for hh in range(2):
            rows = jnp.broadcast_to(w_ref[0, hh], (tile, seq + tile))
            tab_ref[hh] = pltpu.roll(rows, 0, 1, stride=1, stride_axis=0)[:, :seq]

    first = lax.broadcasted_iota(jnp.int32, (tile, LANES), 1) < HEAD_DIM_A
    def logits(qi):
        n = (qi + 1) * tile
        off = (nq - 1 - qi) * tile
        q = q_ref[0, qi * tile:(qi + 1) * tile, :]
        zero = jnp.zeros_like(q)
        q2 = jnp.concatenate([jnp.where(first, q, zero), jnp.where(first, zero, q)], axis=0)
        s = lax.dot_general(q2, k_ref[0, :n, :], (((1,), (1,)), ((), ())),
                            preferred_element_type=F32)
        return s + jnp.concatenate([tab_ref[0, :, off:off + n], tab_ref[1, :, off:off + n]],
                                   axis=0)

    order = list(reversed(range(nq)))
    def values(qi, p):
        o = _dot(p, vext_ref[:(qi + 1) * tile, :])
        o = o[:, :LANES] / o[:, LANES:]
        o_ref[0, qi * tile:(qi + 1) * tile, :] = jnp.where(first, o[:tile], o[tile:]).astype(BF16)

    s_next = logits(order[0])
    pending = None
    for pos, qi in enumerate(order):
        s = s_next
        if pos + 1 < nq:
            s_next = logits(order[pos + 1])
        p = jnp.exp(s - jnp.max(s, axis=-1, keepdims=True)).astype(BF16)
        if pending is not None:
            values(*pending)
        pending = (qi, p)
    values(*pending)


def _attention(q, k, v, bias_rows):
    b, s, _ = q.shape
    tile = ATTN_TILE
    seq_spec = pl.BlockSpec((1, s, LANES), lambda h, bi: (bi, 0, h))
    return pl.pallas_call(
        functools.partial(_attn_kernel, tile=tile),
        grid=(WIDTH_A // LANES, b),
        in_specs=[seq_spec, seq_spec, seq_spec,
                  pl.BlockSpec((1, 2, 1, s + tile), lambda h, bi: (h, 0, 0, 0))],
        out_specs=seq_spec,
        out_shape=jax.ShapeDtypeStruct((b, s, WIDTH_A), BF16),
        scratch_shapes=[pltpu.VMEM((2, tile, s), F32), pltpu.VMEM((s, 2 * LANES), BF16)],
        compiler_params=_params("arbitrary", "arbitrary"),
        name="dilated_attention",
    )(q, k, v, bias_rows)


def _shift_rows_halo(a, tail, back, row8):
    halo = tail.shape[0]
    rolled = pltpu.roll(a, back, 0)
    head = jnp.where(row8 < back, pltpu.roll(tail, back, 0), rolled[:halo])
    return jnp.concatenate([head, rolled[halo:]], axis=0)


def _odd_in_kernel(x_ref, g_ref, w_hbm, cw_ref, cb_ref, wmix_ref, bg_ref,
                   xc_ref, q_ref, k_ref, v_ref, z_ref, gates_ref, gates_t_ref, tail_ref,
                   w_ref, stage_ref, sem_ref, *, layer):
    ts = x_ref.shape[1]
    halo = tail_ref.shape[0]
    cbw = ODD_CHANNEL_BLOCK
    row8 = lax.broadcasted_iota(jnp.int32, (halo, cbw), 0)

    @pl.when(pl.program_id(1) == 0)
    def _():
        tail_ref[...] = jnp.zeros_like(tail_ref)

    @pl.when((pl.program_id(0) == 0) & (pl.program_id(1) == 0))
    def _():
        _stage_weights([(w_hbm.at[layer], w_ref)], stage_ref, sem_ref)

    xn = _rms(x_ref[0], g_ref[...]).astype(BF16)
    gates = jnp.broadcast_to(bg_ref[...], (ts, LANES))
    n_blocks = INNER_C // cbw
    xm_next = _dot(xn, w_ref[:, :cbw])
    for cb in range(n_blocks):
        cs = slice(cb * cbw, (cb + 1) * cbw)
        xm = xm_next
        if cb + 1 < n_blocks:
            xm_next = _dot(xn, w_ref[:, (cb + 1) * cbw:(cb + 2) * cbw])
        tail = tail_ref[:, cs]
        tail_ref[:, cs] = xm[ts - halo:, :]
        w0, w1, w2, w3 = (cw_ref[tap:tap + 1, cs] for tap in range(CONV_WIDTH))
        xm_1 = _shift_rows_halo(xm, tail, 1, row8)
        far = xm * w1 + xm_1 * w0
        far_tail = tail * w1 + pltpu.roll(tail, 1, 0) * w0
        conv = xm * w3 + xm_1 * w2 + _shift_rows_halo(far, far_tail, 2, row8)
        xc16 = _silu(conv + cb_ref[:, cs]).astype(BF16)
        xm16 = xm.astype(BF16)
        xc_ref[0, :, cs] = xc16
        for gl in range(cbw // LANES):
            ls = slice(gl * LANES, (gl + 1) * LANES)
            gi = cb * (cbw // LANES) + gl
            sl = slice(gi * LANES, (gi + 1) * LANES)
            out = _dot(jnp.concatenate([xc16[:, ls], xm16[:, ls]], axis=1), wmix_ref[gi])
            q_ref[0, :, sl] = out[:, :LANES].astype(BF16)
            k_ref[0, :, sl] = out[:, LANES:2 * LANES].astype(BF16)
            v_ref[0, :, sl] = out[:, 2 * LANES:3 * LANES].astype(BF16)
            gates = gates + out[:, 3 * LANES:]
    gates_ref[0] = gates
    gates_t_ref[0] = gates.T
    z_ref[0] = _dot(xn, w_ref[:, INNER_C:]).astype(BF16)


def _block_diag_tiles(w):
    per_tile = LANES // QKV_BLOCK
    rows = w.reshape(-1, LANES, 1, QKV_BLOCK)
    tiled = jnp.broadcast_to(rows, (rows.shape[0], LANES, per_tile, QKV_BLOCK))
    group = np.arange(LANES) // QKV_BLOCK
    on_diagonal = jnp.asarray(group[:, None] == group[None, :], w.dtype)
    return tiled.reshape(-1, LANES, LANES) * on_diagonal


def _odd_in(x, gain, w_in, i, conv_w, conv_b, wq, wk, wv, w_gates, b_gates):
    b, s, _ = x.shape
    c = INNER_C
    ts = TOKEN_TILE
    n_gates = 2 * N_HEADS_C
    wq_t, wk_t, wv_t = _block_diag_tiles(wq), _block_diag_tiles(wk), _block_diag_tiles(wv)
    wg = w_gates.reshape(3, c // LANES, LANES, n_gates)
    compose = lambda w_t, g: jnp.einsum("tij,tjn->tin", w_t, g, precision=lax.Precision.HIGHEST)
    pad_gates = lambda g: jnp.pad(g, ((0, 0), (0, 0), (0, LANES - n_gates)))
    zeros = jnp.zeros_like(wq_t)
    wmix = jnp.concatenate([
        jnp.concatenate([wq_t, wk_t * HEAD_DIM_C ** -0.5, zeros,
                         pad_gates(compose(wq_t, wg[0]) + compose(wk_t, wg[1]))], axis=2),
        jnp.concatenate([zeros, zeros, wv_t, pad_gates(compose(wv_t, wg[2]))], axis=2)], axis=1)
    bg = jnp.pad(b_gates.reshape(1, n_gates), ((0, 0), (0, LANES - n_gates)))
    rows = lambda n: pl.BlockSpec((1, ts, n), lambda bi, si: (bi, si, 0))
    act = jax.ShapeDtypeStruct((b, s, c), BF16)
    return pl.pallas_call(
        functools.partial(_odd_in_kernel, layer=i),
        grid=(b, s // ts),
        in_specs=[rows(D_MODEL), _resident((1, D_MODEL)), pl.BlockSpec(memory_space=pl.ANY),
                  _resident((CONV_WIDTH, c)), _resident((1, c)),
                  _resident((c // LANES, 2 * LANES, 4 * LANES)), _resident((1, LANES))],
        out_specs=[rows(c), rows(c), rows(c), rows(c), rows(c), rows(LANES),
                   pl.BlockSpec((1, LANES, ts), lambda bi, si: (bi, 0, si))],
        out_shape=[act, act, act, act, act, jax.ShapeDtypeStruct((b, s, LANES), F32),
                   jax.ShapeDtypeStruct((b, LANES, s), F32)],
        scratch_shapes=[pltpu.VMEM((SUBLANES, c), F32), pltpu.VMEM((D_MODEL, 2 * c), BF16),
                        pltpu.VMEM((IN_STAGE_SLOTS, IN_STAGE_ROWS, 2 * c), F32),
                        pltpu.SemaphoreType.DMA((IN_STAGE_SLOTS,))],
        compiler_params=_params("arbitrary", "arbitrary"),
        name="odd_in",
    )(x, gain.reshape(1, D_MODEL), w_in, conv_w, conv_b.reshape(1, c), wmix.astype(BF16), bg)


def _log_sigmoid(x):
    return jnp.minimum(x, 0.0) - jnp.log(1.0 + jnp.exp(-jnp.abs(x)))


def _mlstm_head(q, k, v, ig_c, fg_c, ig_r, fg_r, c_ref, m_ref):
    chunk, e = q.shape
    v_ext = jnp.concatenate([v, jnp.ones((chunk, LANES), BF16)], axis=1)
    lf_c = _log_sigmoid(fg_c)
    lf_r = _log_sigmoid(fg_r)
    t_idx = lax.broadcasted_iota(jnp.int32, (chunk, chunk), 0)
    s_idx = lax.broadcasted_iota(jnp.int32, (chunk, chunk), 1)
    causal = s_idx <= t_idx
    bcum_c = jnp.sum(jnp.where(causal, lf_r, 0.0), axis=1, keepdims=True)
    bcum_r = jnp.sum(jnp.where(t_idx <= s_idx, lf_c, 0.0), axis=0, keepdims=True)
    b_last = jnp.sum(lf_r, axis=1, keepdims=True)
    d = jnp.where(causal, bcum_c - bcum_r + ig_r, MASKED)
    m_prev = m_ref[...]
    inter = bcum_c + m_prev
    m_t = jnp.maximum(inter, jnp.max(d, axis=1, keepdims=True))
    w_inter = jnp.exp(inter - m_t)
    s_qk = lax.dot_general(q, k, (((1,), (1,)), ((), ())), preferred_element_type=F32)
    s_qk = (s_qk * jnp.exp(d - m_t)).astype(BF16)
    state = c_ref[...]
    both = w_inter * _dot(q, state.astype(BF16)) + _dot(s_qk, v_ext)
    inv = 1.0 / jnp.maximum(jnp.abs(both[:, e:]), jnp.exp(-m_t))

    a_r = b_last - bcum_r + ig_r
    a_c = b_last - bcum_c + ig_c
    m_new = jnp.maximum(b_last + m_prev, jnp.max(a_r, axis=1, keepdims=True))
    decay = jnp.exp(b_last + m_prev - m_new)
    kw = k * jnp.exp(a_c - m_new).astype(BF16)
    c_ref[...] = decay * state + lax.dot_general(
        kw, v_ext, (((0,), (0,)), ((), ())), preferred_element_type=F32)
    m_ref[...] = m_new
    return both[:, :e], inv


def _mlstm_kernel(q_ref, k_ref, v_ref, gc_ref, gr_ref, og_ref, xc_ref, z_ref, skip_ref,
                  o_ref, c_ref, m_ref):
    @pl.when(pl.program_id(1) == 0)
    def _():
        c_ref[...] = jnp.zeros_like(c_ref)
        m_ref[...] = jnp.zeros_like(m_ref)

    e = HEAD_DIM_C
    nh = N_HEADS_C
    gcol = gc_ref[0]
    grow = gr_ref[0]
    for h in range(nh):
        sl = slice(h * e, (h + 1) * e)
        num, inv = _mlstm_head(q_ref[0, :, sl], k_ref[0, :, sl], v_ref[0, :, sl],
                               gcol[:, h:h + 1], gcol[:, nh + h:nh + h + 1],
                               grow[h:h + 1, :], grow[nh + h:nh + h + 1, :],
                               c_ref.at[h], m_ref.at[h])
        msq = jnp.mean(num * num, axis=1, keepdims=True)
        f = inv * lax.rsqrt(inv * inv * msq + RMS_EPS)
        hn = (num * jnp.concatenate([f] * (e // LANES), axis=1) * og_ref[:, sl]).astype(BF16)
        y = hn + skip_ref[:, sl].astype(BF16) * xc_ref[0, :, sl]
        o_ref[0, :, sl] = y * _silu(z_ref[0, :, sl])


def _mlstm(q, k, v, gates, gates_t, out_gain, xc, z, skip):
    b, s, c = q.shape
    chunk = MLSTM_CHUNK
    e = HEAD_DIM_C
    rows = pl.BlockSpec((1, chunk, c), lambda bi, ci: (bi, ci, 0))
    return pl.pallas_call(
        _mlstm_kernel,
        grid=(b, s // chunk),
        in_specs=[rows, rows, rows,
                  pl.BlockSpec((1, chunk, LANES), lambda bi, ci: (bi, ci, 0)),
                  pl.BlockSpec((1, 2 * N_HEADS_C, chunk), lambda bi, ci: (bi, 0, ci)),
                  _resident((1, c)), rows, rows, _resident((1, c))],
        out_specs=rows,
        out_shape=jax.ShapeDtypeStruct((b, s, c), BF16),
        scratch_shapes=[pltpu.VMEM((N_HEADS_C, e, e + LANES), F32),
                        pltpu.VMEM((N_HEADS_C, 1, 1), F32)],
        compiler_params=_params("parallel", "arbitrary"),
        name="mlstm_chunkwise",
    )(q, k, v, gates, gates_t, out_gain.reshape(1, c), xc, z, skip.reshape(1, c))


def kernel(x, rel_bias, norm_gains, ffn_w_gate, ffn_w_up, ffn_w_down, ev_w_in, ev_q_gain, ev_k_gain, ev_pool_w, ev_pool_scale, ev_w_out, od_w_in, od_conv_w, od_conv_b, od_wq, od_wk, od_wv, od_w_gates, od_b_gates, od_skip, od_out_gain, od_w_out):
    b, s, d = x.shape
    t = b * s
    x = x.reshape(t, d)
    bias_rows = _attn_bias_rows(rel_bias, s, ATTN_TILE)
    ffn_w = (ffn_w_gate, ffn_w_up, ffn_w_down)
    for layer in range(DEPTH):
        g = norm_gains[layer]
        i = layer // 2
        x = _ffn(x, g[0], *ffn_w, layer, 0)
        if layer % 2 == 0:
            q, k, v, pooled = _proj_even(x.reshape(b, s, d), g[1], ev_w_in, i, ev_q_gain[i],
                                         ev_k_gain[i], ev_pool_w[i], ev_pool_scale[i])
            attn = _attention(q, k, v, bias_rows)
            mix, w_out = (attn.reshape(t, WIDTH_A), pooled.reshape(t, WIDTH_B)), ev_w_out
        else:
            xc, q, k, v, z, gates, gates_t = _odd_in(
                x.reshape(b, s, d), g[1], od_w_in, i, od_conv_w[i], od_conv_b[i], od_wq[i],
                od_wk[i], od_wv[i], od_w_gates[i], od_b_gates[i])
            y = _mlstm(q, k, v, gates, gates_t, od_out_gain[i], xc, z, od_skip[i])
            mix, w_out = (y.reshape(t, INNER_C),), od_w_out
        x = _ffn(x, g[2], *ffn_w, layer, 1, mix, w_out, i)
    return x.reshape(b, s, d)
```
